```python
import math
import jax
import jax.numpy as jnp
from jax import lax
import numpy as np

D_MODEL = 2048
BATCH = 4
SEQ = 2048
DEPTH = 2

GRID_W = 64
CTX_LEN = 256
D_MIX = D_MODEL
N_MIXERS = 4
GROUP_W = D_MIX // N_MIXERS
ROPE_DIM = 64
ROPE_THETA = 10000.0
NORM_EPS = 1e-6
QUERY_BLOCK = 128

POOL_WINDOWS = (2, 4, 8, 16)
POOL_GROUP = GROUP_W // len(POOL_WINDOWS)

DIFF_HEADS = 4
DIFF_HD = ROPE_DIM
DIFF_VD = 2 * DIFF_HD
DIFF_QK_COLS = DIFF_HEADS * 2 * DIFF_HD
DIFF_COLS = 2 * DIFF_QK_COLS + DIFF_HEADS * DIFF_VD

MLA_HEADS = 4
MLA_NOPE = 128
MLA_VD = GROUP_W // MLA_HEADS
MLA_Q_LORA = 384
MLA_KV_LORA = 256
MLA_COLS = MLA_Q_LORA + MLA_KV_LORA + ROPE_DIM

CONV_CH = GROUP_W
CONV_K = 3
CONV_COLS = 3 * CONV_CH

D_FF = 5632
FFN_CONV_K = 3

IN_COLS = GROUP_W + DIFF_COLS + MLA_COLS + CONV_COLS
IN_SPLITS = (GROUP_W, GROUP_W + DIFF_COLS, GROUP_W + DIFF_COLS + MLA_COLS)

kernel_name = "hybrid_pool_diffattn_mla_shortconv_dit"


def rmsnorm(x, g):
    xf = x.astype(jnp.float32)
    y = xf * lax.rsqrt(jnp.mean(xf * xf, axis=-1, keepdims=True) + NORM_EPS)
    return (y * g.astype(jnp.float32)).astype(x.dtype)


def modulate(h, shift, scale):
    return h * (1.0 + scale) + shift


def axial_rope_tables(n_tokens, rot_dim):
    rows = n_tokens // GRID_W
    row = jnp.repeat(jnp.arange(rows), GRID_W).astype(jnp.float32)
    col = jnp.tile(jnp.arange(GRID_W), rows).astype(jnp.float32)
    n_freq = rot_dim // 4
    inv = ROPE_THETA ** (-jnp.arange(n_freq, dtype=jnp.float32) / n_freq)
    ang = jnp.concatenate([row[:, None] * inv, col[:, None] * inv], axis=-1)
    return jnp.cos(ang), jnp.sin(ang)


def apply_rope(x, cos, sin):
    half = x.shape[-1] // 2
    xf = x.astype(jnp.float32)
    x1, x2 = xf[..., :half], xf[..., half:]
    c, s = cos[None, :, None, :], sin[None, :, None, :]
    return jnp.concatenate([x1 * c - x2 * s, x1 * s + x2 * c], axis=-1).astype(x.dtype)


def dwconv_centred(u, w):
    k_w = w.shape[0]
    pad = k_w // 2
    n = u.shape[1]
    up = jnp.pad(u, ((0, 0), (pad, pad), (0, 0)))
    return sum(up[:, k:k + n] * w[k] for k in range(k_w))


def centred_mean_minus_self(u, window):
    n = u.shape[1]
    half = window // 2
    t = jnp.arange(n)
    lo = jnp.clip(t - half, 0, n)
    hi = jnp.clip(t + half, 0, n)
    uf = u.astype(jnp.float32)
    cs = jnp.concatenate([jnp.zeros_like(uf[:, :1]), jnp.cumsum(uf, axis=1)], axis=1)
    cnt = (hi - lo).astype(jnp.float32)[None, :, None]
    mean = (jnp.take(cs, hi, axis=1) - jnp.take(cs, lo, axis=1)) / cnt
    return (mean - uf).astype(u.dtype)


def pool_mixer(u, pool_w, pool_scale):
    b, n, _ = u.shape
    ug = u.reshape(b, n, len(POOL_WINDOWS), POOL_GROUP)
    pooled = jnp.stack([centred_mean_minus_self(ug[:, :, g], w) for g, w in enumerate(POOL_WINDOWS)], axis=2)
    y = jnp.einsum("blgc,gcd->blgd", pooled, pool_w).reshape(b, n, GROUP_W)
    return y * pool_scale


def gated_short_conv(u, conv_w):
    gate_b, gate_c, h = jnp.split(u, 3, axis=-1)
    return gate_b * dwconv_centred(gate_c * h, conv_w)


def attend(q, k, v, scale):
    s = jnp.einsum("bqhd,bkhd->bhqk", q, k).astype(jnp.float32) * scale
    p = jax.nn.softmax(s, axis=-1)
    return jnp.einsum("bhqk,bkhd->bqhd", p.astype(v.dtype), v)


def diff_attend(q1, q2, k1, k2, v, lam, scale):
    s1 = jnp.einsum("bqhd,bkhd->bhqk", q1, k1).astype(jnp.float32) * scale
    s2 = jnp.einsum("bqhd,bkhd->bhqk", q2, k2).astype(jnp.float32) * scale
    a = jax.nn.softmax(s1, axis=-1) - lam * jax.nn.softmax(s2, axis=-1)
    return jnp.einsum("bhqk,bkhd->bqhd", a.astype(v.dtype), v)


def sweep_query_blocks(fn, qs):
    b, n = qs[0].shape[:2]
    nb = n // QUERY_BLOCK
    blocks = tuple(q.reshape(b, nb, QUERY_BLOCK, *q.shape[2:]).swapaxes(0, 1) for q in qs)
    out = lax.map(lambda qb: fn(*qb), blocks)
    out = out.swapaxes(0, 1)
    return out.reshape(b, n, *out.shape[3:])


def diff_lambda_init(layer_idx):
    return 0.8 - 0.6 * math.exp(-0.3 * layer_idx)


def diff_attention_mixer(z_x, z_c, lam_params, subln_g, lam_init, cos, sin, last):
    lq1, lk1, lq2, lk2 = (lam_params[j].astype(jnp.float32) for j in range(4))
    lam = jnp.exp(jnp.sum(lq1 * lk1)) - jnp.exp(jnp.sum(lq2 * lk2)) + lam_init
    scale = DIFF_HD ** -0.5

    def split_qkv(z):
        b, n, _ = z.shape
        q, k, v = jnp.split(z, [DIFF_QK_COLS, 2 * DIFF_QK_COLS], axis=-1)
        q = q.reshape(b, n, DIFF_HEADS, 2, DIFF_HD)
        k = k.reshape(b, n, DIFF_HEADS, 2, DIFF_HD)
        return q[..., 0, :], q[..., 1, :], k[..., 0, :], k[..., 1, :], v.reshape(b, n, DIFF_HEADS, DIFF_VD)

    def post(o):
        b, n = o.shape[:2]
        return (rmsnorm(o, subln_g) * (1.0 - lam_init)).reshape(b, n, GROUP_W)

    q1x, q2x, k1x, k2x, vx = split_qkv(z_x)
    q1x, q2x, k1x, k2x = (apply_rope(t, cos, sin) for t in (q1x, q2x, k1x, k2x))
    q1c, q2c, k1c, k2c, vc = split_qkv(z_c)
    k1 = jnp.concatenate([k1x, k1c], axis=1)
    k2 = jnp.concatenate([k2x, k2c], axis=1)
    v = jnp.concatenate([vx, vc], axis=1)
    ox = sweep_query_blocks(lambda a, b_: diff_attend(a, b_, k1, k2, v, lam, scale), (q1x, q2x))
    oc = None if last else post(diff_attend(q1c, q2c, k1c, k2c, vc, lam, scale))
    return post(ox), oc


def mla_mixer(z_x, z_c, gq, w_uq, gkv, w_ukv, cos, sin, last):
    def project(z, rotate):
        b, n, _ = z.shape
        c_q, c_kv, k_rope = jnp.split(z, [MLA_Q_LORA, MLA_Q_LORA + MLA_KV_LORA], axis=-1)
        q = (rmsnorm(c_q, gq) @ w_uq).reshape(b, n, MLA_HEADS, MLA_NOPE + ROPE_DIM)
        kv = (rmsnorm(c_kv, gkv) @ w_ukv).reshape(b, n, MLA_HEADS, MLA_NOPE + MLA_VD)
        q_nope, q_rope = q[..., :MLA_NOPE], q[..., MLA_NOPE:]
        k_rope = k_rope[:, :, None, :]
        if rotate:
            q_rope = apply_rope(q_rope, cos, sin)
            k_rope = apply_rope(k_rope, cos, sin)
        q = jnp.concatenate([q_nope, q_rope], axis=-1)
        k = jnp.concatenate([kv[..., :MLA_NOPE], jnp.broadcast_to(k_rope, (b, n, MLA_HEADS, ROPE_DIM))], axis=-1)
        return q, k, kv[..., MLA_NOPE:]

    def merge(o):
        return o.reshape(o.shape[0], o.shape[1], GROUP_W)

    scale = (MLA_NOPE + ROPE_DIM) ** -0.5
    qx, kx, vx = project(z_x, True)
    qc, kc, vc = project(z_c, False)
    k = jnp.concatenate([kx, kc], axis=1)
    v = jnp.concatenate([vx, vc], axis=1)
    ox = sweep_query_blocks(lambda qb: attend(qb, k, v, scale), (qx,))
    oc = None if last else merge(attend(qc, kc, vc, scale))
    return merge(ox), oc


def conv_ffn(h, w_up, conv_w, w_down):
    u = dwconv_centred(h @ w_up, conv_w)
    gate, val = jnp.split(u, 2, axis=-1)
    return (jax.nn.silu(gate) * val) @ w_down


def trunk_layer(x, ctx, c, c_ctx, p, layer_idx, cos, sin, last):
    mod_x = (jax.nn.silu(c) @ p["ada_w"] + p["ada_b"])[:, None, :]
    mod_c = (jax.nn.silu(c_ctx) @ p["ada_w"] + p["ada_b"])[None, None, :]
    sh1x, sc1x, g1x, sh2x, sc2x, g2x = jnp.split(mod_x, 6, axis=-1)
    sh1c, sc1c, g1c, sh2c, sc2c, g2c = jnp.split(mod_c, 6, axis=-1)

    zx = modulate(rmsnorm(x, p["g_pre_mix"]), sh1x, sc1x) @ p["w_in"]
    zc = modulate(rmsnorm(ctx, p["g_pre_mix"]), sh1c, sc1c) @ p["w_in"]
    za_x, zb_x, zm_x, zd_x = jnp.split(zx, IN_SPLITS, axis=-1)
    za_c, zb_c, zm_c, zd_c = jnp.split(zc, IN_SPLITS, axis=-1)
    yb_x, yb_c = diff_attention_mixer(zb_x, zb_c, p["diff_lambda"], p["diff_subln_g"],
                                      diff_lambda_init(layer_idx), cos, sin, last)
    ym_x, ym_c = mla_mixer(zm_x, zm_c, p["mla_gq"], p["mla_w_uq"], p["mla_gkv"], p["mla_w_ukv"],
                           cos, sin, last)
    mix_x = jnp.concatenate([pool_mixer(za_x, p["pool_w"], p["pool_scale"]), yb_x, ym_x,
                             gated_short_conv(zd_x, p["conv_w"])], axis=-1)
    x = x + g1x * rmsnorm(mix_x @ p["w_out"], p["g_post_mix"])
    hx = modulate(rmsnorm(x, p["g_pre_ffn"]), sh2x, sc2x)
    x = x + g2x * rmsnorm(conv_ffn(hx, p["ffn_w_up"], p["ffn_conv_w"], p["ffn_w_down"]), p["g_post_ffn"])
    if last:
        return x, None
    mix_c = jnp.concatenate([pool_mixer(za_c, p["pool_w"], p["pool_scale"]), yb_c, ym_c,
                             gated_short_conv(zd_c, p["conv_w"])], axis=-1)
    ctx = ctx + g1c * rmsnorm(mix_c @ p["w_out"], p["g_post_mix"])
    hc = modulate(rmsnorm(ctx, p["g_pre_ffn"]), sh2c, sc2c)
    ctx = ctx + g2c * rmsnorm(conv_ffn(hc, p["ffn_w_up"], p["ffn_conv_w"], p["ffn_w_down"]), p["g_post_ffn"])
    return x, ctx


def setup_inputs(seed: int = 0) -> dict:
    key = jax.random.key(seed)
    ks = jax.random.split(key, 26)
    f32 = jnp.float32

    def nrm(k, shape, s):
        return jax.random.normal(k, shape, f32) * s

    def gain(k, shape):
        return 1.0 + 0.1 * jax.random.normal(k, shape, f32)

    return {
        "x": nrm(ks[0], (BATCH, SEQ, D_MODEL), 1.0),
        "c": nrm(ks[1], (BATCH, D_MODEL), 1.0),
        "ctx": nrm(ks[2], (BATCH, CTX_LEN, D_MODEL), 1.0),
        "c_ctx": nrm(ks[3], (D_MODEL,), 1.0),
        "ada_w": nrm(ks[4], (DEPTH, D_MODEL, 6 * D_MODEL), 0.5 * D_MODEL ** -0.5),
        "ada_b": nrm(ks[5], (DEPTH, 6 * D_MODEL), 0.05),
        "g_pre_mix": gain(ks[6], (DEPTH, D_MODEL)),
        "g_post_mix": gain(ks[7], (DEPTH, D_MODEL)),
        "g_pre_ffn": gain(ks[8], (DEPTH, D_MODEL)),
        "g_post_ffn": gain(ks[9], (DEPTH, D_MODEL)),
        "w_in": nrm(ks[10], (DEPTH, D_MODEL, IN_COLS), D_MODEL ** -0.5),
        "pool_w": nrm(ks[11], (DEPTH, len(POOL_WINDOWS), POOL_GROUP, POOL_GROUP), POOL_GROUP ** -0.5),
        "pool_scale": gain(ks[12], (DEPTH, GROUP_W)),
        "diff_lambda": nrm(ks[13], (DEPTH, 4, DIFF_HD), 0.1),
        "diff_subln_g": gain(ks[14], (DEPTH, DIFF_VD)),
        "mla_gq": gain(ks[15], (DEPTH, MLA_Q_LORA)),
        "mla_w_uq": nrm(ks[16], (DEPTH, MLA_Q_LORA, MLA_HEADS * (MLA_NOPE + ROPE_DIM)), MLA_Q_LORA ** -0.5),
        "mla_gkv": gain(ks[17], (DEPTH, MLA_KV_LORA)),
        "mla_w_ukv": nrm(ks[18], (DEPTH, MLA_KV_LORA, MLA_HEADS * (MLA_NOPE + MLA_VD)), MLA_KV_LORA ** -0.5),
        "conv_w": nrm(ks[19], (DEPTH, CONV_K, CONV_CH), CONV_K ** -0.5),
        "w_out": nrm(ks[20], (DEPTH, D_MIX, D_MODEL), D_MIX ** -0.5),
        "ffn_w_up": nrm(ks[21], (DEPTH, D_MODEL, 2 * D_FF), D_MODEL ** -0.5),
        "ffn_conv_w": nrm(ks[22], (DEPTH, FFN_CONV_K, 2 * D_FF), FFN_CONV_K ** -0.5),
        "ffn_w_down": nrm(ks[23], (DEPTH, D_FF, D_MODEL), D_FF ** -0.5),
    }


def reference(x, c, ctx, c_ctx, ada_w, ada_b, g_pre_mix, g_post_mix, g_pre_ffn, g_post_ffn, w_in,
              pool_w, pool_scale, diff_lambda, diff_subln_g, mla_gq, mla_w_uq, mla_gkv, mla_w_ukv,
              conv_w, w_out, ffn_w_up, ffn_conv_w, ffn_w_down):
    cos, sin = axial_rope_tables(x.shape[1], ROPE_DIM)
    for i in range(DEPTH):
        params = {
            "ada_w": ada_w[i], "ada_b": ada_b[i],
            "g_pre_mix": g_pre_mix[i], "g_post_mix": g_post_mix[i],
            "g_pre_ffn": g_pre_ffn[i], "g_post_ffn": g_post_ffn[i],
            "w_in": w_in[i], "pool_w": pool_w[i], "pool_scale": pool_scale[i],
            "diff_lambda": diff_lambda[i], "diff_subln_g": diff_subln_g[i],
            "mla_gq": mla_gq[i], "mla_w_uq": mla_w_uq[i], "mla_gkv": mla_gkv[i], "mla_w_ukv": mla_w_ukv[i],
            "conv_w": conv_w[i], "w_out": w_out[i],
            "ffn_w_up": ffn_w_up[i], "ffn_conv_w": ffn_conv_w[i], "ffn_w_down": ffn_w_down[i],
        }
        x, ctx = trunk_layer(x, ctx, c, c_ctx, params, i, cos, sin, i == DEPTH - 1)
    return x
```

```python
import functools
import math

import jax
import jax.numpy as jnp
from jax import lax
from jax.experimental import pallas as pl
from jax.experimental.pallas import tpu as pltpu

F32 = jnp.float32
BF16 = jnp.bfloat16

D_MODEL = 2048
BATCH = 4
SEQ = 2048
CTX_LEN = 256
GRID_W = 64
DEPTH = 2
NORM_EPS = 1e-6
ROPE_THETA = 10000.0
GROUP_W = 512
POOL_WINDOWS = (2, 4, 8, 16)
HEADS = 4
D_FF = 5632

NX = BATCH * SEQ
NC = BATCH * CTX_LEN
NT = NX + NC

COL_A, COL_Q, COL_K, COL_V = 0, 512, 1024, 1536
COL_DB, COL_DC, COL_DH = 2048, 2560, 3072
COL_CKV, COL_CQ, COL_KR = 3584, 3840, 4224
IN_COLS_PAD = 4352

LANE = 128
TM = 512
TS = 256
TQ = 256
TN_IN = IN_COLS_PAD // 2
TF = 512
TN_ADA = 1024
VMEM_LIMIT = 56 * 1024 * 1024


def _params(sem):
    return pltpu.CompilerParams(dimension_semantics=sem, vmem_limit_bytes=VMEM_LIMIT)


def _rms(x, g):
    ms = jnp.mean(x * x, axis=-1, keepdims=True)
    return x * lax.rsqrt(ms + NORM_EPS) * g


def _rope(v, cos, sin):
    lane = lax.broadcasted_iota(jnp.int32, v.shape, 1)
    sw = jnp.where((lane & 32) == 0, pltpu.roll(v, 96, 1), pltpu.roll(v, 32, 1))
    return v * cos + sw * sin


def _shift_rows(u, prev_row, next_row, start, end):
    n = u.shape[0]
    row = lax.broadcasted_iota(jnp.int32, (n, 1), 0)
    up = jnp.where(row == 0, prev_row, pltpu.roll(u, 1, 0))
    dn = jnp.where(row == n - 1, next_row, pltpu.roll(u, n - 1, 0))
    return jnp.where(start, 0.0, up), jnp.where(end, 0.0, dn)


def _seq_pos(tile, rows):
    r = tile * rows + lax.broadcasted_iota(jnp.int32, (rows, 1), 0)
    seqlen = jnp.where(r < NX, SEQ, CTX_LEN)
    return r & (seqlen - 1), seqlen


def _tile_edges(tile, rows):
    r0 = tile * rows
    seqlen = jnp.where(r0 < NX, SEQ, CTX_LEN)
    has_prev = jnp.where((r0 & (seqlen - 1)) == 0, 0.0, 1.0)
    has_next = jnp.where(((r0 + rows) & (seqlen - 1)) == 0, 0.0, 1.0)
    return has_prev, has_next


def _ada_kernel(c_ref, w_ref, b_ref, o_ref):
    c = c_ref[...]
    a = c * jax.nn.sigmoid(c)
    o_ref[...] = jnp.dot(a.astype(BF16), w_ref[...].astype(BF16),
                         preferred_element_type=F32) + b_ref[...]


def _ada(cc, ada_w, ada_b):
    n = ada_w.shape[-1]
    return pl.pallas_call(
        _ada_kernel,
        grid=(DEPTH, n // TN_ADA),
        in_specs=[
            pl.BlockSpec((8, D_MODEL), lambda l, j: (0, 0)),
            pl.BlockSpec((None, D_MODEL, TN_ADA), lambda l, j: (l, 0, j)),
            pl.BlockSpec((None, 1, TN_ADA), lambda l, j: (l, 0, j)),
        ],
        out_specs=pl.BlockSpec((None, 8, TN_ADA), lambda l, j: (l, 0, j)),
        out_shape=jax.ShapeDtypeStruct((DEPTH, 8, n), F32),
        compiler_params=_params(("arbitrary", "arbitrary")),
        name="ada",
    )(cc, ada_w, ada_b.reshape(DEPTH, 1, n))


def _mod_row(i, rows):
    return jnp.where(i * rows < NX, (i * rows) // SEQ, BATCH)


def _rope_blk(i, rows):
    return jnp.where(i * rows < NX, i % (SEQ // rows), SEQ // rows)


def _in_kernel(x_ref, mod_ref, g_ref, w_ref, cos_ref, sin_ref, z_ref, h_ref):
    j = pl.program_id(1)

    @pl.when(j == 0)
    def _():
        y = _rms(x_ref[...], g_ref[...])
        h_ref[...] = (y * (1.0 + mod_ref[1:2, :]) + mod_ref[0:1, :]).astype(BF16)

    z_ref[...] = jnp.dot(h_ref[...], w_ref[...], preferred_element_type=F32)

    @pl.when(j == 0)
    def _():
        cos = cos_ref[...]
        sin = sin_ref[...]
        for c in range(COL_Q, COL_V, LANE):
            z_ref[:, c:c + LANE] = _rope(z_ref[:, c:c + LANE], cos, sin)


def _in_proj(tok, mods, g, w, cos_t, sin_t):
    n = tok.shape[0]
    return pl.pallas_call(
        _in_kernel,
        grid=(n // TM, IN_COLS_PAD // TN_IN),
        in_specs=[
            pl.BlockSpec((TM, D_MODEL), lambda i, j: (i, 0)),
            pl.BlockSpec((None, 6, D_MODEL), lambda i, j: (_mod_row(i, TM), 0, 0)),
            pl.BlockSpec((1, D_MODEL), lambda i, j: (0, 0)),
            pl.BlockSpec((D_MODEL, TN_IN), lambda i, j: (0, j)),
            pl.BlockSpec((TM, LANE), lambda i, j: (_rope_blk(i, TM), 0)),
            pl.BlockSpec((TM, LANE), lambda i, j: (_rope_blk(i, TM), 0)),
        ],
        out_specs=pl.BlockSpec((TM, TN_IN), lambda i, j: (i, j)),
        out_shape=jax.ShapeDtypeStruct((n, IN_COLS_PAD), F32),
        scratch_shapes=[pltpu.VMEM((TM, D_MODEL), BF16)],
        compiler_params=_params(("arbitrary", "arbitrary")),
        name="in_proj",
    )(tok, mods, g, w, cos_t, sin_t)


def _pool_conv_kernel(a_ref, ap_ref, an_ref, db_ref, dc_ref, dcp_ref, dcn_ref,
                      dh_ref, dhp_ref, dhn_ref, pw_ref, ps_ref, cw_ref,
                      ya_ref, yd_ref, e_ref):
    i = pl.program_id(0)
    pos, seqlen = _seq_pos(i, TS)
    start = pos == 0
    end = pos == seqlen - 1
    pf, nf = _tile_edges(i, TS)

    e_ref[0:8, :] = ap_ref[...] * pf
    e_ref[8:8 + TS, :] = a_ref[...]
    e_ref[8 + TS:16 + TS, :] = an_ref[...] * nf
    for g, w in enumerate(POOL_WINDOWS):
        half = w // 2
        cs = slice(g * LANE, (g + 1) * LANE)
        acc = e_ref[8 - half:8 - half + TS, cs]
        for k in range(1, w):
            acc = acc + e_ref[8 - half + k:8 - half + k + TS, cs]
        cnt = (jnp.minimum(pos + half, seqlen) - jnp.maximum(pos - half, 0)).astype(F32)
        pooled = acc / cnt - a_ref[:, cs]
        y = jnp.dot(pooled.astype(BF16), pw_ref[g], preferred_element_type=F32)
        ya_ref[:, cs] = (y * ps_ref[:, cs]).astype(BF16)

    p = dc_ref[...] * dh_ref[...]
    p_prev = dcp_ref[7:8, :] * dhp_ref[7:8, :]
    p_next = dcn_ref[0:1, :] * dhn_ref[0:1, :]
    up, dn = _shift_rows(p, p_prev, p_next, start, end)
    conv = cw_ref[0:1, :] * up + cw_ref[1:2, :] * p + cw_ref[2:3, :] * dn
    yd_ref[...] = (db_ref[...] * conv).astype(BF16)


def _pool_conv(z, n_rows, pool_w, pool_scale, conv_w):
    r8 = TS // 8
    last8 = NT // 8 - 1

    def main(col):
        return pl.BlockSpec((TS, GROUP_W), lambda i: (i, col))

    def prev(col):
        return pl.BlockSpec((8, GROUP_W), lambda i: (jnp.maximum(i * r8 - 1, 0), col))

    def nxt(col):
        return pl.BlockSpec((8, GROUP_W), lambda i: (jnp.minimum((i + 1) * r8, last8), col))

    ca, cb, cc, ch = (COL_A // GROUP_W, COL_DB // GROUP_W, COL_DC // GROUP_W, COL_DH // GROUP_W)
    out = jax.ShapeDtypeStruct((n_rows, GROUP_W), BF16)
    return pl.pallas_call(
        _pool_conv_kernel,
        grid=(n_rows // TS,),
        in_specs=[
            main(ca), prev(ca), nxt(ca),
            main(cb),
            main(cc), prev(cc), nxt(cc),
            main(ch), prev(ch), nxt(ch),
            pl.BlockSpec((len(POOL_WINDOWS), LANE, LANE), lambda i: (0, 0, 0)),
            pl.BlockSpec((1, GROUP_W), lambda i: (0, 0)),
            pl.BlockSpec((3, GROUP_W), lambda i: (0, 0)),
        ],
        out_specs=[pl.BlockSpec((TS, GROUP_W), lambda i: (i, 0))] * 2,
        out_shape=[out, out],
        scratch_shapes=[pltpu.VMEM((TS + 16, GROUP_W), F32)],
        compiler_params=_params(("arbitrary",)),
        name="pool_conv",
    )(z, z, z, z, z, z, z, z, z, z, pool_w, pool_scale, conv_w)


NQX = SEQ // TQ


def _q_tile(b, qt):
    return jnp.where(qt < NQX, b * NQX + qt, NX // TQ + b)


def _softmax_parts(s):
    m = jnp.max(s, axis=-1, keepdims=True)
    e = jnp.exp(s - m)
    return e, 1.0 / jnp.sum(e, axis=-1, keepdims=True)


def _nt_dot(a, b):
    return lax.dot_general(a, b, (((1,), (1,)), ((), ())), preferred_element_type=F32)


def _diff_kernel(lam_init, with_ctx, q_ref, kx_ref, kc_ref, vx_ref, vc_ref, lam_ref, g_ref,
                 o_ref, kb_ref, vb_ref):
    qt = pl.program_id(2)

    @pl.when(qt == 0)
    def _():
        kb_ref[0:SEQ, :] = kx_ref[...].astype(BF16)
        kb_ref[SEQ:, :] = kc_ref[...].astype(BF16)
        vb_ref[0:SEQ, :] = vx_ref[...].astype(BF16)
        vb_ref[SEQ:, :] = vc_ref[...].astype(BF16)

    lp = lam_ref[...]
    lam = (jnp.exp(jnp.sum(lp[0:1] * lp[1:2], keepdims=True))
           - jnp.exp(jnp.sum(lp[2:3] * lp[3:4], keepdims=True)) + lam_init)
    q = q_ref[...] * (64 ** -0.5)
    lane = lax.broadcasted_iota(jnp.int32, q.shape, 1)
    q1 = jnp.where(lane < 64, q, 0.0).astype(BF16)
    q2 = jnp.where(lane >= 64, q, 0.0).astype(BF16)

    def attend(k, v):
        e1, r1 = _softmax_parts(_nt_dot(q1, k))
        e2, r2 = _softmax_parts(_nt_dot(q2, k))
        a = e1 * r1 - e2 * (lam * r2)
        o = jnp.dot(a.astype(BF16), v, preferred_element_type=F32)
        o_ref[...] = (_rms(o, g_ref[...]) * (1.0 - lam_init)).astype(BF16)

    if with_ctx:
        @pl.when(qt < NQX)
        def _():
            attend(kb_ref[...], vb_ref[...])

        @pl.when(qt == NQX)
        def _():
            attend(kb_ref[SEQ:, :], vb_ref[SEQ:, :])
    else:
        attend(kb_ref[...], vb_ref[...])


def _diff_attn(z, lam_p, subln_g, lam_init, with_ctx):
    nq = NQX + 1 if with_ctx else NQX
    n_rows = NT if with_ctx else NX
    cq, ck, cv = COL_Q // LANE, COL_K // LANE, COL_V // LANE
    cblk = NX // CTX_LEN
    return pl.pallas_call(
        functools.partial(_diff_kernel, lam_init, with_ctx),
        grid=(BATCH, HEADS, nq),
        in_specs=[
            pl.BlockSpec((TQ, LANE), lambda b, h, t: (_q_tile(b, t), cq + h)),
            pl.BlockSpec((SEQ, LANE), lambda b, h, t: (b, ck + h)),
            pl.BlockSpec((CTX_LEN, LANE), lambda b, h, t: (cblk + b, ck + h)),
            pl.BlockSpec((SEQ, LANE), lambda b, h, t: (b, cv + h)),
            pl.BlockSpec((CTX_LEN, LANE), lambda b, h, t: (cblk + b, cv + h)),
            pl.BlockSpec((4, 64), lambda b, h, t: (0, 0)),
            pl.BlockSpec((1, LANE), lambda b, h, t: (0, 0)),
        ],
        out_specs=pl.BlockSpec((TQ, LANE), lambda b, h, t: (_q_tile(b, t), h)),
        out_shape=jax.ShapeDtypeStruct((n_rows, GROUP_W), BF16),
        scratch_shapes=[pltpu.VMEM((SEQ + CTX_LEN, LANE), BF16),
                        pltpu.VMEM((SEQ + CTX_LEN, LANE), BF16)],
        compiler_params=_params(("arbitrary", "arbitrary", "arbitrary")),
        name="diff_attn",
    )(z, z, z, z, z, lam_p, subln_g)


def _mla_prep_kernel(ckv_ref, cq_ref, kr_ref, gq_ref, gkv_ref, wq_ref, wkv_ref, cos_ref, sin_ref,
                     qn_ref, qr_ref, kn_ref, v_ref, kr2_ref):
    cos = cos_ref[...]
    sin = sin_ref[...]
    q = jnp.dot(_rms(cq_ref[...], gq_ref[...]).astype(BF16), wq_ref[...],
                preferred_element_type=F32)
    qn_ref[...] = q[:, :GROUP_W].astype(BF16)
    for c in range(0, 2 * LANE, LANE):
        qr_ref[:, c:c + LANE] = _rope(q[:, GROUP_W + c:GROUP_W + c + LANE], cos, sin).astype(BF16)
    kv = jnp.dot(_rms(ckv_ref[...], gkv_ref[...]).astype(BF16), wkv_ref[...],
                 preferred_element_type=F32)
    kn_ref[...] = kv[:, :GROUP_W].astype(BF16)
    v_ref[...] = kv[:, GROUP_W:].astype(BF16)
    kr = _rope(kr_ref[...], cos, sin)
    kr2_ref[...] = (kr + pltpu.roll(kr, 64, 1)).astype(BF16)


def _mla_prep(z, gq, gkv, wq, wkv, cos_t, sin_t):
    def out(w):
        return jax.ShapeDtypeStruct((NT, w), BF16)

    def row(w):
        return pl.BlockSpec((TM, w), lambda i: (i, 0))

    return pl.pallas_call(
        _mla_prep_kernel,
        grid=(NT // TM,),
        in_specs=[
            pl.BlockSpec((TM, 256), lambda i: (i, COL_CKV // 256)),
            pl.BlockSpec((TM, 384), lambda i: (i, COL_CQ // 384)),
            pl.BlockSpec((TM, LANE), lambda i: (i, COL_KR // LANE)),
            pl.BlockSpec((1, 384), lambda i: (0, 0)),
            pl.BlockSpec((1, 256), lambda i: (0, 0)),
            pl.BlockSpec((384, 768), lambda i: (0, 0)),
            pl.BlockSpec((256, 1024), lambda i: (0, 0)),
            pl.BlockSpec((TM, LANE), lambda i: (_rope_blk(i, TM), 0)),
            pl.BlockSpec((TM, LANE), lambda i: (_rope_blk(i, TM), 0)),
        ],
        out_specs=[row(GROUP_W), row(2 * LANE), row(GROUP_W), row(GROUP_W), row(LANE)],
        out_shape=[out(GROUP_W), out(2 * LANE), out(GROUP_W), out(GROUP_W), out(LANE)],
        compiler_params=_params(("arbitrary",)),
        name="mla_prep",
    )(z, z, z, gq, gkv, wq, wkv, cos_t, sin_t)


def _mla_kernel(with_ctx, qn_ref, qr_ref, knx_ref, knc_ref, krx_ref, krc_ref, vx_ref, vc_ref,
                o_ref, kb_ref, vb_ref):
    h = pl.program_id(1)
    qt = pl.program_id(2)

    @pl.when(qt == 0)
    def _():
        kb_ref[0:SEQ, 0:LANE] = knx_ref[...]
        kb_ref[SEQ:, 0:LANE] = knc_ref[...]
        kb_ref[0:SEQ, LANE:] = krx_ref[...]
        kb_ref[SEQ:, LANE:] = krc_ref[...]
        vb_ref[0:SEQ, :] = vx_ref[...]
        vb_ref[SEQ:, :] = vc_ref[...]

    qr = qr_ref[...]
    lane = lax.broadcasted_iota(jnp.int32, qr.shape, 1)
    lo = (h % 2) * 64
    qr = jnp.where((lane >= lo) & (lane < lo + 64), qr, jnp.zeros_like(qr))
    q = jnp.concatenate([qn_ref[...], qr], axis=-1)
    scale = (128 + 64) ** -0.5

    def attend(k, v):
        e, r = _softmax_parts(_nt_dot(q, k) * scale)
        o = jnp.dot((e * r).astype(BF16), v, preferred_element_type=F32)
        o_ref[...] = o.astype(BF16)

    if with_ctx:
        @pl.when(qt < NQX)
        def _():
            attend(kb_ref[...], vb_ref[...])

        @pl.when(qt == NQX)
        def _():
            attend(kb_ref[SEQ:, :], vb_ref[SEQ:, :])
    else:
        attend(kb_ref[...], vb_ref[...])


def _mla_attn(qn, qr, kn, v, kr2, with_ctx):
    nq = NQX + 1 if with_ctx else NQX
    n_rows = NT if with_ctx else NX
    cblk = NX // CTX_LEN
    return pl.pallas_call(
        functools.partial(_mla_kernel, with_ctx),
        grid=(BATCH, HEADS, nq),
        in_specs=[
            pl.BlockSpec((TQ, LANE), lambda b, h, t: (_q_tile(b, t), h)),
            pl.BlockSpec((TQ, LANE), lambda b, h, t: (_q_tile(b, t), h // 2)),
            pl.BlockSpec((SEQ, LANE), lambda b, h, t: (b, h)),
            pl.BlockSpec((CTX_LEN, LANE), lambda b, h, t: (cblk + b, h)),
            pl.BlockSpec((SEQ, LANE), lambda b, h, t: (b, 0)),
            pl.BlockSpec((CTX_LEN, LANE), lambda b, h, t: (cblk + b, 0)),
            pl.BlockSpec((SEQ, LANE), lambda b, h, t: (b, h)),
            pl.BlockSpec((CTX_LEN, LANE), lambda b, h, t: (cblk + b, h)),
        ],
        out_specs=pl.BlockSpec((TQ, LANE), lambda b, h, t: (_q_tile(b, t), h)),
        out_shape=jax.ShapeDtypeStruct((n_rows, GROUP_W), BF16),
        scratch_shapes=[pltpu.VMEM((SEQ + CTX_LEN, 2 * LANE), BF16),
                        pltpu.VMEM((SEQ + CTX_LEN, LANE), BF16)],
        compiler_params=_params(("arbitrary", "arbitrary", "arbitrary")),
        name="mla_attn",
    )(qn, qr, kn, kn, kr2, kr2, v, v)


def _out_kernel(ya_ref, yb_ref, ym_ref, yd_ref, w_ref, x_ref, mod_ref, gpost_ref, gpre_ref,
                xo_ref, h_ref):
    y = jnp.dot(ya_ref[...], w_ref[0:GROUP_W, :], preferred_element_type=F32)
    y += jnp.dot(yb_ref[...], w_ref[GROUP_W:2 * GROUP_W, :], preferred_element_type=F32)
    y += jnp.dot(ym_ref[...], w_ref[2 * GROUP_W:3 * GROUP_W, :], preferred_element_type=F32)
    y += jnp.dot(yd_ref[...], w_ref[3 * GROUP_W:, :], preferred_element_type=F32)
    x = x_ref[...] + mod_ref[2:3, :] * _rms(y, gpost_ref[...])
    xo_ref[...] = x
    h = _rms(x, gpre_ref[...])
    h_ref[...] = (h * (1.0 + mod_ref[4:5, :]) + mod_ref[3:4, :]).astype(BF16)


def _out_proj(ya, yb, ym, yd, w, tok, mods, g_post, g_pre_ffn):
    n = ya.shape[0]
    y_spec = pl.BlockSpec((TM, GROUP_W), lambda i: (i, 0))
    row = pl.BlockSpec((TM, D_MODEL), lambda i: (i, 0))
    vec = pl.BlockSpec((1, D_MODEL), lambda i: (0, 0))
    return pl.pallas_call(
        _out_kernel,
        grid=(n // TM,),
        in_specs=[
            y_spec, y_spec, y_spec, y_spec,
            pl.BlockSpec((D_MODEL, D_MODEL), lambda i: (0, 0)),
            row,
            pl.BlockSpec((None, 6, D_MODEL), lambda i: (_mod_row(i, TM), 0, 0)),
            vec, vec,
        ],
        out_specs=[row, row],
        out_shape=[jax.ShapeDtypeStruct((n, D_MODEL), F32),
                   jax.ShapeDtypeStruct((n, D_MODEL), BF16)],
        compiler_params=_params(("arbitrary",)),
        name="out_proj",
    )(ya, yb, ym, yd, w, tok, mods, g_post, g_pre_ffn)


def _ffn_kernel(h_ref, hp_ref, hn_ref, wg_ref, wv_ref, cg_ref, cv_ref, wd_ref, x_ref, mod_ref,
                gpost_ref, xo_ref, acc_ref):
    i = pl.program_id(0)
    j = pl.program_id(1)
    pos, seqlen = _seq_pos(i, TM)
    start = pos == 0
    end = pos == seqlen - 1

    h = h_ref[...]
    halo = jnp.concatenate([hp_ref[...], hn_ref[...]], axis=0)

    def conv_up(w_ref, c_ref):
        u = jnp.dot(h, w_ref[...], preferred_element_type=F32)
        uh = jnp.dot(halo, w_ref[...], preferred_element_type=F32)
        up, dn = _shift_rows(u, uh[15:16, :], uh[16:17, :], start, end)
        return c_ref[0:1, :] * up + c_ref[1:2, :] * u + c_ref[2:3, :] * dn

    gate = conv_up(wg_ref, cg_ref)
    val = conv_up(wv_ref, cv_ref)
    act = (gate * jax.nn.sigmoid(gate) * val).astype(BF16)
    part = jnp.dot(act, wd_ref[...], preferred_element_type=F32)

    @pl.when(j == 0)
    def _():
        acc_ref[...] = part

    @pl.when(j > 0)
    def _():
        acc_ref[...] += part

    @pl.when(j == pl.num_programs(1) - 1)
    def _():
        xo_ref[...] = x_ref[...] + mod_ref[5:6, :] * _rms(acc_ref[...], gpost_ref[...])


def _ffn(h, tok, mods, w_up, conv_w, w_down, g_post, n_rows):
    nj = D_FF // TF
    r16 = TM // 16
    last16 = h.shape[0] // 16 - 1
    row = pl.BlockSpec((TM, D_MODEL), lambda i, j: (i, 0))
    return pl.pallas_call(
        _ffn_kernel,
        grid=(n_rows // TM, nj),
        in_specs=[
            row,
            pl.BlockSpec((16, D_MODEL), lambda i, j: (jnp.maximum(i * r16 - 1, 0), 0)),
            pl.BlockSpec((16, D_MODEL), lambda i, j: (jnp.minimum((i + 1) * r16, last16), 0)),
            pl.BlockSpec((D_MODEL, TF), lambda i, j: (0, j)),
            pl.BlockSpec((D_MODEL, TF), lambda i, j: (0, j + nj)),
            pl.BlockSpec((3, TF), lambda i, j: (0, j)),
            pl.BlockSpec((3, TF), lambda i, j: (0, j + nj)),
            pl.BlockSpec((TF, D_MODEL), lambda i, j: (j, 0)),
            row,
            pl.BlockSpec((None, 6, D_MODEL), lambda i, j: (_mod_row(i, TM), 0, 0)),
            pl.BlockSpec((1, D_MODEL), lambda i, j: (0, 0)),
        ],
        out_specs=row,
        out_shape=jax.ShapeDtypeStruct((n_rows, D_MODEL), F32),
        scratch_shapes=[pltpu.VMEM((TM, D_MODEL), F32)],
        compiler_params=_params(("arbitrary", "arbitrary")),
        name="ffn",
    )(h, h, h, w_up, w_up, conv_w, conv_w, w_down, tok, mods, g_post)


def _rope_tables():
    rows = SEQ // GRID_W
    row = jnp.repeat(jnp.arange(rows), GRID_W).astype(F32)
    col = jnp.tile(jnp.arange(GRID_W), rows).astype(F32)
    n_freq = 64 // 4
    inv = ROPE_THETA ** (-jnp.arange(n_freq, dtype=F32) / n_freq)
    ang = jnp.concatenate([row[:, None] * inv, col[:, None] * inv], axis=-1)
    cos, sin = jnp.cos(ang), jnp.sin(ang)
    cos_t = jnp.concatenate([cos, cos, cos, cos], axis=-1)
    sin_t = jnp.concatenate([-sin, sin, -sin, sin], axis=-1)
    cos_t = jnp.concatenate([cos_t, jnp.ones((TM, LANE), F32)], axis=0)
    sin_t = jnp.concatenate([sin_t, jnp.zeros((TM, LANE), F32)], axis=0)
    return cos_t, sin_t


def _layout_w_in(w):
    a, qkv, cq, ckv, kr, d = jnp.split(w, [512, 2048, 2432, 2688, 2752], axis=-1)
    pad = jnp.zeros((w.shape[0], IN_COLS_PAD - COL_KR - 64), w.dtype)
    return jnp.concatenate([a, qkv, d, ckv, cq, kr, pad], axis=-1).astype(BF16)


def _layout_w_uq(w):
    w = w.reshape(384, HEADS, 192)
    return jnp.concatenate([w[:, :, :128].reshape(384, 512), w[:, :, 128:].reshape(384, 256)],
                           axis=-1).astype(BF16)


def _layout_w_ukv(w):
    w = w.reshape(256, HEADS, 256)
    return jnp.concatenate([w[:, :, :128].reshape(256, 512), w[:, :, 128:].reshape(256, 512)],
                           axis=-1).astype(BF16)


def kernel(x, c, ctx, c_ctx, ada_w, ada_b, g_pre_mix, g_post_mix, g_pre_ffn, g_post_ffn, w_in,
           pool_w, pool_scale, diff_lambda, diff_subln_g, mla_gq, mla_w_uq, mla_gkv, mla_w_ukv,
           conv_w, w_out, ffn_w_up, ffn_conv_w, ffn_w_down):
    tok = jnp.concatenate([x.reshape(NX, D_MODEL), ctx.reshape(NC, D_MODEL)], axis=0)
    cc = jnp.concatenate([c, c_ctx[None, :], jnp.zeros((3, D_MODEL), F32)], axis=0)
    mods_all = _ada(cc, ada_w, ada_b).reshape(DEPTH, 8, 6, D_MODEL)
    cos_t, sin_t = _rope_tables()

    for l in range(DEPTH):
        last = l == DEPTH - 1
        mods = mods_all[l]
        n_rows = NX if last else NT
        lam_init = 0.8 - 0.6 * math.exp(-0.3 * l)

        z = _in_proj(tok, mods, g_pre_mix[l][None, :], _layout_w_in(w_in[l]), cos_t, sin_t)
        ya, yd = _pool_conv(z, n_rows, pool_w[l].astype(BF16), pool_scale[l][None, :], conv_w[l])
        yb = _diff_attn(z, diff_lambda[l], diff_subln_g[l][None, :], lam_init, not last)
        qn, qr, kn, v, kr2 = _mla_prep(z, mla_gq[l][None, :], mla_gkv[l][None, :],
                                       _layout_w_uq(mla_w_uq[l]), _layout_w_ukv(mla_w_ukv[l]),
                                       cos_t, sin_t)
        ym = _mla_attn(qn, qr, kn, v, kr2, not last)
        tok, h = _out_proj(ya, yb, ym, yd, w_out[l].astype(BF16), tok, mods,
                           g_post_mix[l][None, :], g_pre_ffn[l][None, :])
        tok = _ffn(h, tok, mods, ffn_w_up[l].astype(BF16), ffn_conv_w[l],
                   ffn_w_down[l].astype(BF16), g_post_ffn[l][None, :], n_rows)
    return tok.reshape(BATCH, SEQ, D_MODEL)
```

```python
import functools
import math

import jax
import jax.numpy as jnp
from jax import lax
from jax.experimental import pallas as pl
from jax.experimental.pallas import tpu as pltpu

F32 = jnp.float32
BF16 = jnp.bfloat16

D_MODEL = 2048
BATCH = 4
SEQ = 2048
CTX_LEN = 256
GRID_W = 64
DEPTH = 2
NORM_EPS = 1e-6
ROPE_THETA = 10000.0
GROUP_W = 512
POOL_WINDOWS = (2, 4, 8, 16)
HEADS = 4
D_FF = 5632

NX = BATCH * SEQ
NC = BATCH * CTX_LEN
NT = NX + NC

COL_A, COL_Q, COL_K, COL_V = 0, 512, 1024, 1536
COL_DB, COL_DC, COL_DH = 2048, 2560, 3072
COL_CKV, COL_CQ, COL_KR = 3584, 3840, 4224
IN_COLS_PAD = 4352

LANE = 128
SUBLANE = 8
TM = 512
TS = 256
TQ = 256
QSUB = 128
TN_IN = IN_COLS_PAD // 2
TF = 512
TN_ADA = 1024
VMEM_LIMIT = 56 * 1024 * 1024


def _params(sem):
    return pltpu.CompilerParams(dimension_semantics=sem, vmem_limit_bytes=VMEM_LIMIT)


def _rms(x, g):
    ms = jnp.mean(x * x, axis=-1, keepdims=True)
    return x * lax.rsqrt(ms + NORM_EPS) * g


def _rope(v, cos, sin):
    lane = lax.broadcasted_iota(jnp.int32, v.shape, 1)
    sw = jnp.where((lane & 32) == 0, pltpu.roll(v, 96, 1), pltpu.roll(v, 32, 1))
    return v * cos + sw * sin


def _seq_pos(tile, rows):
    r = tile * rows + lax.broadcasted_iota(jnp.int32, (rows, 1), 0)
    seqlen = jnp.where(r < NX, SEQ, CTX_LEN)
    return r & (seqlen - 1), seqlen


def _tile_edges(tile, rows):
    r0 = tile * rows
    seqlen = jnp.where(r0 < NX, SEQ, CTX_LEN)
    has_prev = jnp.where((r0 & (seqlen - 1)) == 0, 0.0, 1.0)
    has_next = jnp.where(((r0 + rows) & (seqlen - 1)) == 0, 0.0, 1.0)
    return has_prev, has_next


def _conv3_rows(u, c_ref, is_ctx):
    n = u.shape[0] - 2 * SUBLANE
    up = u[SUBLANE - 1:SUBLANE - 1 + n]
    mid = u[SUBLANE:SUBLANE + n]
    dn = u[SUBLANE + 1:SUBLANE + 1 + n]
    row8 = lax.broadcasted_iota(jnp.int32, (SUBLANE, 1), 0) + jnp.where(is_ctx, 0, 2 * SUBLANE)
    for p in range(CTX_LEN, n, CTX_LEN):
        up = jnp.concatenate([up[:p], jnp.where(row8 == 0, 0.0, up[p:p + SUBLANE]),
                              up[p + SUBLANE:]], axis=0)
        dn = jnp.concatenate([dn[:p - SUBLANE],
                              jnp.where(row8 == SUBLANE - 1, 0.0, dn[p - SUBLANE:p]), dn[p:]], axis=0)
    return c_ref[0:1, :] * up + c_ref[1:2, :] * mid + c_ref[2:3, :] * dn


def _halo_specs(rows, width, n_rows, col):
    r8 = rows // SUBLANE
    last8 = n_rows // SUBLANE - 1
    return [
        pl.BlockSpec((rows, width), lambda i, *_: (i, col)),
        pl.BlockSpec((SUBLANE, width), lambda i, *_: (jnp.maximum(i * r8 - 1, 0), col)),
        pl.BlockSpec((SUBLANE, width), lambda i, *_: (jnp.minimum((i + 1) * r8, last8), col)),
    ]


def _layer_spec(l, shape):
    return pl.BlockSpec((None,) + shape, lambda *_: (l,) + (0,) * len(shape))


def _mod_spec(l, rows):
    def index(i, *_):
        return (l, jnp.where(i * rows < NX, (i * rows) // SEQ, BATCH), 0, 0)
    return pl.BlockSpec((None, None, 6, D_MODEL), index)


def _ada_kernel(c_ref, w_ref, b_ref, o_ref):
    c = c_ref[...]
    a = c * jax.nn.sigmoid(c)
    o_ref[...] = jnp.dot(a.astype(BF16), w_ref[...].astype(BF16),
                         preferred_element_type=F32) + b_ref[...]


def _ada(cc, ada_w, ada_b):
    n = ada_w.shape[-1]
    return pl.pallas_call(
        _ada_kernel,
        grid=(DEPTH, n // TN_ADA),
        in_specs=[
            pl.BlockSpec((8, D_MODEL), lambda l, j: (0, 0)),
            pl.BlockSpec((None, D_MODEL, TN_ADA), lambda l, j: (l, 0, j)),
            pl.BlockSpec((None, 1, TN_ADA), lambda l, j: (l, 0, j)),
        ],
        out_specs=pl.BlockSpec((None, 8, TN_ADA), lambda l, j: (l, 0, j)),
        out_shape=jax.ShapeDtypeStruct((DEPTH, 8, n), F32),
        compiler_params=_params(("arbitrary", "arbitrary")),
        name="ada",
    )(cc, ada_w, ada_b.reshape(DEPTH, 1, n))


def _in_kernel(x_ref, mod_ref, g_ref, w_ref, z_ref, h_ref):
    @pl.when(pl.program_id(1) == 0)
    def _():
        y = _rms(x_ref[...], g_ref[...])
        h_ref[...] = (y * (1.0 + mod_ref[1:2, :]) + mod_ref[0:1, :]).astype(BF16)

    z_ref[...] = jnp.dot(h_ref[...], w_ref[...], preferred_element_type=F32)


def _in_proj(l, tok, mods, g, w):
    n = tok.shape[0]
    return pl.pallas_call(
        _in_kernel,
        grid=(n // TM, IN_COLS_PAD // TN_IN),
        in_specs=[
            pl.BlockSpec((TM, D_MODEL), lambda i, j: (i, 0)),
            _mod_spec(l, TM),
            _layer_spec(l, (1, D_MODEL)),
            pl.BlockSpec((None, D_MODEL, TN_IN), lambda i, j: (l, 0, j)),
        ],
        out_specs=pl.BlockSpec((TM, TN_IN), lambda i, j: (i, j)),
        out_shape=jax.ShapeDtypeStruct((n, IN_COLS_PAD), F32),
        scratch_shapes=[pltpu.VMEM((TM, D_MODEL), BF16)],
        compiler_params=_params(("arbitrary", "arbitrary")),
        name="in_proj",
    )(tok, mods, g, w)


def _pool_conv_kernel(a_ref, ap_ref, an_ref, db_ref, dc_ref, dcp_ref, dcn_ref,
                      dh_ref, dhp_ref, dhn_ref, pw_ref, ps_ref, cw_ref,
                      ya_ref, yd_ref, e_ref):
    i = pl.program_id(0)
    pos, seqlen = _seq_pos(i, TS)
    pf, nf = _tile_edges(i, TS)

    e_ref[0:8, :] = ap_ref[...] * pf
    e_ref[8:8 + TS, :] = a_ref[...]
    e_ref[8 + TS:16 + TS, :] = an_ref[...] * nf
    for g, w in enumerate(POOL_WINDOWS):
        half = w // 2
        cs = slice(g * LANE, (g + 1) * LANE)
        acc = e_ref[8 - half:8 - half + TS, cs]
        for k in range(1, w):
            acc = acc + e_ref[8 - half + k:8 - half + k + TS, cs]
        cnt = (jnp.minimum(pos + half, seqlen) - jnp.maximum(pos - half, 0)).astype(F32)
        pooled = acc / cnt - a_ref[:, cs]
        y = jnp.dot(pooled.astype(BF16), pw_ref[g], preferred_element_type=F32)
        ya_ref[:, cs] = (y * ps_ref[:, cs]).astype(BF16)

    p = jnp.concatenate([dcp_ref[...] * dhp_ref[...] * pf, dc_ref[...] * dh_ref[...],
                         dcn_ref[...] * dhn_ref[...] * nf], axis=0)
    yd_ref[...] = (db_ref[...] * _conv3_rows(p, cw_ref, False)).astype(BF16)


def _pool_conv(l, z, n_rows, pool_w, pool_scale, conv_w):
    ca, cb, cc, ch = (COL_A // GROUP_W, COL_DB // GROUP_W, COL_DC // GROUP_W, COL_DH // GROUP_W)
    out = jax.ShapeDtypeStruct((n_rows, GROUP_W), BF16)
    return pl.pallas_call(
        _pool_conv_kernel,
        grid=(n_rows // TS,),
        in_specs=(_halo_specs(TS, GROUP_W, NT, ca)
                  + [pl.BlockSpec((TS, GROUP_W), lambda i: (i, cb))]
                  + _halo_specs(TS, GROUP_W, NT, cc) + _halo_specs(TS, GROUP_W, NT, ch)
                  + [_layer_spec(l, (len(POOL_WINDOWS), LANE, LANE)),
                     _layer_spec(l, (1, GROUP_W)), _layer_spec(l, (3, GROUP_W))]),
        out_specs=[pl.BlockSpec((TS, GROUP_W), lambda i: (i, 0))] * 2,
        out_shape=[out, out],
        scratch_shapes=[pltpu.VMEM((TS + 16, GROUP_W), F32)],
        compiler_params=_params(("arbitrary",)),
        name="pool_conv",
    )(z, z, z, z, z, z, z, z, z, z, pool_w, pool_scale, conv_w)


NQX = SEQ // TQ


def _q_tile(b, qt):
    return jnp.where(qt < NQX, b * NQX + qt, NX // TQ + b)


def _q_rope_spec():
    return pl.BlockSpec((TQ, LANE), lambda b, h, t: (jnp.minimum(t, NQX), 0))


def _nt_dot(a, b):
    return lax.dot_general(a, b, (((1,), (1,)), ((), ())), preferred_element_type=F32)


def _exp_rows(s):
    return jnp.exp(s - jnp.max(s, axis=-1, keepdims=True)).astype(BF16)


def _on_keys(with_ctx, qt, attend):
    if with_ctx:
        @pl.when(qt < NQX)
        def _():
            attend(0)

        @pl.when(qt == NQX)
        def _():
            attend(SEQ)
    else:
        attend(0)


def _diff_kernel(lam_init, with_ctx, q_ref, kx_ref, kc_ref, vx_ref, vc_ref, cq_ref, sq_ref,
                 ck_ref, sk_ref, lam_ref, g_ref, o_ref, kb_ref, vb_ref):
    qt = pl.program_id(2)

    @pl.when(qt == 0)
    def _():
        kb_ref[0:SEQ, :] = _rope(kx_ref[...], ck_ref[...], sk_ref[...]).astype(BF16)
        kb_ref[SEQ:, :] = kc_ref[...].astype(BF16)
        vb_ref[0:SEQ, 0:LANE] = vx_ref[...].astype(BF16)
        vb_ref[SEQ:, 0:LANE] = vc_ref[...].astype(BF16)
        vb_ref[:, LANE:] = jnp.ones((SEQ + CTX_LEN, LANE), BF16)

    lp = lam_ref[...]
    lam = (jnp.exp(jnp.sum(lp[0:1] * lp[1:2], keepdims=True))
           - jnp.exp(jnp.sum(lp[2:3] * lp[3:4], keepdims=True)) + lam_init)
    q = _rope(q_ref[...], cq_ref[...], sq_ref[...]) * (64 ** -0.5)
    lane = lax.broadcasted_iota(jnp.int32, q.shape, 1)
    q1 = jnp.where(lane < 64, q, 0.0).astype(BF16)
    q2 = jnp.where(lane >= 64, q, 0.0).astype(BF16)

    def attend(k0):
        def pv(qh):
            return jnp.dot(_exp_rows(_nt_dot(qh, kb_ref[k0:, :])), vb_ref[k0:, :],
                           preferred_element_type=F32)

        for r in range(0, TQ, QSUB):
            o1 = pv(q1[r:r + QSUB])
            o2 = pv(q2[r:r + QSUB])
            o = o1[:, :LANE] / o1[:, LANE:] - o2[:, :LANE] * (lam / o2[:, LANE:])
            o_ref[r:r + QSUB, :] = (_rms(o, g_ref[...]) * (1.0 - lam_init)).astype(BF16)

    _on_keys(with_ctx, qt, attend)


def _diff_attn(l, z, cos_t, sin_t, lam_p, subln_g, lam_init, with_ctx):
    nq = NQX + 1 if with_ctx else NQX
    n_rows = NT if with_ctx else NX
    cq, ck, cv = COL_Q // LANE, COL_K // LANE, COL_V // LANE
    cblk = NX // CTX_LEN
    k_rope = pl.BlockSpec((SEQ, LANE), lambda b, h, t: (0, 0))
    return pl.pallas_call(
        functools.partial(_diff_kernel, lam_init, with_ctx),
        grid=(BATCH, HEADS, nq),
        in_specs=[
            pl.BlockSpec((TQ, LANE), lambda b, h, t: (_q_tile(b, t), cq + h)),
            pl.BlockSpec((SEQ, LANE), lambda b, h, t: (b, ck + h)),
            pl.BlockSpec((CTX_LEN, LANE), lambda b, h, t: (cblk + b, ck + h)),
            pl.BlockSpec((SEQ, LANE), lambda b, h, t: (b, cv + h)),
            pl.BlockSpec((CTX_LEN, LANE), lambda b, h, t: (cblk + b, cv + h)),
            _q_rope_spec(), _q_rope_spec(), k_rope, k_rope,
            _layer_spec(l, (4, 64)),
            _layer_spec(l, (1, LANE)),
        ],
        out_specs=pl.BlockSpec((TQ, LANE), lambda b, h, t: (_q_tile(b, t), h)),
        out_shape=jax.ShapeDtypeStruct((n_rows, GROUP_W), BF16),
        scratch_shapes=[pltpu.VMEM((SEQ + CTX_LEN, LANE), BF16),
                        pltpu.VMEM((SEQ + CTX_LEN, 2 * LANE), BF16)],
        compiler_params=_params(("arbitrary", "arbitrary", "arbitrary")),
        name="diff_attn",
    )(z, z, z, z, z, cos_t, sin_t, cos_t, sin_t, lam_p, subln_g)


def _mla_prep_kernel(ckv_ref, cq_ref, kr_ref, gq_ref, gkv_ref, wq_ref, wkv_ref, cos_ref, sin_ref,
                     qn_ref, qr_ref, kn_ref, v_ref, kr2_ref):
    cos = cos_ref[...]
    sin = sin_ref[...]
    q = jnp.dot(_rms(cq_ref[...], gq_ref[...]).astype(BF16), wq_ref[...],
                preferred_element_type=F32)
    qn_ref[...] = q[:, :GROUP_W].astype(BF16)
    for c in range(0, 2 * LANE, LANE):
        qr_ref[:, c:c + LANE] = _rope(q[:, GROUP_W + c:GROUP_W + c + LANE], cos, sin).astype(BF16)
    kv = jnp.dot(_rms(ckv_ref[...], gkv_ref[...]).astype(BF16), wkv_ref[...],
                 preferred_element_type=F32)
    kn_ref[...] = kv[:, :GROUP_W].astype(BF16)
    v_ref[...] = kv[:, GROUP_W:].astype(BF16)
    kr = _rope(kr_ref[...], cos, sin)
    kr2_ref[...] = (kr + pltpu.roll(kr, 64, 1)).astype(BF16)


def _mla_prep(l, z, gq, gkv, wq, wkv, cos_t, sin_t):
    def out(w):
        return jax.ShapeDtypeStruct((NT, w), BF16)

    def row(w):
        return pl.BlockSpec((TM, w), lambda i: (i, 0))

    rope = pl.BlockSpec((TM, LANE), lambda i: (jnp.where(i * TM < NX, i % (SEQ // TM), SEQ // TM), 0))
    return pl.pallas_call(
        _mla_prep_kernel,
        grid=(NT // TM,),
        in_specs=[
            pl.BlockSpec((TM, 256), lambda i: (i, COL_CKV // 256)),
            pl.BlockSpec((TM, 384), lambda i: (i, COL_CQ // 384)),
            pl.BlockSpec((TM, LANE), lambda i: (i, COL_KR // LANE)),
            _layer_spec(l, (1, 384)),
            _layer_spec(l, (1, 256)),
            _layer_spec(l, (384, 768)),
            _layer_spec(l, (256, 1024)),
            rope, rope,
        ],
        out_specs=[row(GROUP_W), row(2 * LANE), row(GROUP_W), row(GROUP_W), row(LANE)],
        out_shape=[out(GROUP_W), out(2 * LANE), out(GROUP_W), out(GROUP_W), out(LANE)],
        compiler_params=_params(("arbitrary",)),
        name="mla_prep",
    )(z, z, z, gq, gkv, wq, wkv, cos_t, sin_t)


def _mla_kernel(with_ctx, qn_ref, qr_ref, knx_ref, knc_ref, krx_ref, krc_ref, vx_ref, vc_ref,
                o_ref, kb_ref, vb_ref):
    h = pl.program_id(1)
    qt = pl.program_id(2)

    @pl.when(qt == 0)
    def _():
        kb_ref[0:SEQ, 0:LANE] = knx_ref[...]
        kb_ref[SEQ:, 0:LANE] = knc_ref[...]
        kb_ref[0:SEQ, LANE:] = krx_ref[...]
        kb_ref[SEQ:, LANE:] = krc_ref[...]
        vb_ref[0:SEQ, 0:LANE] = vx_ref[...]
        vb_ref[SEQ:, 0:LANE] = vc_ref[...]
        vb_ref[:, LANE:] = jnp.ones((SEQ + CTX_LEN, LANE), BF16)

    qr = qr_ref[...]
    lane = lax.broadcasted_iota(jnp.int32, qr.shape, 1)
    lo = (h % 2) * 64
    qr = jnp.where((lane >= lo) & (lane < lo + 64), qr, jnp.zeros_like(qr))
    q = jnp.concatenate([qn_ref[...], qr], axis=-1)
    scale = (128 + 64) ** -0.5

    def attend(k0):
        for r in range(0, TQ, QSUB):
            o = jnp.dot(_exp_rows(_nt_dot(q[r:r + QSUB], kb_ref[k0:, :]) * scale), vb_ref[k0:, :],
                        preferred_element_type=F32)
            o_ref[r:r + QSUB, :] = (o[:, :LANE] / o[:, LANE:]).astype(BF16)

    _on_keys(with_ctx, qt, attend)


def _mla_attn(qn, qr, kn, v, kr2, with_ctx):
    nq = NQX + 1 if with_ctx else NQX
    n_rows = NT if with_ctx else NX
    cblk = NX // CTX_LEN
    return pl.pallas_call(
        functools.partial(_mla_kernel, with_ctx),
        grid=(BATCH, HEADS, nq),
        in_specs=[
            pl.BlockSpec((TQ, LANE), lambda b, h, t: (_q_tile(b, t), h)),
            pl.BlockSpec((TQ, LANE), lambda b, h, t: (_q_tile(b, t), h // 2)),
            pl.BlockSpec((SEQ, LANE), lambda b, h, t: (b, h)),
            pl.BlockSpec((CTX_LEN, LANE), lambda b, h, t: (cblk + b, h)),
            pl.BlockSpec((SEQ, LANE), lambda b, h, t: (b, 0)),
            pl.BlockSpec((CTX_LEN, LANE), lambda b, h, t: (cblk + b, 0)),
            pl.BlockSpec((SEQ, LANE), lambda b, h, t: (b, h)),
            pl.BlockSpec((CTX_LEN, LANE), lambda b, h, t: (cblk + b, h)),
        ],
        out_specs=pl.BlockSpec((TQ, LANE), lambda b, h, t: (_q_tile(b, t), h)),
        out_shape=jax.ShapeDtypeStruct((n_rows, GROUP_W), BF16),
        scratch_shapes=[pltpu.VMEM((SEQ + CTX_LEN, 2 * LANE), BF16),
                        pltpu.VMEM((SEQ + CTX_LEN, 2 * LANE), BF16)],
        compiler_params=_params(("arbitrary", "arbitrary", "arbitrary")),
        name="mla_attn",
    )(qn, qr, kn, kn, kr2, kr2, v, v)


def _out_kernel(ya_ref, yb_ref, ym_ref, yd_ref, w_ref, x_ref, mod_ref, gpost_ref, xo_ref, wb_ref):
    @pl.when(pl.program_id(0) == 0)
    def _():
        wb_ref[...] = w_ref[...].astype(BF16)

    y = jnp.dot(ya_ref[...], wb_ref[0:GROUP_W, :], preferred_element_type=F32)
    y += jnp.dot(yb_ref[...], wb_ref[GROUP_W:2 * GROUP_W, :], preferred_element_type=F32)
    y += jnp.dot(ym_ref[...], wb_ref[2 * GROUP_W:3 * GROUP_W, :], preferred_element_type=F32)
    y += jnp.dot(yd_ref[...], wb_ref[3 * GROUP_W:, :], preferred_element_type=F32)
    xo_ref[...] = x_ref[...] + mod_ref[2:3, :] * _rms(y, gpost_ref[...])


def _out_proj(l, ya, yb, ym, yd, w, tok, mods, g_post):
    n = ya.shape[0]
    y_spec = pl.BlockSpec((TM, GROUP_W), lambda i: (i, 0))
    row = pl.BlockSpec((TM, D_MODEL), lambda i: (i, 0))
    return pl.pallas_call(
        _out_kernel,
        grid=(n // TM,),
        in_specs=[
            y_spec, y_spec, y_spec, y_spec,
            pl.BlockSpec((None, D_MODEL, D_MODEL), lambda i: (l, 0, 0),
                         pipeline_mode=pl.Buffered(1)),
            row,
            _mod_spec(l, TM),
            _layer_spec(l, (1, D_MODEL)),
        ],
        out_specs=row,
        out_shape=jax.ShapeDtypeStruct((n, D_MODEL), F32),
        scratch_shapes=[pltpu.VMEM((D_MODEL, D_MODEL), BF16)],
        compiler_params=_params(("arbitrary",)),
        name="out_proj",
    )(ya, yb, ym, yd, w, tok, mods, g_post)


def _ffn_kernel(x_ref, xp_ref, xn_ref, mod_ref, gpre_ref, gpost_ref, wg_ref, wv_ref, cg_ref, cv_ref,
                wd_ref, xo_ref, hx_ref, acc_ref):
    i = pl.program_id(0)
    j = pl.program_id(1)

    @pl.when(j == 0)
    def _():
        has_prev, has_next = _tile_edges(i, TM)

        def hmod(x):
            return _rms(x, gpre_ref[...]) * (1.0 + mod_ref[4:5, :]) + mod_ref[3:4, :]

        hx_ref[...] = jnp.concatenate(
            [hmod(xp_ref[...]) * has_prev, hmod(x_ref[...]), hmod(xn_ref[...]) * has_next],
            axis=0).astype(BF16)
        acc_ref[...] = jnp.zeros_like(acc_ref)

    is_ctx = i * TM >= NX
    hx = hx_ref[...]
    gate = _conv3_rows(jnp.dot(hx, wg_ref[...], preferred_element_type=F32), cg_ref, is_ctx)
    val = _conv3_rows(jnp.dot(hx, wv_ref[...], preferred_element_type=F32), cv_ref, is_ctx)
    act = (gate * jax.nn.sigmoid(gate) * val).astype(BF16)
    acc_ref[...] += jnp.dot(act, wd_ref[...], preferred_element_type=F32)

    @pl.when(j == pl.num_programs(1) - 1)
    def _():
        xo_ref[...] = x_ref[...] + mod_ref[5:6, :] * _rms(acc_ref[...], gpost_ref[...])


def _ffn(l, tok, mods, g_pre, g_post, w_up, conv_w, w_down, n_rows):
    nj = D_FF // TF
    return pl.pallas_call(
        _ffn_kernel,
        grid=(n_rows // TM, nj),
        in_specs=(_halo_specs(TM, D_MODEL, n_rows, 0) + [
            _mod_spec(l, TM),
            _layer_spec(l, (1, D_MODEL)),
            _layer_spec(l, (1, D_MODEL)),
            pl.BlockSpec((None, D_MODEL, TF), lambda i, j: (l, 0, j)),
            pl.BlockSpec((None, D_MODEL, TF), lambda i, j: (l, 0, j + nj)),
            pl.BlockSpec((None, 3, TF), lambda i, j: (l, 0, j)),
            pl.BlockSpec((None, 3, TF), lambda i, j: (l, 0, j + nj)),
            pl.BlockSpec((None, TF, D_MODEL), lambda i, j: (l, j, 0)),
        ]),
        out_specs=pl.BlockSpec((TM, D_MODEL), lambda i, j: (i, 0)),
        out_shape=jax.ShapeDtypeStruct((n_rows, D_MODEL), F32),
        scratch_shapes=[pltpu.VMEM((TM + 2 * SUBLANE, D_MODEL), BF16),
                        pltpu.VMEM((TM, D_MODEL), F32)],
        compiler_params=_params(("arbitrary", "arbitrary")),
        name="ffn",
    )(tok, tok, tok, mods, g_pre, g_post, w_up, w_up, conv_w, conv_w, w_down)


def _rope_tables():
    rows = SEQ // GRID_W
    row = jnp.repeat(jnp.arange(rows), GRID_W).astype(F32)
    col = jnp.tile(jnp.arange(GRID_W), rows).astype(F32)
    n_freq = 64 // 4
    inv = ROPE_THETA ** (-jnp.arange(n_freq, dtype=F32) / n_freq)
    ang = jnp.concatenate([row[:, None] * inv, col[:, None] * inv], axis=-1)
    cos, sin = jnp.cos(ang), jnp.sin(ang)
    cos_t = jnp.concatenate([cos, cos, cos, cos], axis=-1)
    sin_t = jnp.concatenate([-sin, sin, -sin, sin], axis=-1)
    cos_t = jnp.concatenate([cos_t, jnp.ones((TM, LANE), F32)], axis=0)
    sin_t = jnp.concatenate([sin_t, jnp.zeros((TM, LANE), F32)], axis=0)
    return cos_t, sin_t


def _layout_w_in(w):
    a, qkv, cq, ckv, kr, d = jnp.split(w, [512, 2048, 2432, 2688, 2752], axis=-1)
    pad = jnp.zeros(w.shape[:-1] + (IN_COLS_PAD - COL_KR - 64,), w.dtype)
    return jnp.concatenate([a, qkv, d, ckv, cq, kr, pad], axis=-1).astype(BF16)


def _layout_w_uq(w):
    w = w.reshape(DEPTH, 384, HEADS, 192)
    return jnp.concatenate([w[..., :128].reshape(DEPTH, 384, 512),
                            w[..., 128:].reshape(DEPTH, 384, 256)], axis=-1).astype(BF16)


def _layout_w_ukv(w):
    w = w.reshape(DEPTH, 256, HEADS, 256)
    return jnp.concatenate([w[..., :128].reshape(DEPTH, 256, 512),
                            w[..., 128:].reshape(DEPTH, 256, 512)], axis=-1).astype(BF16)


def kernel(x, c, ctx, c_ctx, ada_w, ada_b, g_pre_mix, g_post_mix, g_pre_ffn, g_post_ffn, w_in,
           pool_w, pool_scale, diff_lambda, diff_subln_g, mla_gq, mla_w_uq, mla_gkv, mla_w_ukv,
           conv_w, w_out, ffn_w_up, ffn_conv_w, ffn_w_down):
    tok = jnp.concatenate([x.reshape(NX, D_MODEL), ctx.reshape(NC, D_MODEL)], axis=0)
    cc = jnp.concatenate([c, c_ctx[None, :], jnp.zeros((3, D_MODEL), F32)], axis=0)
    mods = _ada(cc, ada_w, ada_b).reshape(DEPTH, 8, 6, D_MODEL)
    cos_t, sin_t = _rope_tables()

    def vec(p):
        return p.reshape(DEPTH, 1, p.shape[-1])

    w_in_b = _layout_w_in(w_in)
    w_uq_b, w_ukv_b = _layout_w_uq(mla_w_uq), _layout_w_ukv(mla_w_ukv)
    pool_w_b = pool_w.astype(BF16)
    w_up_b, w_down_b = ffn_w_up.astype(BF16), ffn_w_down.astype(BF16)

    for l in range(DEPTH):
        last = l == DEPTH - 1
        n_rows = NX if last else NT
        lam_init = 0.8 - 0.6 * math.exp(-0.3 * l)

        z = _in_proj(l, tok, mods, vec(g_pre_mix), w_in_b)
        ya, yd = _pool_conv(l, z, n_rows, pool_w_b, vec(pool_scale), conv_w)
        yb = _diff_attn(l, z, cos_t, sin_t, diff_lambda, vec(diff_subln_g), lam_init, not last)
        qn, qr, kn, v, kr2 = _mla_prep(l, z, vec(mla_gq), vec(mla_gkv), w_uq_b, w_ukv_b, cos_t, sin_t)
        ym = _mla_attn(qn, qr, kn, v, kr2, not last)
        tok = _out_proj(l, ya, yb, ym, yd, w_out, tok, mods, vec(g_post_mix))
        tok = _ffn(l, tok, mods, vec(g_pre_ffn), vec(g_post_ffn), w_up_b, ffn_conv_w, w_down_b, n_rows)
    return tok.reshape(BATCH, SEQ, D_MODEL)
```

```python
import functools
import math

import jax
import jax.numpy as jnp
from jax import lax
from jax.experimental import pallas as pl
from jax.experimental.pallas import tpu as pltpu

F32 = jnp.float32
BF16 = jnp.bfloat16

D_MODEL = 2048
BATCH = 4
SEQ = 2048
CTX_LEN = 256
GRID_W = 64
DEPTH = 2
NORM_EPS = 1e-6
ROPE_THETA = 10000.0
GROUP_W = 512
POOL_WINDOWS = (2, 4, 8, 16)
HEADS = 4
D_FF = 5632

NX = BATCH * SEQ
NC = BATCH * CTX_LEN
NT = NX + NC

COL_A, COL_Q, COL_K, COL_V = 0, 512, 1024, 1536
COL_DB, COL_DC, COL_DH = 2048, 2560, 3072
COL_CKV, COL_CQ, COL_KR = 3584, 3840, 4224
IN_COLS_PAD = 4352

LANE = 128
SUBLANE = 8
TM = 512
TS = 256
QSUB = 256
TN_IN = IN_COLS_PAD // 2
TF = 512
TN_ADA = 1024
VMEM_LIMIT = 56 * 1024 * 1024


def _params(sem):
    return pltpu.CompilerParams(dimension_semantics=sem, vmem_limit_bytes=VMEM_LIMIT)


def _rms(x, g):
    ms = jnp.mean(x * x, axis=-1, keepdims=True)
    return x * lax.rsqrt(ms + NORM_EPS) * g


def _rope(v, cos, sin):
    lane = lax.broadcasted_iota(jnp.int32, v.shape, 1)
    sw = jnp.where((lane & 32) == 0, pltpu.roll(v, 96, 1), pltpu.roll(v, 32, 1))
    return v * cos + sw * sin


def _seq_pos(tile, rows):
    r = tile * rows + lax.broadcasted_iota(jnp.int32, (rows, 1), 0)
    seqlen = jnp.where(r < NX, SEQ, CTX_LEN)
    return r & (seqlen - 1), seqlen


def _tile_edges(tile, rows):
    r0 = tile * rows
    seqlen = jnp.where(r0 < NX, SEQ, CTX_LEN)
    has_prev = jnp.where((r0 & (seqlen - 1)) == 0, 0.0, 1.0)
    has_next = jnp.where(((r0 + rows) & (seqlen - 1)) == 0, 0.0, 1.0)
    return has_prev, has_next


def _conv3_rows(u, c_ref, is_ctx):
    n = u.shape[0] - 2 * SUBLANE
    up = u[SUBLANE - 1:SUBLANE - 1 + n]
    mid = u[SUBLANE:SUBLANE + n]
    dn = u[SUBLANE + 1:SUBLANE + 1 + n]
    row8 = lax.broadcasted_iota(jnp.int32, (SUBLANE, 1), 0) + jnp.where(is_ctx, 0, 2 * SUBLANE)
    for p in range(CTX_LEN, n, CTX_LEN):
        up = jnp.concatenate([up[:p], jnp.where(row8 == 0, 0.0, up[p:p + SUBLANE]),
                              up[p + SUBLANE:]], axis=0)
        dn = jnp.concatenate([dn[:p - SUBLANE],
                              jnp.where(row8 == SUBLANE - 1, 0.0, dn[p - SUBLANE:p]), dn[p:]], axis=0)
    return c_ref[0:1, :] * up + c_ref[1:2, :] * mid + c_ref[2:3, :] * dn


def _first_axis(i, *_):
    return i


def _halo_specs(rows, width, n_rows, col, tile=_first_axis):
    r8 = rows // SUBLANE
    last8 = n_rows // SUBLANE - 1
    return [
        pl.BlockSpec((rows, width), lambda *g: (tile(*g), col)),
        pl.BlockSpec((SUBLANE, width), lambda *g: (jnp.maximum(tile(*g) * r8 - 1, 0), col)),
        pl.BlockSpec((SUBLANE, width), lambda *g: (jnp.minimum((tile(*g) + 1) * r8, last8), col)),
    ]


def _layer_spec(l, shape):
    return pl.BlockSpec((None,) + shape, lambda *_: (l,) + (0,) * len(shape))


def _mod_spec(l, rows, tile=_first_axis):
    def index(*g):
        r = tile(*g) * rows
        return (l, jnp.where(r < NX, r // SEQ, BATCH), 0, 0)
    return pl.BlockSpec((None, None, 6, D_MODEL), index)


def _ada_kernel(c_ref, w_ref, b_ref, o_ref):
    c = c_ref[...]
    a = c * jax.nn.sigmoid(c)
    o_ref[...] = jnp.dot(a.astype(BF16), w_ref[...].astype(BF16),
                         preferred_element_type=F32) + b_ref[...]


def _ada(cc, ada_w, ada_b):
    n = ada_w.shape[-1]
    return pl.pallas_call(
        _ada_kernel,
        grid=(DEPTH, n // TN_ADA),
        in_specs=[
            pl.BlockSpec((8, D_MODEL), lambda l, j: (0, 0)),
            pl.BlockSpec((None, D_MODEL, TN_ADA), lambda l, j: (l, 0, j)),
            pl.BlockSpec((None, 1, TN_ADA), lambda l, j: (l, 0, j)),
        ],
        out_specs=pl.BlockSpec((None, 8, TN_ADA), lambda l, j: (l, 0, j)),
        out_shape=jax.ShapeDtypeStruct((DEPTH, 8, n), F32),
        compiler_params=_params(("arbitrary", "arbitrary")),
        name="ada",
    )(cc, ada_w, ada_b.reshape(DEPTH, 1, n))


def _in_kernel(x_ref, mod_ref, g_ref, w_ref, z_ref, h_ref):
    @pl.when(pl.program_id(1) == 0)
    def _():
        y = _rms(x_ref[...], g_ref[...])
        h_ref[...] = (y * (1.0 + mod_ref[1:2, :]) + mod_ref[0:1, :]).astype(BF16)

    z_ref[...] = jnp.dot(h_ref[...], w_ref[...], preferred_element_type=F32)


def _in_proj(l, tok, mods, g, w):
    n = tok.shape[0]
    return pl.pallas_call(
        _in_kernel,
        grid=(n // TM, IN_COLS_PAD // TN_IN),
        in_specs=[
            pl.BlockSpec((TM, D_MODEL), lambda i, j: (i, 0)),
            _mod_spec(l, TM),
            _layer_spec(l, (1, D_MODEL)),
            pl.BlockSpec((None, D_MODEL, TN_IN), lambda i, j: (l, 0, j)),
        ],
        out_specs=pl.BlockSpec((TM, TN_IN), lambda i, j: (i, j)),
        out_shape=jax.ShapeDtypeStruct((n, IN_COLS_PAD), F32),
        scratch_shapes=[pltpu.VMEM((TM, D_MODEL), BF16)],
        compiler_params=_params(("arbitrary", "arbitrary")),
        name="in_proj",
    )(tok, mods, g, w)


def _pool_conv_kernel(a_ref, ap_ref, an_ref, db_ref, dc_ref, dcp_ref, dcn_ref,
                      dh_ref, dhp_ref, dhn_ref, pw_ref, ps_ref, cw_ref,
                      ya_ref, yd_ref, e_ref):
    i = pl.program_id(0)
    pos, seqlen = _seq_pos(i, TS)
    pf, nf = _tile_edges(i, TS)

    e_ref[0:8, :] = ap_ref[...] * pf
    e_ref[8:8 + TS, :] = a_ref[...]
    e_ref[8 + TS:16 + TS, :] = an_ref[...] * nf
    for g, w in enumerate(POOL_WINDOWS):
        half = w // 2
        cs = slice(g * LANE, (g + 1) * LANE)
        acc = e_ref[8 - half:8 - half + TS, cs]
        for k in range(1, w):
            acc = acc + e_ref[8 - half + k:8 - half + k + TS, cs]
        cnt = (jnp.minimum(pos + half, seqlen) - jnp.maximum(pos - half, 0)).astype(F32)
        pooled = acc / cnt - a_ref[:, cs]
        y = jnp.dot(pooled.astype(BF16), pw_ref[g], preferred_element_type=F32)
        ya_ref[:, cs] = (y * ps_ref[:, cs]).astype(BF16)

    p = jnp.concatenate([dcp_ref[...] * dhp_ref[...] * pf, dc_ref[...] * dh_ref[...],
                         dcn_ref[...] * dhn_ref[...] * nf], axis=0)
    yd_ref[...] = (db_ref[...] * _conv3_rows(p, cw_ref, False)).astype(BF16)


def _pool_conv(l, z, n_rows, pool_w, pool_scale, conv_w):
    ca, cb, cc, ch = (COL_A // GROUP_W, COL_DB // GROUP_W, COL_DC // GROUP_W, COL_DH // GROUP_W)
    out = jax.ShapeDtypeStruct((n_rows, GROUP_W), BF16)
    return pl.pallas_call(
        _pool_conv_kernel,
        grid=(n_rows // TS,),
        in_specs=(_halo_specs(TS, GROUP_W, NT, ca)
                  + [pl.BlockSpec((TS, GROUP_W), lambda i: (i, cb))]
                  + _halo_specs(TS, GROUP_W, NT, cc) + _halo_specs(TS, GROUP_W, NT, ch)
                  + [_layer_spec(l, (len(POOL_WINDOWS), LANE, LANE)),
                     _layer_spec(l, (1, GROUP_W)), _layer_spec(l, (3, GROUP_W))]),
        out_specs=[pl.BlockSpec((TS, GROUP_W), lambda i: (i, 0))] * 2,
        out_shape=[out, out],
        scratch_shapes=[pltpu.VMEM((TS + 16, GROUP_W), F32)],
        compiler_params=_params(("arbitrary",)),
        name="pool_conv",
    )(z, z, z, z, z, z, z, z, z, z, pool_w, pool_scale, conv_w)


CBLK = NX // CTX_LEN


def _nt_dot(a, b):
    return lax.dot_general(a, b, (((1,), (1,)), ((), ())), preferred_element_type=F32)


def _softmax_pv(s, v):
    e = jnp.exp(s - jnp.max(s, axis=-1, keepdims=True)).astype(BF16)
    return jnp.dot(e, v, preferred_element_type=F32)


def _with_ones(v):
    return jnp.concatenate([v, jnp.ones(v.shape, v.dtype)], axis=-1)


def _ctx_call(kernel_fn, name, ins, in_specs, y):
    n_in = len(ins)
    return pl.pallas_call(
        kernel_fn,
        grid=(BATCH, HEADS),
        in_specs=in_specs + [pl.BlockSpec(memory_space=pl.ANY)],
        out_specs=pl.BlockSpec((CTX_LEN, LANE), lambda b, h: (CBLK + b, h)),
        out_shape=jax.ShapeDtypeStruct(y.shape, y.dtype),
        input_output_aliases={n_in: 0},
        compiler_params=_params(("arbitrary", "arbitrary")),
        name=name,
    )(*ins, y)


def _diff_lambda(lam_ref, lam_init):
    lp = lam_ref[...]
    return (jnp.exp(jnp.sum(lp[0:1] * lp[1:2], keepdims=True))
            - jnp.exp(jnp.sum(lp[2:3] * lp[3:4], keepdims=True)) + lam_init)


def _diff_chain(q, k, v, lam, g, lam_init):
    q = q * (64 ** -0.5)
    lane = lax.broadcasted_iota(jnp.int32, q.shape, 1)
    o1 = _softmax_pv(_nt_dot(jnp.where(lane < 64, q, 0.0).astype(BF16), k[...]), v[...])
    o2 = _softmax_pv(_nt_dot(jnp.where(lane >= 64, q, 0.0).astype(BF16), k[...]), v[...])
    o = o1[:, :LANE] / o1[:, LANE:] - o2[:, :LANE] * (lam / o2[:, LANE:])
    return (_rms(o, g) * (1.0 - lam_init)).astype(BF16)


def _diff_kernel(lam_init, q_ref, kx_ref, kc_ref, vx_ref, vc_ref, cos_ref, sin_ref, lam_ref, g_ref,
                 o_ref, kb_ref, vb_ref):
    kb_ref[0:SEQ, :] = _rope(kx_ref[...], cos_ref[...], sin_ref[...]).astype(BF16)
    kb_ref[SEQ:, :] = kc_ref[...].astype(BF16)
    vb_ref[0:SEQ, 0:LANE] = vx_ref[...].astype(BF16)
    vb_ref[SEQ:, 0:LANE] = vc_ref[...].astype(BF16)
    vb_ref[:, LANE:] = jnp.ones((SEQ + CTX_LEN, LANE), BF16)
    lam = _diff_lambda(lam_ref, lam_init)
    for r in range(0, SEQ, QSUB):
        rows = slice(r, r + QSUB)
        q = _rope(q_ref[rows, :], cos_ref[rows, :], sin_ref[rows, :])
        o_ref[rows, :] = _diff_chain(q, kb_ref, vb_ref, lam, g_ref[...], lam_init)


def _diff_ctx_kernel(lam_init, q_ref, k_ref, v_ref, lam_ref, g_ref, y_ref, o_ref):
    del y_ref
    o_ref[...] = _diff_chain(q_ref[...], k_ref[...].astype(BF16),
                             _with_ones(v_ref[...].astype(BF16)),
                             _diff_lambda(lam_ref, lam_init), g_ref[...], lam_init)


def _diff_attn(l, z, cos_t, sin_t, lam_p, subln_g, lam_init, with_ctx):
    cq, ck, cv = COL_Q // LANE, COL_K // LANE, COL_V // LANE
    table = pl.BlockSpec((SEQ, LANE), lambda b, h: (0, 0))
    y = pl.pallas_call(
        functools.partial(_diff_kernel, lam_init),
        grid=(BATCH, HEADS),
        in_specs=[
            pl.BlockSpec((SEQ, LANE), lambda b, h: (b, cq + h)),
            pl.BlockSpec((SEQ, LANE), lambda b, h: (b, ck + h)),
            pl.BlockSpec((CTX_LEN, LANE), lambda b, h: (CBLK + b, ck + h)),
            pl.BlockSpec((SEQ, LANE), lambda b, h: (b, cv + h)),
            pl.BlockSpec((CTX_LEN, LANE), lambda b, h: (CBLK + b, cv + h)),
            table, table,
            _layer_spec(l, (4, 64)),
            _layer_spec(l, (1, LANE)),
        ],
        out_specs=pl.BlockSpec((SEQ, LANE), lambda b, h: (b, h)),
        out_shape=jax.ShapeDtypeStruct((NT if with_ctx else NX, GROUP_W), BF16),
        scratch_shapes=[pltpu.VMEM((SEQ + CTX_LEN, LANE), BF16),
                        pltpu.VMEM((SEQ + CTX_LEN, 2 * LANE), BF16)],
        compiler_params=_params(("arbitrary", "arbitrary")),
        name="diff_attn",
    )(z, z, z, z, z, cos_t, sin_t, lam_p, subln_g)
    if not with_ctx:
        return y
    return _ctx_call(
        functools.partial(_diff_ctx_kernel, lam_init), "diff_attn_ctx",
        [z, z, z, lam_p, subln_g],
        [pl.BlockSpec((CTX_LEN, LANE), lambda b, h: (CBLK + b, cq + h)),
         pl.BlockSpec((CTX_LEN, LANE), lambda b, h: (CBLK + b, ck + h)),
         pl.BlockSpec((CTX_LEN, LANE), lambda b, h: (CBLK + b, cv + h)),
         _layer_spec(l, (4, 64)),
         _layer_spec(l, (1, LANE))],
        y)


def _mla_prep_kernel(ckv_ref, cq_ref, kr_ref, gq_ref, gkv_ref, wq_ref, wkv_ref, cos_ref, sin_ref,
                     qn_ref, qr_ref, kn_ref, v_ref, kr2_ref):
    cos = cos_ref[...]
    sin = sin_ref[...]
    q = jnp.dot(_rms(cq_ref[...], gq_ref[...]).astype(BF16), wq_ref[...],
                preferred_element_type=F32)
    qn_ref[...] = q[:, :GROUP_W].astype(BF16)
    for c in range(0, 2 * LANE, LANE):
        qr_ref[:, c:c + LANE] = _rope(q[:, GROUP_W + c:GROUP_W + c + LANE], cos, sin).astype(BF16)
    kv = jnp.dot(_rms(ckv_ref[...], gkv_ref[...]).astype(BF16), wkv_ref[...],
                 preferred_element_type=F32)
    kn_ref[...] = kv[:, :GROUP_W].astype(BF16)
    v_ref[...] = kv[:, GROUP_W:].astype(BF16)
    kr = _rope(kr_ref[...], cos, sin)
    kr2_ref[...] = (kr + pltpu.roll(kr, 64, 1)).astype(BF16)


def _mla_prep(l, z, gq, gkv, wq, wkv, cos_t, sin_t):
    def out(w):
        return jax.ShapeDtypeStruct((NT, w), BF16)

    def row(w):
        return pl.BlockSpec((TM, w), lambda i: (i, 0))

    rope = pl.BlockSpec((TM, LANE), lambda i: (jnp.where(i * TM < NX, i % (SEQ // TM), SEQ // TM), 0))
    return pl.pallas_call(
        _mla_prep_kernel,
        grid=(NT // TM,),
        in_specs=[
            pl.BlockSpec((TM, 256), lambda i: (i, COL_CKV // 256)),
            pl.BlockSpec((TM, 384), lambda i: (i, COL_CQ // 384)),
            pl.BlockSpec((TM, LANE), lambda i: (i, COL_KR // LANE)),
            _layer_spec(l, (1, 384)),
            _layer_spec(l, (1, 256)),
            _layer_spec(l, (384, 768)),
            _layer_spec(l, (256, 1024)),
            rope, rope,
        ],
        out_specs=[row(GROUP_W), row(2 * LANE), row(GROUP_W), row(GROUP_W), row(LANE)],
        out_shape=[out(GROUP_W), out(2 * LANE), out(GROUP_W), out(GROUP_W), out(LANE)],
        compiler_params=_params(("arbitrary",)),
        name="mla_prep",
    )(z, z, z, gq, gkv, wq, wkv, cos_t, sin_t)


def _mla_chain(h, qn, qr, k, v):
    lane = lax.broadcasted_iota(jnp.int32, qr.shape, 1)
    lo = (h % 2) * 64
    qr = jnp.where((lane >= lo) & (lane < lo + 64), qr, jnp.zeros_like(qr))
    s = _nt_dot(jnp.concatenate([qn, qr], axis=-1), k[...]) * ((128 + 64) ** -0.5)
    o = _softmax_pv(s, v[...])
    return (o[:, :LANE] / o[:, LANE:]).astype(BF16)


def _mla_kernel(qn_ref, qr_ref, knx_ref, knc_ref, krx_ref, krc_ref, vx_ref, vc_ref,
                o_ref, kb_ref, vb_ref):
    h = pl.program_id(1)
    kb_ref[0:SEQ, 0:LANE] = knx_ref[...]
    kb_ref[SEQ:, 0:LANE] = knc_ref[...]
    kb_ref[0:SEQ, LANE:] = krx_ref[...]
    kb_ref[SEQ:, LANE:] = krc_ref[...]
    vb_ref[0:SEQ, 0:LANE] = vx_ref[...]
    vb_ref[SEQ:, 0:LANE] = vc_ref[...]
    vb_ref[:, LANE:] = jnp.ones((SEQ + CTX_LEN, LANE), BF16)
    for r in range(0, SEQ, QSUB):
        rows = slice(r, r + QSUB)
        o_ref[rows, :] = _mla_chain(h, qn_ref[rows, :], qr_ref[rows, :], kb_ref, vb_ref)


def _mla_ctx_kernel(qn_ref, qr_ref, kn_ref, kr_ref, v_ref, y_ref, o_ref):
    del y_ref
    k = jnp.concatenate([kn_ref[...], kr_ref[...]], axis=-1)
    o_ref[...] = _mla_chain(pl.program_id(1), qn_ref[...], qr_ref[...], k, _with_ones(v_ref[...]))


def _mla_attn(qn, qr, kn, v, kr2, with_ctx):
    def xs(col):
        return pl.BlockSpec((SEQ, LANE), lambda b, h: (b, col(h)))

    def cs(col):
        return pl.BlockSpec((CTX_LEN, LANE), lambda b, h: (CBLK + b, col(h)))

    head, pair, first = (lambda h: h), (lambda h: h // 2), (lambda h: 0)
    y = pl.pallas_call(
        _mla_kernel,
        grid=(BATCH, HEADS),
        in_specs=[xs(head), xs(pair), xs(head), cs(head), xs(first), cs(first), xs(head), cs(head)],
        out_specs=xs(head),
        out_shape=jax.ShapeDtypeStruct((NT if with_ctx else NX, GROUP_W), BF16),
        scratch_shapes=[pltpu.VMEM((SEQ + CTX_LEN, 2 * LANE), BF16),
                        pltpu.VMEM((SEQ + CTX_LEN, 2 * LANE), BF16)],
        compiler_params=_params(("arbitrary", "arbitrary")),
        name="mla_attn",
    )(qn, qr, kn, kn, kr2, kr2, v, v)
    if not with_ctx:
        return y
    return _ctx_call(_mla_ctx_kernel, "mla_attn_ctx", [qn, qr, kn, kr2, v],
                     [cs(head), cs(pair), cs(head), cs(first), cs(head)], y)


def _out_kernel(ya_ref, yb_ref, ym_ref, yd_ref, w_ref, x_ref, mod_ref, gpost_ref, xo_ref, wb_ref):
    @pl.when(pl.program_id(0) == 0)
    def _():
        wb_ref[...] = w_ref[...].astype(BF16)

    y = jnp.dot(ya_ref[...], wb_ref[0:GROUP_W, :], preferred_element_type=F32)
    y += jnp.dot(yb_ref[...], wb_ref[GROUP_W:2 * GROUP_W, :], preferred_element_type=F32)
    y += jnp.dot(ym_ref[...], wb_ref[2 * GROUP_W:3 * GROUP_W, :], preferred_element_type=F32)
    y += jnp.dot(yd_ref[...], wb_ref[3 * GROUP_W:, :], preferred_element_type=F32)
    xo_ref[...] = x_ref[...] + mod_ref[2:3, :] * _rms(y, gpost_ref[...])


def _out_proj(l, ya, yb, ym, yd, w, tok, mods, g_post):
    n = ya.shape[0]
    y_spec = pl.BlockSpec((TM, GROUP_W), lambda i: (i, 0))
    row = pl.BlockSpec((TM, D_MODEL), lambda i: (i, 0))
    return pl.pallas_call(
        _out_kernel,
        grid=(n // TM,),
        in_specs=[
            y_spec, y_spec, y_spec, y_spec,
            pl.BlockSpec((None, D_MODEL, D_MODEL), lambda i: (l, 0, 0),
                         pipeline_mode=pl.Buffered(1)),
            row,
            _mod_spec(l, TM),
            _layer_spec(l, (1, D_MODEL)),
        ],
        out_specs=row,
        out_shape=jax.ShapeDtypeStruct((n, D_MODEL), F32),
        scratch_shapes=[pltpu.VMEM((D_MODEL, D_MODEL), BF16)],
        compiler_params=_params(("arbitrary",)),
        name="out_proj",
    )(ya, yb, ym, yd, w, tok, mods, g_post)


NJ = D_FF // TF


def _ffn_kernel(n_chunks, xu_ref, xup_ref, xun_ref, modu_ref, xd_ref, modd_ref, gpre_ref, gpost_ref,
                wg_ref, wv_ref, cg_ref, cv_ref, wd_ref, xo_ref, hx_ref, u_ref, act_ref, acc_ref):
    s = pl.program_id(0)
    su = jnp.minimum(s, n_chunks - 1)
    sc = jnp.clip(s - 1, 0, n_chunks - 1)
    sd = jnp.clip(s - 2, 0, n_chunks - 1)
    slot = s % 2

    @pl.when(s == 0)
    def _():
        u_ref[...] = jnp.zeros_like(u_ref)
        act_ref[...] = jnp.zeros_like(act_ref)

    @pl.when(jnp.logical_and(su % NJ == 0, s < n_chunks))
    def _():
        has_prev, has_next = _tile_edges(su // NJ, TM)

        def hmod(x):
            return _rms(x, gpre_ref[...]) * (1.0 + modu_ref[4:5, :]) + modu_ref[3:4, :]

        hx_ref[...] = jnp.concatenate(
            [hmod(xup_ref[...]) * has_prev, hmod(xu_ref[...]), hmod(xun_ref[...]) * has_next],
            axis=0).astype(BF16)

    @pl.when(sd % NJ == 0)
    def _():
        acc_ref[...] = jnp.zeros_like(acc_ref)

    acc_ref[...] += jnp.dot(act_ref[1 - slot], wd_ref[...], preferred_element_type=F32)

    is_ctx = (sc // NJ) * TM >= NX
    gate = _conv3_rows(u_ref[0], cg_ref, is_ctx)
    val = _conv3_rows(u_ref[1], cv_ref, is_ctx)
    act_ref[slot] = (gate * jax.nn.sigmoid(gate) * val).astype(BF16)

    hx = hx_ref[...]
    u_ref[0] = jnp.dot(hx, wg_ref[...], preferred_element_type=F32)
    u_ref[1] = jnp.dot(hx, wv_ref[...], preferred_element_type=F32)

    @pl.when(jnp.logical_and(sd % NJ == NJ - 1, s >= 2))
    def _():
        xo_ref[...] = xd_ref[...] + modd_ref[5:6, :] * _rms(acc_ref[...], gpost_ref[...])


def _ffn(l, tok, mods, g_pre, g_post, w_up, conv_w, w_down, n_rows):
    n_chunks = (n_rows // TM) * NJ

    def up(s):
        return jnp.minimum(s, n_chunks - 1)

    def conv(s):
        return jnp.clip(s - 1, 0, n_chunks - 1)

    def down(s):
        return jnp.clip(s - 2, 0, n_chunks - 1)

    return pl.pallas_call(
        functools.partial(_ffn_kernel, n_chunks),
        grid=(n_chunks + 2,),
        in_specs=(_halo_specs(TM, D_MODEL, n_rows, 0, lambda s: up(s) // NJ) + [
            _mod_spec(l, TM, lambda s: up(s) // NJ),
            pl.BlockSpec((TM, D_MODEL), lambda s: (down(s) // NJ, 0), pipeline_mode=pl.Buffered(1)),
            _mod_spec(l, TM, lambda s: down(s) // NJ),
            _layer_spec(l, (1, D_MODEL)),
            _layer_spec(l, (1, D_MODEL)),
            pl.BlockSpec((None, D_MODEL, TF), lambda s: (l, 0, up(s) % NJ)),
            pl.BlockSpec((None, D_MODEL, TF), lambda s: (l, 0, up(s) % NJ + NJ)),
            pl.BlockSpec((None, 3, TF), lambda s: (l, 0, conv(s) % NJ)),
            pl.BlockSpec((None, 3, TF), lambda s: (l, 0, conv(s) % NJ + NJ)),
            pl.BlockSpec((None, TF, D_MODEL), lambda s: (l, down(s) % NJ, 0)),
        ]),
        out_specs=pl.BlockSpec((TM, D_MODEL), lambda s: (down(s) // NJ, 0)),
        out_shape=jax.ShapeDtypeStruct((n_rows, D_MODEL), F32),
        scratch_shapes=[pltpu.VMEM((TM + 2 * SUBLANE, D_MODEL), BF16),
                        pltpu.VMEM((2, TM + 2 * SUBLANE, TF), F32),
                        pltpu.VMEM((2, TM, TF), BF16),
                        pltpu.VMEM((TM, D_MODEL), F32)],
        compiler_params=_params(("arbitrary",)),
        name="ffn",
    )(tok, tok, tok, mods, tok, mods, g_pre, g_post, w_up, w_up, conv_w, conv_w, w_down)


def _rope_tables():
    rows = SEQ // GRID_W
    row = jnp.repeat(jnp.arange(rows), GRID_W).astype(F32)
    col = jnp.tile(jnp.arange(GRID_W), rows).astype(F32)
    n_freq = 64 // 4
    inv = ROPE_THETA ** (-jnp.arange(n_freq, dtype=F32) / n_freq)
    ang = jnp.concatenate([row[:, None] * inv, col[:, None] * inv], axis=-1)
    cos, sin = jnp.cos(ang), jnp.sin(ang)
    cos_t = jnp.concatenate([cos, cos, cos, cos], axis=-1)
    sin_t = jnp.concatenate([-sin, sin, -sin, sin], axis=-1)
    cos_t = jnp.concatenate([cos_t, jnp.ones((TM, LANE), F32)], axis=0)
    sin_t = jnp.concatenate([sin_t, jnp.zeros((TM, LANE), F32)], axis=0)
    return cos_t, sin_t


def _layout_w_in(w):
    a, qkv, cq, ckv, kr, d = jnp.split(w, [512, 2048, 2432, 2688, 2752], axis=-1)
    pad = jnp.zeros(w.shape[:-1] + (IN_COLS_PAD - COL_KR - 64,), w.dtype)
    return jnp.concatenate([a, qkv, d, ckv, cq, kr, pad], axis=-1).astype(BF16)


def _layout_w_uq(w):
    w = w.reshape(DEPTH, 384, HEADS, 192)
    return jnp.concatenate([w[..., :128].reshape(DEPTH, 384, 512),
                            w[..., 128:].reshape(DEPTH, 384, 256)], axis=-1).astype(BF16)


def _layout_w_ukv(w):
    w = w.reshape(DEPTH, 256, HEADS, 256)
    return jnp.concatenate([w[..., :128].reshape(DEPTH, 256, 512),
                            w[..., 128:].reshape(DEPTH, 256, 512)], axis=-1).astype(BF16)


def kernel(x, c, ctx, c_ctx, ada_w, ada_b, g_pre_mix, g_post_mix, g_pre_ffn, g_post_ffn, w_in,
           pool_w, pool_scale, diff_lambda, diff_subln_g, mla_gq, mla_w_uq, mla_gkv, mla_w_ukv,
           conv_w, w_out, ffn_w_up, ffn_conv_w, ffn_w_down):
    tok = jnp.concatenate([x.reshape(NX, D_MODEL), ctx.reshape(NC, D_MODEL)], axis=0)
    cc = jnp.concatenate([c, c_ctx[None, :], jnp.zeros((3, D_MODEL), F32)], axis=0)
    mods = _ada(cc, ada_w, ada_b).reshape(DEPTH, 8, 6, D_MODEL)
    cos_t, sin_t = _rope_tables()

    def vec(p):
        return p.reshape(DEPTH, 1, p.shape[-1])

    w_in_b = _layout_w_in(w_in)
    w_uq_b, w_ukv_b = _layout_w_uq(mla_w_uq), _layout_w_ukv(mla_w_ukv)
    pool_w_b = pool_w.astype(BF16)
    w_up_b, w_down_b = ffn_w_up.astype(BF16), ffn_w_down.astype(BF16)

    for l in range(DEPTH):
        last = l == DEPTH - 1
        n_rows = NX if last else NT
        lam_init = 0.8 - 0.6 * math.exp(-0.3 * l)

        z = _in_proj(l, tok, mods, vec(g_pre_mix), w_in_b)
        ya, yd = _pool_conv(l, z, n_rows, pool_w_b, vec(pool_scale), conv_w)
        yb = _diff_attn(l, z, cos_t, sin_t, diff_lambda, vec(diff_subln_g), lam_init, not last)
        qn, qr, kn, v, kr2 = _mla_prep(l, z, vec(mla_gq), vec(mla_gkv), w_uq_b, w_ukv_b, cos_t, sin_t)
        ym = _mla_attn(qn, qr, kn, v, kr2, not last)
        tok = _out_proj(l, ya, yb, ym, yd, w_out, tok, mods, vec(g_post_mix))
        tok = _ffn(l, tok, mods, vec(g_pre_ffn), vec(g_post_ffn), w_up_b, ffn_conv_w, w_down_b, n_rows)
    return tok.reshape(BATCH, SEQ, D_MODEL)
```

```python
import functools
import math

import jax
import jax.numpy as jnp
from jax import lax
from jax.experimental import pallas as pl
from jax.experimental.pallas import tpu as pltpu

F32 = jnp.float32
BF16 = jnp.bfloat16

D_MODEL = 2048
BATCH = 4
SEQ = 2048
CTX_LEN = 256
GRID_W = 64
DEPTH = 2
NORM_EPS = 1e-6
ROPE_THETA = 10000.0
GROUP_W = 512
POOL_WINDOWS = (2, 4, 8, 16)
HEADS = 4
D_FF = 5632

NX = BATCH * SEQ
NC = BATCH * CTX_LEN
NT = NX + NC

COL_A, COL_Q, COL_K, COL_V = 0, 512, 1024, 1536
COL_DB, COL_DC, COL_DH = 2048, 2560, 3072
COL_CKV, COL_CQ, COL_KR = 3584, 3840, 4224
IN_COLS_PAD = 4608

LANE = 128
SUBLANE = 8
BF16_ROWS = 2 * SUBLANE
TM = 512
TM_BIG = 1024
TS = 256
QSUB = 256
TN_IN = IN_COLS_PAD // 4
TF = 512
TN_ADA = 1024
VMEM_LIMIT = 56 * 1024 * 1024
VMEM_LIMIT_FFN = 62 * 1024 * 1024


def _params(sem, vmem_limit=VMEM_LIMIT):
    return pltpu.CompilerParams(dimension_semantics=sem, vmem_limit_bytes=vmem_limit)


def _split_rows_specs(rows, n_lat_rows):
    n_lat = n_lat_rows // rows
    return [pl.BlockSpec((rows, D_MODEL), lambda i, *_: (jnp.minimum(i, n_lat - 1), 0)),
            pl.BlockSpec((rows, D_MODEL), lambda i, *_: (jnp.maximum(i - n_lat, 0), 0),
                         pipeline_mode=pl.Buffered(1))]


def _pick_rows(i, rows, x_ref, c_ref, fn):
    if c_ref is None:
        fn(x_ref)
        return

    @pl.when(i * rows < NX)
    def _():
        fn(x_ref)

    @pl.when(i * rows >= NX)
    def _():
        fn(c_ref)


def _rms(x, g):
    ms = jnp.mean(x * x, axis=-1, keepdims=True)
    return x * lax.rsqrt(ms + NORM_EPS) * g


def _rope(v, cos, sin):
    lane = lax.broadcasted_iota(jnp.int32, v.shape, 1)
    sw = jnp.where((lane & 32) == 0, pltpu.roll(v, 96, 1), pltpu.roll(v, 32, 1))
    return v * cos + sw * sin


def _seq_pos(tile, rows):
    r = tile * rows + lax.broadcasted_iota(jnp.int32, (rows, 1), 0)
    seqlen = jnp.where(r < NX, SEQ, CTX_LEN)
    return r & (seqlen - 1), seqlen


def _tile_edges(tile, rows):
    r0 = tile * rows
    seqlen = jnp.where(r0 < NX, SEQ, CTX_LEN)
    has_prev = jnp.where((r0 & (seqlen - 1)) == 0, 0.0, 1.0)
    has_next = jnp.where(((r0 + rows) & (seqlen - 1)) == 0, 0.0, 1.0)
    return has_prev, has_next


def _conv3_rows(u, prev_row, next_row, c_ref, is_ctx):
    n = u.shape[0]
    row8 = lax.broadcasted_iota(jnp.int32, (SUBLANE, 1), 0)
    prev = pltpu.roll(u, 1, 0)
    prev = jnp.concatenate([jnp.where(row8 == 0, prev_row, prev[:SUBLANE]), prev[SUBLANE:]], axis=0)
    nxt = pltpu.roll(u, n - 1, 0)
    nxt = jnp.concatenate([nxt[:n - SUBLANE],
                           jnp.where(row8 == SUBLANE - 1, next_row, nxt[n - SUBLANE:])], axis=0)
    ctx_row = row8 + jnp.where(is_ctx, 0, 2 * SUBLANE)
    for p in range(CTX_LEN, n, CTX_LEN):
        prev = jnp.concatenate([prev[:p], jnp.where(ctx_row == 0, 0.0, prev[p:p + SUBLANE]),
                                prev[p + SUBLANE:]], axis=0)
        nxt = jnp.concatenate([nxt[:p - SUBLANE],
                               jnp.where(ctx_row == SUBLANE - 1, 0.0, nxt[p - SUBLANE:p]), nxt[p:]],
                              axis=0)
    return c_ref[0:1, :] * prev + c_ref[1:2, :] * u + c_ref[2:3, :] * nxt


def _first_axis(i, *_):
    return i


def _halo_specs(rows, width, n_rows, col, tile=_first_axis):
    r8 = rows // SUBLANE
    last8 = n_rows // SUBLANE - 1
    return [
        pl.BlockSpec((rows, width), lambda *g: (tile(*g), col)),
        pl.BlockSpec((SUBLANE, width), lambda *g: (jnp.maximum(tile(*g) * r8 - 1, 0), col)),
        pl.BlockSpec((SUBLANE, width), lambda *g: (jnp.minimum((tile(*g) + 1) * r8, last8), col)),
    ]


def _layer_spec(l, shape):
    return pl.BlockSpec((None,) + shape, lambda *_: (l,) + (0,) * len(shape))


def _mod_spec(l, rows, tile=_first_axis):
    def index(*g):
        r = tile(*g) * rows
        return (l, jnp.where(r < NX, r // SEQ, BATCH), 0, 0)
    return pl.BlockSpec((None, None, 6, D_MODEL), index)


def _ada_kernel(c_ref, w_ref, b_ref, o_ref):
    c = c_ref[...]
    a = c * jax.nn.sigmoid(c)
    o_ref[...] = jnp.dot(a.astype(BF16), w_ref[...].astype(BF16),
                         preferred_element_type=F32) + b_ref[...]


def _ada(cc, ada_w, ada_b):
    n = ada_w.shape[-1]
    return pl.pallas_call(
        _ada_kernel,
        grid=(DEPTH, n // TN_ADA),
        in_specs=[
            pl.BlockSpec((8, D_MODEL), lambda l, j: (0, 0)),
            pl.BlockSpec((None, D_MODEL, TN_ADA), lambda l, j: (l, 0, j)),
            pl.BlockSpec((None, 1, TN_ADA), lambda l, j: (l, 0, j)),
        ],
        out_specs=pl.BlockSpec((None, 8, TN_ADA), lambda l, j: (l, 0, j)),
        out_shape=jax.ShapeDtypeStruct((DEPTH, 8, n), F32),
        compiler_params=_params(("arbitrary", "arbitrary")),
        name="ada",
    )(cc, ada_w, ada_b.reshape(DEPTH, 1, n))


def _in_kernel(split, x_ref, *refs):
    c_ref, (mod_ref, g_ref, w_ref, z_ref, h_ref, r_ref) = ((refs[0], refs[1:]) if split
                                                              else (None, refs))

    @pl.when(pl.program_id(1) == 0)
    def _():
        def put(src):
            x = src[...]
            r_ref[...] = lax.rsqrt(jnp.mean(x * x, axis=-1, keepdims=True) + NORM_EPS)

            shape = (BF16_ROWS, D_MODEL)
            g = jnp.broadcast_to(g_ref[...], shape)
            scale = jnp.broadcast_to(1.0 + mod_ref[1:2, :], shape)
            shift = jnp.broadcast_to(mod_ref[0:1, :], shape)

            def slab(k, carry):
                rows = pl.ds(pl.multiple_of(k * BF16_ROWS, BF16_ROWS), BF16_ROWS)
                h_ref[rows, :] = (src[rows, :] * r_ref[rows, :] * g * scale + shift).astype(BF16)
                return carry

            lax.fori_loop(0, TM_BIG // BF16_ROWS, slab, 0, unroll=4)

        _pick_rows(pl.program_id(0), TM_BIG, x_ref, c_ref, put)

    z_ref[...] = jnp.dot(h_ref[...], w_ref[...], preferred_element_type=F32)


def _in_proj(l, toks, mods, g, w):
    split = len(toks) == 2
    rows = (_split_rows_specs(TM_BIG, NX) if split
            else [pl.BlockSpec((TM_BIG, D_MODEL), lambda i, j: (i, 0))])
    return pl.pallas_call(
        functools.partial(_in_kernel, split),
        grid=(NT // TM_BIG, IN_COLS_PAD // TN_IN),
        in_specs=rows + [
            _mod_spec(l, TM_BIG),
            _layer_spec(l, (1, D_MODEL)),
            pl.BlockSpec((None, D_MODEL, TN_IN), lambda i, j: (l, 0, j)),
        ],
        out_specs=pl.BlockSpec((TM_BIG, TN_IN), lambda i, j: (i, j)),
        out_shape=jax.ShapeDtypeStruct((NT, IN_COLS_PAD), F32),
        scratch_shapes=[pltpu.VMEM((TM_BIG, D_MODEL), BF16), pltpu.VMEM((TM_BIG, 1), F32)],
        compiler_params=_params(("arbitrary", "arbitrary")),
        name="in_proj",
    )(*toks, mods, g, w)


def _pool_conv_kernel(a_ref, ap_ref, an_ref, db_ref, dc_ref, dcp_ref, dcn_ref,
                      dh_ref, dhp_ref, dhn_ref, pw_ref, ps_ref, cw_ref,
                      ya_ref, yd_ref, e_ref):
    i = pl.program_id(0)
    pos, seqlen = _seq_pos(i, TS)
    pf, nf = _tile_edges(i, TS)

    e_ref[0:8, :] = ap_ref[...] * pf
    e_ref[8:8 + TS, :] = a_ref[...]
    e_ref[8 + TS:16 + TS, :] = an_ref[...] * nf
    for g, w in enumerate(POOL_WINDOWS):
        half = w // 2
        cs = slice(g * LANE, (g + 1) * LANE)
        acc = e_ref[8 - half:8 - half + TS, cs]
        for k in range(1, w):
            acc = acc + e_ref[8 - half + k:8 - half + k + TS, cs]
        cnt = (jnp.minimum(pos + half, seqlen) - jnp.maximum(pos - half, 0)).astype(F32)
        pooled = acc / cnt - a_ref[:, cs]
        y = jnp.dot(pooled.astype(BF16), pw_ref[g], preferred_element_type=F32)
        ya_ref[:, cs] = (y * ps_ref[:, cs]).astype(BF16)

    p_prev = dcp_ref[SUBLANE - 1:SUBLANE, :] * dhp_ref[SUBLANE - 1:SUBLANE, :] * pf
    p_next = dcn_ref[0:1, :] * dhn_ref[0:1, :] * nf
    conv = _conv3_rows(dc_ref[...] * dh_ref[...], p_prev, p_next, cw_ref, False)
    yd_ref[...] = (db_ref[...] * conv).astype(BF16)


def _pool_conv(l, z, n_rows, pool_w, pool_scale, conv_w):
    ca, cb, cc, ch = (COL_A // GROUP_W, COL_DB // GROUP_W, COL_DC // GROUP_W, COL_DH // GROUP_W)
    out = jax.ShapeDtypeStruct((n_rows, GROUP_W), BF16)
    return pl.pallas_call(
        _pool_conv_kernel,
        grid=(n_rows // TS,),
        in_specs=(_halo_specs(TS, GROUP_W, NT, ca)
                  + [pl.BlockSpec((TS, GROUP_W), lambda i: (i, cb))]
                  + _halo_specs(TS, GROUP_W, NT, cc) + _halo_specs(TS, GROUP_W, NT, ch)
                  + [_layer_spec(l, (len(POOL_WINDOWS), LANE, LANE)),
                     _layer_spec(l, (1, GROUP_W)), _layer_spec(l, (3, GROUP_W))]),
        out_specs=[pl.BlockSpec((TS, GROUP_W), lambda i: (i, 0))] * 2,
        out_shape=[out, out],
        scratch_shapes=[pltpu.VMEM((TS + 16, GROUP_W), F32)],
        compiler_params=_params(("arbitrary",)),
        name="pool_conv",
    )(z, z, z, z, z, z, z, z, z, z, pool_w, pool_scale, conv_w)


CBLK = NX // CTX_LEN


def _nt_dot(a, b):
    return lax.dot_general(a, b, (((1,), (1,)), ((), ())), preferred_element_type=F32)


def _softmax_pv(s, v):
    e = jnp.exp(s - jnp.max(s, axis=-1, keepdims=True)).astype(BF16)
    return jnp.dot(e, v, preferred_element_type=F32)


def _with_ones(v):
    return jnp.concatenate([v, jnp.ones(v.shape, v.dtype)], axis=-1)


def _ctx_call(kernel_fn, name, ins, in_specs, y):
    n_in = len(ins)
    return pl.pallas_call(
        kernel_fn,
        grid=(BATCH, HEADS),
        in_specs=in_specs + [pl.BlockSpec(memory_space=pl.ANY)],
        out_specs=pl.BlockSpec((CTX_LEN, LANE), lambda b, h: (CBLK + b, h)),
        out_shape=jax.ShapeDtypeStruct(y.shape, y.dtype),
        input_output_aliases={n_in: 0},
        compiler_params=_params(("arbitrary", "arbitrary")),
        name=name,
    )(*ins, y)


def _diff_lambda(lam_ref, lam_init):
    lp = lam_ref[...]
    return (jnp.exp(jnp.sum(lp[0:1] * lp[1:2], keepdims=True))
            - jnp.exp(jnp.sum(lp[2:3] * lp[3:4], keepdims=True)) + lam_init)


def _diff_chain(q, k, v, lam, g, lam_init):
    q = q * (64 ** -0.5)
    lane = lax.broadcasted_iota(jnp.int32, q.shape, 1)
    o1 = _softmax_pv(_nt_dot(jnp.where(lane < 64, q, 0.0).astype(BF16), k[...]), v[...])
    o2 = _softmax_pv(_nt_dot(jnp.where(lane >= 64, q, 0.0).astype(BF16), k[...]), v[...])
    o = o1[:, :LANE] / o1[:, LANE:] - o2[:, :LANE] * (lam / o2[:, LANE:])
    return (_rms(o, g) * (1.0 - lam_init)).astype(BF16)


def _diff_kernel(lam_init, q_ref, kx_ref, kc_ref, vx_ref, vc_ref, cos_ref, sin_ref, lam_ref, g_ref,
                 o_ref, kb_ref, vb_ref):
    kb_ref[0:SEQ, :] = _rope(kx_ref[...], cos_ref[...], sin_ref[...]).astype(BF16)
    kb_ref[SEQ:, :] = kc_ref[...].astype(BF16)
    vb_ref[0:SEQ, 0:LANE] = vx_ref[...].astype(BF16)
    vb_ref[SEQ:, 0:LANE] = vc_ref[...].astype(BF16)
    vb_ref[:, LANE:] = jnp.ones((SEQ + CTX_LEN, LANE), BF16)
    lam = _diff_lambda(lam_ref, lam_init)
    for r in range(0, SEQ, QSUB):
        rows = slice(r, r + QSUB)
        q = _rope(q_ref[rows, :], cos_ref[rows, :], sin_ref[rows, :])
        o_ref[rows, :] = _diff_chain(q, kb_ref, vb_ref, lam, g_ref[...], lam_init)


def _diff_ctx_kernel(lam_init, q_ref, k_ref, v_ref, lam_ref, g_ref, y_ref, o_ref):
    del y_ref
    o_ref[...] = _diff_chain(q_ref[...], k_ref[...].astype(BF16),
                             _with_ones(v_ref[...].astype(BF16)),
                             _diff_lambda(lam_ref, lam_init), g_ref[...], lam_init)


def _diff_attn(l, z, cos_t, sin_t, lam_p, subln_g, lam_init, with_ctx):
    cq, ck, cv = COL_Q // LANE, COL_K // LANE, COL_V // LANE
    table = pl.BlockSpec((SEQ, LANE), lambda b, h: (0, 0))
    y = pl.pallas_call(
        functools.partial(_diff_kernel, lam_init),
        grid=(BATCH, HEADS),
        in_specs=[
            pl.BlockSpec((SEQ, LANE), lambda b, h: (b, cq + h)),
            pl.BlockSpec((SEQ, LANE), lambda b, h: (b, ck + h)),
            pl.BlockSpec((CTX_LEN, LANE), lambda b, h: (CBLK + b, ck + h)),
            pl.BlockSpec((SEQ, LANE), lambda b, h: (b, cv + h)),
            pl.BlockSpec((CTX_LEN, LANE), lambda b, h: (CBLK + b, cv + h)),
            table, table,
            _layer_spec(l, (4, 64)),
            _layer_spec(l, (1, LANE)),
        ],
        out_specs=pl.BlockSpec((SEQ, LANE), lambda b, h: (b, h)),
        out_shape=jax.ShapeDtypeStruct((NT if with_ctx else NX, GROUP_W), BF16),
        scratch_shapes=[pltpu.VMEM((SEQ + CTX_LEN, LANE), BF16),
                        pltpu.VMEM((SEQ + CTX_LEN, 2 * LANE), BF16)],
        compiler_params=_params(("arbitrary", "arbitrary")),
        name="diff_attn",
    )(z, z, z, z, z, cos_t, sin_t, lam_p, subln_g)
    if not with_ctx:
        return y
    return _ctx_call(
        functools.partial(_diff_ctx_kernel, lam_init), "diff_attn_ctx",
        [z, z, z, lam_p, subln_g],
        [pl.BlockSpec((CTX_LEN, LANE), lambda b, h: (CBLK + b, cq + h)),
         pl.BlockSpec((CTX_LEN, LANE), lambda b, h: (CBLK + b, ck + h)),
         pl.BlockSpec((CTX_LEN, LANE), lambda b, h: (CBLK + b, cv + h)),
         _layer_spec(l, (4, 64)),
         _layer_spec(l, (1, LANE))],
        y)


def _mla_prep_kernel(ckv_ref, cq_ref, kr_ref, gq_ref, gkv_ref, wq_ref, wkv_ref, cos_ref, sin_ref,
                     qn_ref, qr_ref, kn_ref, v_ref, kr2_ref):
    cos = cos_ref[...]
    sin = sin_ref[...]
    q = jnp.dot(_rms(cq_ref[...], gq_ref[...]).astype(BF16), wq_ref[...],
                preferred_element_type=F32)
    qn_ref[...] = q[:, :GROUP_W].astype(BF16)
    for c in range(0, 2 * LANE, LANE):
        qr_ref[:, c:c + LANE] = _rope(q[:, GROUP_W + c:GROUP_W + c + LANE], cos, sin).astype(BF16)
    kv = jnp.dot(_rms(ckv_ref[...], gkv_ref[...]).astype(BF16), wkv_ref[...],
                 preferred_element_type=F32)
    kn_ref[...] = kv[:, :GROUP_W].astype(BF16)
    v_ref[...] = kv[:, GROUP_W:].astype(BF16)
    kr = _rope(kr_ref[...], cos, sin)
    kr2_ref[...] = (kr + pltpu.roll(kr, 64, 1)).astype(BF16)


def _mla_prep(l, z, gq, gkv, wq, wkv, cos_t, sin_t):
    def out(w):
        return jax.ShapeDtypeStruct((NT, w), BF16)

    def row(w):
        return pl.BlockSpec((TM, w), lambda i: (i, 0))

    rope = pl.BlockSpec((TM, LANE), lambda i: (jnp.where(i * TM < NX, i % (SEQ // TM), SEQ // TM), 0))
    return pl.pallas_call(
        _mla_prep_kernel,
        grid=(NT // TM,),
        in_specs=[
            pl.BlockSpec((TM, 256), lambda i: (i, COL_CKV // 256)),
            pl.BlockSpec((TM, 384), lambda i: (i, COL_CQ // 384)),
            pl.BlockSpec((TM, LANE), lambda i: (i, COL_KR // LANE)),
            _layer_spec(l, (1, 384)),
            _layer_spec(l, (1, 256)),
            _layer_spec(l, (384, 768)),
            _layer_spec(l, (256, 1024)),
            rope, rope,
        ],
        out_specs=[row(GROUP_W), row(2 * LANE), row(GROUP_W), row(GROUP_W), row(LANE)],
        out_shape=[out(GROUP_W), out(2 * LANE), out(GROUP_W), out(GROUP_W), out(LANE)],
        compiler_params=_params(("arbitrary",)),
        name="mla_prep",
    )(z, z, z, gq, gkv, wq, wkv, cos_t, sin_t)


def _mla_chain(h, qn, qr, k, v):
    lane = lax.broadcasted_iota(jnp.int32, qr.shape, 1)
    lo = (h % 2) * 64
    qr = jnp.where((lane >= lo) & (lane < lo + 64), qr, jnp.zeros_like(qr))
    s = _nt_dot(jnp.concatenate([qn, qr], axis=-1), k[...]) * ((128 + 64) ** -0.5)
    o = _softmax_pv(s, v[...])
    return (o[:, :LANE] / o[:, LANE:]).astype(BF16)


def _mla_kernel(qn_ref, qr_ref, knx_ref, knc_ref, krx_ref, krc_ref, vx_ref, vc_ref,
                o_ref, kb_ref, vb_ref):
    h = pl.program_id(1)
    kb_ref[0:SEQ, 0:LANE] = knx_ref[...]
    kb_ref[SEQ:, 0:LANE] = knc_ref[...]
    kb_ref[0:SEQ, LANE:] = krx_ref[...]
    kb_ref[SEQ:, LANE:] = krc_ref[...]
    vb_ref[0:SEQ, 0:LANE] = vx_ref[...]
    vb_ref[SEQ:, 0:LANE] = vc_ref[...]
    vb_ref[:, LANE:] = jnp.ones((SEQ + CTX_LEN, LANE), BF16)
    for r in range(0, SEQ, QSUB):
        rows = slice(r, r + QSUB)
        o_ref[rows, :] = _mla_chain(h, qn_ref[rows, :], qr_ref[rows, :], kb_ref, vb_ref)


def _mla_ctx_kernel(qn_ref, qr_ref, kn_ref, kr_ref, v_ref, y_ref, o_ref):
    del y_ref
    k = jnp.concatenate([kn_ref[...], kr_ref[...]], axis=-1)
    o_ref[...] = _mla_chain(pl.program_id(1), qn_ref[...], qr_ref[...], k, _with_ones(v_ref[...]))


def _mla_attn(qn, qr, kn, v, kr2, with_ctx):
    def xs(col):
        return pl.BlockSpec((SEQ, LANE), lambda b, h: (b, col(h)))

    def cs(col):
        return pl.BlockSpec((CTX_LEN, LANE), lambda b, h: (CBLK + b, col(h)))

    head, pair, first = (lambda h: h), (lambda h: h // 2), (lambda h: 0)
    y = pl.pallas_call(
        _mla_kernel,
        grid=(BATCH, HEADS),
        in_specs=[xs(head), xs(pair), xs(head), cs(head), xs(first), cs(first), xs(head), cs(head)],
        out_specs=xs(head),
        out_shape=jax.ShapeDtypeStruct((NT if with_ctx else NX, GROUP_W), BF16),
        scratch_shapes=[pltpu.VMEM((SEQ + CTX_LEN, 2 * LANE), BF16),
                        pltpu.VMEM((SEQ + CTX_LEN, 2 * LANE), BF16)],
        compiler_params=_params(("arbitrary", "arbitrary")),
        name="mla_attn",
    )(qn, qr, kn, kn, kr2, kr2, v, v)
    if not with_ctx:
        return y
    return _ctx_call(_mla_ctx_kernel, "mla_attn_ctx", [qn, qr, kn, kr2, v],
                     [cs(head), cs(pair), cs(head), cs(first), cs(head)], y)


def _out_kernel(split, ya_ref, yb_ref, ym_ref, yd_ref, w_ref, x_ref, *refs):
    c_ref, (mod_ref, gpost_ref, xo_ref, wb_ref) = (refs[0], refs[1:]) if split else (None, refs)

    @pl.when(pl.program_id(0) == 0)
    def _():
        wb_ref[...] = w_ref[...].astype(BF16)

    y = jnp.dot(ya_ref[...], wb_ref[0:GROUP_W, :], preferred_element_type=F32)
    y += jnp.dot(yb_ref[...], wb_ref[GROUP_W:2 * GROUP_W, :], preferred_element_type=F32)
    y += jnp.dot(ym_ref[...], wb_ref[2 * GROUP_W:3 * GROUP_W, :], preferred_element_type=F32)
    y += jnp.dot(yd_ref[...], wb_ref[3 * GROUP_W:, :], preferred_element_type=F32)
    delta = mod_ref[2:3, :] * _rms(y, gpost_ref[...])

    def add_residual(src):
        xo_ref[...] = src[...] + delta

    _pick_rows(pl.program_id(0), TM, x_ref, c_ref, add_residual)


def _out_proj(l, ya, yb, ym, yd, w, toks, mods, g_post):
    n = ya.shape[0]
    split = len(toks) == 2
    y_spec = pl.BlockSpec((TM, GROUP_W), lambda i: (i, 0))
    row = pl.BlockSpec((TM, D_MODEL), lambda i: (i, 0))
    return pl.pallas_call(
        functools.partial(_out_kernel, split),
        grid=(n // TM,),
        in_specs=[
            y_spec, y_spec, y_spec, y_spec,
            pl.BlockSpec((None, D_MODEL, D_MODEL), lambda i: (l, 0, 0),
                         pipeline_mode=pl.Buffered(1)),
        ] + (_split_rows_specs(TM, NX) if split else [row]) + [
            _mod_spec(l, TM),
            _layer_spec(l, (1, D_MODEL)),
        ],
        out_specs=row,
        out_shape=jax.ShapeDtypeStruct((n, D_MODEL), F32),
        scratch_shapes=[pltpu.VMEM((D_MODEL, D_MODEL), BF16)],
        compiler_params=_params(("arbitrary",)),
        name="out_proj",
    )(ya, yb, ym, yd, w, *toks, mods, g_post)


def _ffn_kernel(x_ref, xp_ref, xn_ref, mod_ref, gpre_ref, gpost_ref, wg_ref, wv_ref, cg_ref, cv_ref,
                wd_ref, xo_ref, hx_ref, r_ref):
    i = pl.program_id(0)
    j = pl.program_id(1)
    n = TM_BIG

    @pl.when(j == 0)
    def _():
        def hmod(x, r):
            return x * r * gpre_ref[...] * (1.0 + mod_ref[4:5, :]) + mod_ref[3:4, :]

        def rinv(x):
            return lax.rsqrt(jnp.mean(x * x, axis=-1, keepdims=True) + NORM_EPS)

        r_ref[...] = rinv(x_ref[...])

        def slab(k, carry):
            rows = pl.ds(pl.multiple_of(k * BF16_ROWS, BF16_ROWS), BF16_ROWS)
            hx_ref[rows, :] = hmod(x_ref[rows, :], r_ref[rows, :]).astype(BF16)
            return carry

        lax.fori_loop(0, n // BF16_ROWS, slab, 0, unroll=4)
        has_prev, has_next = _tile_edges(i, n)
        halo = jnp.concatenate([xp_ref[...], xn_ref[...]], axis=0)
        keep = jnp.where(lax.broadcasted_iota(jnp.int32, (BF16_ROWS, 1), 0) < SUBLANE,
                         has_prev, has_next)
        hx_ref[n:, :] = (hmod(halo, rinv(halo)) * keep).astype(BF16)
        xo_ref[...] = jnp.zeros_like(xo_ref)

    is_ctx = i * n >= NX
    hx = hx_ref[...]

    def conv_up(w_ref, c_ref):
        u = jnp.dot(hx, w_ref[...], preferred_element_type=F32)
        return _conv3_rows(u[:n], u[n + SUBLANE - 1:n + SUBLANE], u[n + SUBLANE:n + SUBLANE + 1],
                           c_ref, is_ctx)

    gate = conv_up(wg_ref, cg_ref)
    val = conv_up(wv_ref, cv_ref)
    act = (gate * jax.nn.sigmoid(gate) * val).astype(BF16)
    xo_ref[...] += jnp.dot(act, wd_ref[...], preferred_element_type=F32)

    @pl.when(j == pl.num_programs(1) - 1)
    def _():
        y = xo_ref[...]
        r_ref[...] = lax.rsqrt(jnp.mean(y * y, axis=-1, keepdims=True) + NORM_EPS)

        def slab(k, carry):
            rows = pl.ds(pl.multiple_of(k * BF16_ROWS, BF16_ROWS), BF16_ROWS)
            normed = xo_ref[rows, :] * r_ref[rows, :] * gpost_ref[...]
            xo_ref[rows, :] = x_ref[rows, :] + mod_ref[5:6, :] * normed
            return carry

        lax.fori_loop(0, n // BF16_ROWS, slab, 0, unroll=4)


def _ffn(l, tok, mods, g_pre, g_post, w_up, conv_w, w_down, n_rows):
    nj = D_FF // TF
    return pl.pallas_call(
        _ffn_kernel,
        grid=(n_rows // TM_BIG, nj),
        in_specs=(_halo_specs(TM_BIG, D_MODEL, n_rows, 0) + [
            _mod_spec(l, TM_BIG),
            _layer_spec(l, (1, D_MODEL)),
            _layer_spec(l, (1, D_MODEL)),
            pl.BlockSpec((None, D_MODEL, TF), lambda i, j: (l, 0, j)),
            pl.BlockSpec((None, D_MODEL, TF), lambda i, j: (l, 0, j + nj)),
            pl.BlockSpec((None, 3, TF), lambda i, j: (l, 0, j)),
            pl.BlockSpec((None, 3, TF), lambda i, j: (l, 0, j + nj)),
            pl.BlockSpec((None, TF, D_MODEL), lambda i, j: (l, j, 0)),
        ]),
        out_specs=pl.BlockSpec((TM_BIG, D_MODEL), lambda i, j: (i, 0)),
        out_shape=jax.ShapeDtypeStruct((n_rows, D_MODEL), F32),
        scratch_shapes=[pltpu.VMEM((TM_BIG + BF16_ROWS, D_MODEL), BF16),
                        pltpu.VMEM((TM_BIG, 1), F32)],
        compiler_params=_params(("arbitrary", "arbitrary"), VMEM_LIMIT_FFN),
        name="ffn",
    )(tok, tok, tok, mods, g_pre, g_post, w_up, w_up, conv_w, conv_w, w_down)


def _rope_tables():
    rows = SEQ // GRID_W
    row = jnp.repeat(jnp.arange(rows), GRID_W).astype(F32)
    col = jnp.tile(jnp.arange(GRID_W), rows).astype(F32)
    n_freq = 64 // 4
    inv = ROPE_THETA ** (-jnp.arange(n_freq, dtype=F32) / n_freq)
    ang = jnp.concatenate([row[:, None] * inv, col[:, None] * inv], axis=-1)
    cos, sin = jnp.cos(ang), jnp.sin(ang)
    cos_t = jnp.concatenate([cos, cos, cos, cos], axis=-1)
    sin_t = jnp.concatenate([-sin, sin, -sin, sin], axis=-1)
    cos_t = jnp.concatenate([cos_t, jnp.ones((TM, LANE), F32)], axis=0)
    sin_t = jnp.concatenate([sin_t, jnp.zeros((TM, LANE), F32)], axis=0)
    return cos_t, sin_t


def _layout_w_in(w):
    a, qkv, cq, ckv, kr, d = jnp.split(w, [512, 2048, 2432, 2688, 2752], axis=-1)
    pad = jnp.zeros(w.shape[:-1] + (IN_COLS_PAD - COL_KR - 64,), w.dtype)
    return jnp.concatenate([a, qkv, d, ckv, cq, kr, pad], axis=-1).astype(BF16)


def _layout_w_uq(w):
    w = w.reshape(DEPTH, 384, HEADS, 192)
    return jnp.concatenate([w[..., :128].reshape(DEPTH, 384, 512),
                            w[..., 128:].reshape(DEPTH, 384, 256)], axis=-1).astype(BF16)


def _layout_w_ukv(w):
    w = w.reshape(DEPTH, 256, HEADS, 256)
    return jnp.concatenate([w[..., :128].reshape(DEPTH, 256, 512),
                            w[..., 128:].reshape(DEPTH, 256, 512)], axis=-1).astype(BF16)


def kernel(x, c, ctx, c_ctx, ada_w, ada_b, g_pre_mix, g_post_mix, g_pre_ffn, g_post_ffn, w_in,
           pool_w, pool_scale, diff_lambda, diff_subln_g, mla_gq, mla_w_uq, mla_gkv, mla_w_ukv,
           conv_w, w_out, ffn_w_up, ffn_conv_w, ffn_w_down):
    toks = (x.reshape(NX, D_MODEL), ctx.reshape(NC, D_MODEL))
    cc = jnp.concatenate([c, c_ctx[None, :], jnp.zeros((3, D_MODEL), F32)], axis=0)
    mods = _ada(cc, ada_w, ada_b).reshape(DEPTH, 8, 6, D_MODEL)
    cos_t, sin_t = _rope_tables()

    def vec(p):
        return p.reshape(DEPTH, 1, p.shape[-1])

    w_in_b = _layout_w_in(w_in)
    w_uq_b, w_ukv_b = _layout_w_uq(mla_w_uq), _layout_w_ukv(mla_w_ukv)
    pool_w_b = pool_w.astype(BF16)
    w_up_b, w_down_b = ffn_w_up.astype(BF16), ffn_w_down.astype(BF16)

    for l in range(DEPTH):
        last = l == DEPTH - 1
        n_rows = NX if last else NT
        lam_init = 0.8 - 0.6 * math.exp(-0.3 * l)

        z = _in_proj(l, toks, mods, vec(g_pre_mix), w_in_b)
        ya, yd = _pool_conv(l, z, n_rows, pool_w_b, vec(pool_scale), conv_w)
        yb = _diff_attn(l, z, cos_t, sin_t, diff_lambda, vec(diff_subln_g), lam_init, not last)
        qn, qr, kn, v, kr2 = _mla_prep(l, z, vec(mla_gq), vec(mla_gkv), w_uq_b, w_ukv_b, cos_t, sin_t)
        ym = _mla_attn(qn, qr, kn, v, kr2, not last)
        tok = _out_proj(l, ya, yb, ym, yd, w_out, toks, mods, vec(g_post_mix))
        tok = _ffn(l, tok, mods, vec(g_pre_ffn), vec(g_post_ffn), w_up_b, ffn_conv_w, w_down_b, n_rows)
        toks = (tok,)
    return tok.reshape(BATCH, SEQ, D_MODEL)
```

```python
import functools
import math

import jax
import jax.numpy as jnp
from jax import lax
from jax.experimental import pallas as pl
from jax.experimental.pallas import tpu as pltpu

F32 = jnp.float32
BF16 = jnp.bfloat16

D_MODEL = 2048
BATCH = 4
SEQ = 2048
CTX_LEN = 256
GRID_W = 64
DEPTH = 2
NORM_EPS = 1e-6
ROPE_THETA = 10000.0
GROUP_W = 512
POOL_WINDOWS = (2, 4, 8, 16)
HEADS = 4
D_FF = 5632

NX = BATCH * SEQ
NC = BATCH * CTX_LEN
NT = NX + NC

COL_A, COL_Q, COL_K, COL_V = 0, 512, 1024, 1536
COL_DB, COL_DC, COL_DH = 2048, 2560, 3072
COL_CKV, COL_CQ, COL_KR = 3584, 3840, 4224
IN_COLS_PAD = 4608

LANE = 128
SUBLANE = 8
BF16_ROWS = 2 * SUBLANE
TM = 512
TM_BIG = 1024
TS = 256
QSUB = 256
TN_IN = IN_COLS_PAD // 3
TF = 512
TN_ADA = 1024
VMEM_LIMIT = 56 * 1024 * 1024
VMEM_LIMIT_BIG = 62 * 1024 * 1024


def _params(sem, vmem_limit=VMEM_LIMIT):
    return pltpu.CompilerParams(dimension_semantics=sem, vmem_limit_bytes=vmem_limit)


def _split_rows_specs(rows, n_lat_rows):
    n_lat = n_lat_rows // rows
    return [pl.BlockSpec((rows, D_MODEL), lambda i, *_: (jnp.minimum(i, n_lat - 1), 0)),
            pl.BlockSpec((rows, D_MODEL), lambda i, *_: (jnp.maximum(i - n_lat, 0), 0),
                         pipeline_mode=pl.Buffered(1))]


def _pick_rows(i, rows, x_ref, c_ref, fn):
    if c_ref is None:
        fn(x_ref)
        return

    @pl.when(i * rows < NX)
    def _():
        fn(x_ref)

    @pl.when(i * rows >= NX)
    def _():
        fn(c_ref)


def _rms(x, g):
    ms = jnp.mean(x * x, axis=-1, keepdims=True)
    return x * lax.rsqrt(ms + NORM_EPS) * g


def _rope(v, cos, sin):
    lane = lax.broadcasted_iota(jnp.int32, v.shape, 1)
    sw = jnp.where((lane & 32) == 0, pltpu.roll(v, 96, 1), pltpu.roll(v, 32, 1))
    return v * cos + sw * sin


def _seq_pos(tile, rows):
    r = tile * rows + lax.broadcasted_iota(jnp.int32, (rows, 1), 0)
    seqlen = jnp.where(r < NX, SEQ, CTX_LEN)
    return r & (seqlen - 1), seqlen


def _tile_edges(tile, rows):
    r0 = tile * rows
    seqlen = jnp.where(r0 < NX, SEQ, CTX_LEN)
    has_prev = jnp.where((r0 & (seqlen - 1)) == 0, 0.0, 1.0)
    has_next = jnp.where(((r0 + rows) & (seqlen - 1)) == 0, 0.0, 1.0)
    return has_prev, has_next


def _conv3_rows(u, prev_row, next_row, c_ref, is_ctx):
    n = u.shape[0]
    row8 = lax.broadcasted_iota(jnp.int32, (SUBLANE, 1), 0)
    prev = pltpu.roll(u, 1, 0)
    prev = jnp.concatenate([jnp.where(row8 == 0, prev_row, prev[:SUBLANE]), prev[SUBLANE:]], axis=0)
    nxt = pltpu.roll(u, n - 1, 0)
    nxt = jnp.concatenate([nxt[:n - SUBLANE],
                           jnp.where(row8 == SUBLANE - 1, next_row, nxt[n - SUBLANE:])], axis=0)
    ctx_row = row8 + jnp.where(is_ctx, 0, 2 * SUBLANE)
    for p in range(CTX_LEN, n, CTX_LEN):
        prev = jnp.concatenate([prev[:p], jnp.where(ctx_row == 0, 0.0, prev[p:p + SUBLANE]),
                                prev[p + SUBLANE:]], axis=0)
        nxt = jnp.concatenate([nxt[:p - SUBLANE],
                               jnp.where(ctx_row == SUBLANE - 1, 0.0, nxt[p - SUBLANE:p]), nxt[p:]],
                              axis=0)
    return c_ref[0:1, :] * prev + c_ref[1:2, :] * u + c_ref[2:3, :] * nxt


def _first_axis(i, *_):
    return i


def _halo_specs(rows, width, n_rows, col, tile=_first_axis):
    r8 = rows // SUBLANE
    last8 = n_rows // SUBLANE - 1
    return [
        pl.BlockSpec((rows, width), lambda *g: (tile(*g), col)),
        pl.BlockSpec((SUBLANE, width), lambda *g: (jnp.maximum(tile(*g) * r8 - 1, 0), col)),
        pl.BlockSpec((SUBLANE, width), lambda *g: (jnp.minimum((tile(*g) + 1) * r8, last8), col)),
    ]


def _layer_spec(l, shape):
    return pl.BlockSpec((None,) + shape, lambda *_: (l,) + (0,) * len(shape))


def _mod_spec(l, rows, tile=_first_axis):
    def index(*g):
        r = tile(*g) * rows
        return (l, jnp.where(r < NX, r // SEQ, BATCH), 0, 0)
    return pl.BlockSpec((None, None, 6, D_MODEL), index)


def _ada_kernel(c_ref, w_ref, b_ref, o_ref):
    c = c_ref[...]
    a = c * jax.nn.sigmoid(c)
    o_ref[...] = jnp.dot(a.astype(BF16), w_ref[...].astype(BF16),
                         preferred_element_type=F32) + b_ref[...]


def _ada(cc, ada_w, ada_b):
    n = ada_w.shape[-1]
    return pl.pallas_call(
        _ada_kernel,
        grid=(DEPTH, n // TN_ADA),
        in_specs=[
            pl.BlockSpec((8, D_MODEL), lambda l, j: (0, 0)),
            pl.BlockSpec((None, D_MODEL, TN_ADA), lambda l, j: (l, 0, j)),
            pl.BlockSpec((None, 1, TN_ADA), lambda l, j: (l, 0, j)),
        ],
        out_specs=pl.BlockSpec((None, 8, TN_ADA), lambda l, j: (l, 0, j)),
        out_shape=jax.ShapeDtypeStruct((DEPTH, 8, n), F32),
        compiler_params=_params(("arbitrary", "arbitrary")),
        name="ada",
    )(cc, ada_w, ada_b.reshape(DEPTH, 1, n))


def _in_kernel(split, x_ref, *refs):
    c_ref, (mod_ref, g_ref, w_ref, z_ref, h_ref, r_ref) = ((refs[0], refs[1:]) if split
                                                              else (None, refs))

    @pl.when(pl.program_id(1) == 0)
    def _():
        def put(src):
            x = src[...]
            r_ref[...] = lax.rsqrt(jnp.mean(x * x, axis=-1, keepdims=True) + NORM_EPS)

            shape = (BF16_ROWS, D_MODEL)
            g = jnp.broadcast_to(g_ref[...], shape)
            scale = jnp.broadcast_to(1.0 + mod_ref[1:2, :], shape)
            shift = jnp.broadcast_to(mod_ref[0:1, :], shape)

            def slab(k, carry):
                rows = pl.ds(pl.multiple_of(k * BF16_ROWS, BF16_ROWS), BF16_ROWS)
                h_ref[rows, :] = (src[rows, :] * r_ref[rows, :] * g * scale + shift).astype(BF16)
                return carry

            lax.fori_loop(0, TM_BIG // BF16_ROWS, slab, 0, unroll=4)

        _pick_rows(pl.program_id(0), TM_BIG, x_ref, c_ref, put)

    z_ref[...] = jnp.dot(h_ref[...], w_ref[...], preferred_element_type=F32)


def _in_proj(l, toks, mods, g, w):
    split = len(toks) == 2
    rows = (_split_rows_specs(TM_BIG, NX) if split
            else [pl.BlockSpec((TM_BIG, D_MODEL), lambda i, j: (i, 0))])
    return pl.pallas_call(
        functools.partial(_in_kernel, split),
        grid=(NT // TM_BIG, IN_COLS_PAD // TN_IN),
        in_specs=rows + [
            _mod_spec(l, TM_BIG),
            _layer_spec(l, (1, D_MODEL)),
            pl.BlockSpec((None, D_MODEL, TN_IN), lambda i, j: (l, 0, j)),
        ],
        out_specs=pl.BlockSpec((TM_BIG, TN_IN), lambda i, j: (i, j)),
        out_shape=jax.ShapeDtypeStruct((NT, IN_COLS_PAD), F32),
        scratch_shapes=[pltpu.VMEM((TM_BIG, D_MODEL), BF16), pltpu.VMEM((TM_BIG, 1), F32)],
        compiler_params=_params(("arbitrary", "arbitrary"), VMEM_LIMIT_BIG),
        name="in_proj",
    )(*toks, mods, g, w)


def _pool_conv_kernel(a_ref, ap_ref, an_ref, db_ref, dc_ref, dcp_ref, dcn_ref,
                      dh_ref, dhp_ref, dhn_ref, pw_ref, ps_ref, cw_ref,
                      ya_ref, yd_ref, e_ref):
    i = pl.program_id(0)
    pos, seqlen = _seq_pos(i, TS)
    pf, nf = _tile_edges(i, TS)

    e_ref[0:8, :] = ap_ref[...] * pf
    e_ref[8:8 + TS, :] = a_ref[...]
    e_ref[8 + TS:16 + TS, :] = an_ref[...] * nf
    for g, w in enumerate(POOL_WINDOWS):
        half = w // 2
        cs = slice(g * LANE, (g + 1) * LANE)
        acc = e_ref[8 - half:8 - half + TS, cs]
        for k in range(1, w):
            acc = acc + e_ref[8 - half + k:8 - half + k + TS, cs]
        cnt = (jnp.minimum(pos + half, seqlen) - jnp.maximum(pos - half, 0)).astype(F32)
        pooled = acc / cnt - a_ref[:, cs]
        y = jnp.dot(pooled.astype(BF16), pw_ref[g], preferred_element_type=F32)
        ya_ref[:, cs] = (y * ps_ref[:, cs]).astype(BF16)

    p_prev = dcp_ref[SUBLANE - 1:SUBLANE, :] * dhp_ref[SUBLANE - 1:SUBLANE, :] * pf
    p_next = dcn_ref[0:1, :] * dhn_ref[0:1, :] * nf
    conv = _conv3_rows(dc_ref[...] * dh_ref[...], p_prev, p_next, cw_ref, False)
    yd_ref[...] = (db_ref[...] * conv).astype(BF16)


def _pool_conv(l, z, n_rows, pool_w, pool_scale, conv_w):
    ca, cb, cc, ch = (COL_A // GROUP_W, COL_DB // GROUP_W, COL_DC // GROUP_W, COL_DH // GROUP_W)
    out = jax.ShapeDtypeStruct((n_rows, GROUP_W), BF16)
    return pl.pallas_call(
        _pool_conv_kernel,
        grid=(n_rows // TS,),
        in_specs=(_halo_specs(TS, GROUP_W, NT, ca)
                  + [pl.BlockSpec((TS, GROUP_W), lambda i: (i, cb))]
                  + _halo_specs(TS, GROUP_W, NT, cc) + _halo_specs(TS, GROUP_W, NT, ch)
                  + [_layer_spec(l, (len(POOL_WINDOWS), LANE, LANE)),
                     _layer_spec(l, (1, GROUP_W)), _layer_spec(l, (3, GROUP_W))]),
        out_specs=[pl.BlockSpec((TS, GROUP_W), lambda i: (i, 0))] * 2,
        out_shape=[out, out],
        scratch_shapes=[pltpu.VMEM((TS + 16, GROUP_W), F32)],
        compiler_params=_params(("arbitrary",)),
        name="pool_conv",
    )(z, z, z, z, z, z, z, z, z, z, pool_w, pool_scale, conv_w)


CBLK = NX // CTX_LEN


def _nt_dot(a, b):
    return lax.dot_general(a, b, (((1,), (1,)), ((), ())), preferred_element_type=F32)


def _softmax_pv(s, v):
    e = jnp.exp(s - jnp.max(s, axis=-1, keepdims=True)).astype(BF16)
    return jnp.dot(e, v, preferred_element_type=F32)


def _with_ones(v):
    return jnp.concatenate([v, jnp.ones(v.shape, v.dtype)], axis=-1)


def _attn_calls(lat_fn, ctx_fn, name, lat_ins, lat_specs, ctx_ins, ctx_specs, scratch, with_ctx):
    def call(fn, call_name, ins, specs, rows, block, scratch_shapes, y):
        aliased = [] if y is None else [y]
        return pl.pallas_call(
            (lambda *refs: fn(*refs[:len(ins)], *refs[len(ins) + len(aliased):])),
            grid=(BATCH, HEADS),
            in_specs=specs + [pl.BlockSpec(memory_space=pl.ANY)] * len(aliased),
            out_specs=pl.BlockSpec((rows, LANE), block),
            out_shape=jax.ShapeDtypeStruct((NT if with_ctx else NX, GROUP_W), BF16),
            input_output_aliases={len(ins): 0} if aliased else {},
            scratch_shapes=scratch_shapes,
            compiler_params=_params(("arbitrary", "arbitrary")),
            name=call_name,
        )(*ins, *aliased)

    y0 = jnp.zeros((NT, GROUP_W), BF16) if with_ctx else None
    y = call(lat_fn, name, lat_ins, lat_specs, SEQ, lambda b, h: (b, h), scratch, y0)
    if not with_ctx:
        return y
    return call(ctx_fn, name + "_ctx", ctx_ins, ctx_specs, CTX_LEN, lambda b, h: (CBLK + b, h), [], y)


def _diff_lambda(lam_ref, lam_init):
    lp = lam_ref[...]
    return (jnp.exp(jnp.sum(lp[0:1] * lp[1:2], keepdims=True))
            - jnp.exp(jnp.sum(lp[2:3] * lp[3:4], keepdims=True)) + lam_init)


def _diff_chain(q, k, v, lam, g, lam_init):
    q = q * (64 ** -0.5)
    lane = lax.broadcasted_iota(jnp.int32, q.shape, 1)
    o1 = _softmax_pv(_nt_dot(jnp.where(lane < 64, q, 0.0).astype(BF16), k[...]), v[...])
    o2 = _softmax_pv(_nt_dot(jnp.where(lane >= 64, q, 0.0).astype(BF16), k[...]), v[...])
    o = o1[:, :LANE] / o1[:, LANE:] - o2[:, :LANE] * (lam / o2[:, LANE:])
    return (_rms(o, g) * (1.0 - lam_init)).astype(BF16)


def _diff_kernel(lam_init, q_ref, kx_ref, kc_ref, vx_ref, vc_ref, cos_ref, sin_ref, lam_ref, g_ref,
                 o_ref, kb_ref, vb_ref):
    kb_ref[0:SEQ, :] = _rope(kx_ref[...], cos_ref[...], sin_ref[...]).astype(BF16)
    kb_ref[SEQ:, :] = kc_ref[...].astype(BF16)
    vb_ref[0:SEQ, 0:LANE] = vx_ref[...].astype(BF16)
    vb_ref[SEQ:, 0:LANE] = vc_ref[...].astype(BF16)
    vb_ref[:, LANE:] = jnp.ones((SEQ + CTX_LEN, LANE), BF16)
    lam = _diff_lambda(lam_ref, lam_init)
    for r in range(0, SEQ, QSUB):
        rows = slice(r, r + QSUB)
        q = _rope(q_ref[rows, :], cos_ref[rows, :], sin_ref[rows, :])
        o_ref[rows, :] = _diff_chain(q, kb_ref, vb_ref, lam, g_ref[...], lam_init)


def _diff_ctx_kernel(lam_init, q_ref, k_ref, v_ref, lam_ref, g_ref, o_ref):
    o_ref[...] = _diff_chain(q_ref[...], k_ref[...].astype(BF16),
                             _with_ones(v_ref[...].astype(BF16)),
                             _diff_lambda(lam_ref, lam_init), g_ref[...], lam_init)


def _diff_attn(l, z, cos_t, sin_t, lam_p, subln_g, lam_init, with_ctx):
    cq, ck, cv = COL_Q // LANE, COL_K // LANE, COL_V // LANE

    def xs(col):
        return pl.BlockSpec((SEQ, LANE), lambda b, h: (b, col + h))

    def cs(col):
        return pl.BlockSpec((CTX_LEN, LANE), lambda b, h: (CBLK + b, col + h))

    table = pl.BlockSpec((SEQ, LANE), lambda b, h: (0, 0))
    params = [_layer_spec(l, (4, 64)), _layer_spec(l, (1, LANE))]
    return _attn_calls(
        functools.partial(_diff_kernel, lam_init), functools.partial(_diff_ctx_kernel, lam_init),
        "diff_attn",
        [z, z, z, z, z, cos_t, sin_t, lam_p, subln_g],
        [xs(cq), xs(ck), cs(ck), xs(cv), cs(cv), table, table] + params,
        [z, z, z, lam_p, subln_g],
        [cs(cq), cs(ck), cs(cv)] + params,
        [pltpu.VMEM((SEQ + CTX_LEN, LANE), BF16), pltpu.VMEM((SEQ + CTX_LEN, 2 * LANE), BF16)],
        with_ctx)


def _mla_prep_kernel(ckv_ref, cq_ref, kr_ref, gq_ref, gkv_ref, wq_ref, wkv_ref, cos_ref, sin_ref,
                     qn_ref, qr_ref, kn_ref, v_ref, kr2_ref):
    cos = cos_ref[...]
    sin = sin_ref[...]
    q = jnp.dot(_rms(cq_ref[...], gq_ref[...]).astype(BF16), wq_ref[...],
                preferred_element_type=F32)
    qn_ref[...] = q[:, :GROUP_W].astype(BF16)
    for c in range(0, 2 * LANE, LANE):
        qr_ref[:, c:c + LANE] = _rope(q[:, GROUP_W + c:GROUP_W + c + LANE], cos, sin).astype(BF16)
    kv = jnp.dot(_rms(ckv_ref[...], gkv_ref[...]).astype(BF16), wkv_ref[...],
                 preferred_element_type=F32)
    kn_ref[...] = kv[:, :GROUP_W].astype(BF16)
    v_ref[...] = kv[:, GROUP_W:].astype(BF16)
    kr = _rope(kr_ref[...], cos, sin)
    kr2_ref[...] = (kr + pltpu.roll(kr, 64, 1)).astype(BF16)


def _mla_prep(l, z, gq, gkv, wq, wkv, cos_t, sin_t):
    def out(w):
        return jax.ShapeDtypeStruct((NT, w), BF16)

    def row(w):
        return pl.BlockSpec((TM, w), lambda i: (i, 0))

    rope = pl.BlockSpec((TM, LANE), lambda i: (jnp.where(i * TM < NX, i % (SEQ // TM), SEQ // TM), 0))
    return pl.pallas_call(
        _mla_prep_kernel,
        grid=(NT // TM,),
        in_specs=[
            pl.BlockSpec((TM, 256), lambda i: (i, COL_CKV // 256)),
            pl.BlockSpec((TM, 384), lambda i: (i, COL_CQ // 384)),
            pl.BlockSpec((TM, LANE), lambda i: (i, COL_KR // LANE)),
            _layer_spec(l, (1, 384)),
            _layer_spec(l, (1, 256)),
            _layer_spec(l, (384, 768)),
            _layer_spec(l, (256, 1024)),
            rope, rope,
        ],
        out_specs=[row(GROUP_W), row(2 * LANE), row(GROUP_W), row(GROUP_W), row(LANE)],
        out_shape=[out(GROUP_W), out(2 * LANE), out(GROUP_W), out(GROUP_W), out(LANE)],
        compiler_params=_params(("arbitrary",)),
        name="mla_prep",
    )(z, z, z, gq, gkv, wq, wkv, cos_t, sin_t)


def _mla_chain(h, qn, qr, k, v):
    lane = lax.broadcasted_iota(jnp.int32, qr.shape, 1)
    lo = (h % 2) * 64
    qr = jnp.where((lane >= lo) & (lane < lo + 64), qr, jnp.zeros_like(qr))
    s = _nt_dot(jnp.concatenate([qn, qr], axis=-1), k[...]) * ((128 + 64) ** -0.5)
    o = _softmax_pv(s, v[...])
    return (o[:, :LANE] / o[:, LANE:]).astype(BF16)


def _mla_kernel(qn_ref, qr_ref, knx_ref, knc_ref, krx_ref, krc_ref, vx_ref, vc_ref,
                o_ref, kb_ref, vb_ref):
    h = pl.program_id(1)
    kb_ref[0:SEQ, 0:LANE] = knx_ref[...]
    kb_ref[SEQ:, 0:LANE] = knc_ref[...]
    kb_ref[0:SEQ, LANE:] = krx_ref[...]
    kb_ref[SEQ:, LANE:] = krc_ref[...]
    vb_ref[0:SEQ, 0:LANE] = vx_ref[...]
    vb_ref[SEQ:, 0:LANE] = vc_ref[...]
    vb_ref[:, LANE:] = jnp.ones((SEQ + CTX_LEN, LANE), BF16)
    for r in range(0, SEQ, QSUB):
        rows = slice(r, r + QSUB)
        o_ref[rows, :] = _mla_chain(h, qn_ref[rows, :], qr_ref[rows, :], kb_ref, vb_ref)


def _mla_ctx_kernel(qn_ref, qr_ref, kn_ref, kr_ref, v_ref, o_ref):
    k = jnp.concatenate([kn_ref[...], kr_ref[...]], axis=-1)
    o_ref[...] = _mla_chain(pl.program_id(1), qn_ref[...], qr_ref[...], k, _with_ones(v_ref[...]))


def _mla_attn(qn, qr, kn, v, kr2, with_ctx):
    def xs(col):
        return pl.BlockSpec((SEQ, LANE), lambda b, h: (b, col(h)))

    def cs(col):
        return pl.BlockSpec((CTX_LEN, LANE), lambda b, h: (CBLK + b, col(h)))

    head, pair, first = (lambda h: h), (lambda h: h // 2), (lambda h: 0)
    return _attn_calls(
        _mla_kernel, _mla_ctx_kernel, "mla_attn",
        [qn, qr, kn, kn, kr2, kr2, v, v],
        [xs(head), xs(pair), xs(head), cs(head), xs(first), cs(first), xs(head), cs(head)],
        [qn, qr, kn, kr2, v],
        [cs(head), cs(pair), cs(head), cs(first), cs(head)],
        [pltpu.VMEM((SEQ + CTX_LEN, 2 * LANE), BF16), pltpu.VMEM((SEQ + CTX_LEN, 2 * LANE), BF16)],
        with_ctx)


def _out_kernel(split, ya_ref, yb_ref, ym_ref, yd_ref, w_ref, x_ref, *refs):
    c_ref, (mod_ref, gpost_ref, xo_ref, wb_ref) = (refs[0], refs[1:]) if split else (None, refs)

    @pl.when(pl.program_id(0) == 0)
    def _():
        wb_ref[...] = w_ref[...].astype(BF16)

    y = jnp.dot(ya_ref[...], wb_ref[0:GROUP_W, :], preferred_element_type=F32)
    y += jnp.dot(yb_ref[...], wb_ref[GROUP_W:2 * GROUP_W, :], preferred_element_type=F32)
    y += jnp.dot(ym_ref[...], wb_ref[2 * GROUP_W:3 * GROUP_W, :], preferred_element_type=F32)
    y += jnp.dot(yd_ref[...], wb_ref[3 * GROUP_W:, :], preferred_element_type=F32)
    delta = mod_ref[2:3, :] * _rms(y, gpost_ref[...])

    def add_residual(src):
        xo_ref[...] = src[...] + delta

    _pick_rows(pl.program_id(0), TM, x_ref, c_ref, add_residual)


def _out_proj(l, ya, yb, ym, yd, w, toks, mods, g_post):
    n = ya.shape[0]
    split = len(toks) == 2
    y_spec = pl.BlockSpec((TM, GROUP_W), lambda i: (i, 0))
    row = pl.BlockSpec((TM, D_MODEL), lambda i: (i, 0))
    return pl.pallas_call(
        functools.partial(_out_kernel, split),
        grid=(n // TM,),
        in_specs=[
            y_spec, y_spec, y_spec, y_spec,
            pl.BlockSpec((None, D_MODEL, D_MODEL), lambda i: (l, 0, 0),
                         pipeline_mode=pl.Buffered(1)),
        ] + (_split_rows_specs(TM, NX) if split else [row]) + [
            _mod_spec(l, TM),
            _layer_spec(l, (1, D_MODEL)),
        ],
        out_specs=row,
        out_shape=jax.ShapeDtypeStruct((n, D_MODEL), F32),
        scratch_shapes=[pltpu.VMEM((D_MODEL, D_MODEL), BF16)],
        compiler_params=_params(("arbitrary",)),
        name="out_proj",
    )(ya, yb, ym, yd, w, *toks, mods, g_post)


def _ffn_kernel(x_ref, xp_ref, xn_ref, mod_ref, gpre_ref, gpost_ref, wg_ref, wv_ref, cg_ref, cv_ref,
                wd_ref, xo_ref, hx_ref, r_ref):
    i = pl.program_id(0)
    j = pl.program_id(1)
    n = TM_BIG

    @pl.when(j == 0)
    def _():
        def hmod(x, r):
            return x * r * gpre_ref[...] * (1.0 + mod_ref[4:5, :]) + mod_ref[3:4, :]

        def rinv(x):
            return lax.rsqrt(jnp.mean(x * x, axis=-1, keepdims=True) + NORM_EPS)

        r_ref[...] = rinv(x_ref[...])

        def slab(k, carry):
            rows = pl.ds(pl.multiple_of(k * BF16_ROWS, BF16_ROWS), BF16_ROWS)
            hx_ref[rows, :] = hmod(x_ref[rows, :], r_ref[rows, :]).astype(BF16)
            return carry

        lax.fori_loop(0, n // BF16_ROWS, slab, 0, unroll=4)
        has_prev, has_next = _tile_edges(i, n)
        halo = jnp.concatenate([xp_ref[...], xn_ref[...]], axis=0)
        keep = jnp.where(lax.broadcasted_iota(jnp.int32, (BF16_ROWS, 1), 0) < SUBLANE,
                         has_prev, has_next)
        hx_ref[n:, :] = (hmod(halo, rinv(halo)) * keep).astype(BF16)
        xo_ref[...] = jnp.zeros_like(xo_ref)

    is_ctx = i * n >= NX
    hx = hx_ref[...]

    def conv_up(w_ref, c_ref):
        u = jnp.dot(hx, w_ref[...], preferred_element_type=F32)
        return _conv3_rows(u[:n], u[n + SUBLANE - 1:n + SUBLANE], u[n + SUBLANE:n + SUBLANE + 1],
                           c_ref, is_ctx)

    gate = conv_up(wg_ref, cg_ref)
    val = conv_up(wv_ref, cv_ref)
    act = (gate * jax.nn.sigmoid(gate) * val).astype(BF16)
    xo_ref[...] += jnp.dot(act, wd_ref[...], preferred_element_type=F32)

    @pl.when(j == pl.num_programs(1) - 1)
    def _():
        y = xo_ref[...]
        r_ref[...] = lax.rsqrt(jnp.mean(y * y, axis=-1, keepdims=True) + NORM_EPS)

        def slab(k, carry):
            rows = pl.ds(pl.multiple_of(k * BF16_ROWS, BF16_ROWS), BF16_ROWS)
            normed = xo_ref[rows, :] * r_ref[rows, :] * gpost_ref[...]
            xo_ref[rows, :] = x_ref[rows, :] + mod_ref[5:6, :] * normed
            return carry

        lax.fori_loop(0, n // BF16_ROWS, slab, 0, unroll=4)


def _ffn(l, tok, mods, g_pre, g_post, w_up, conv_w, w_down, n_rows):
    nj = D_FF // TF
    return pl.pallas_call(
        _ffn_kernel,
        grid=(n_rows // TM_BIG, nj),
        in_specs=(_halo_specs(TM_BIG, D_MODEL, n_rows, 0) + [
            _mod_spec(l, TM_BIG),
            _layer_spec(l, (1, D_MODEL)),
            _layer_spec(l, (1, D_MODEL)),
            pl.BlockSpec((None, D_MODEL, TF), lambda i, j: (l, 0, j)),
            pl.BlockSpec((None, D_MODEL, TF), lambda i, j: (l, 0, j + nj)),
            pl.BlockSpec((None, 3, TF), lambda i, j: (l, 0, j)),
            pl.BlockSpec((None, 3, TF), lambda i, j: (l, 0, j + nj)),
            pl.BlockSpec((None, TF, D_MODEL), lambda i, j: (l, j, 0)),
        ]),
        out_specs=pl.BlockSpec((TM_BIG, D_MODEL), lambda i, j: (i, 0)),
        out_shape=jax.ShapeDtypeStruct((n_rows, D_MODEL), F32),
        scratch_shapes=[pltpu.VMEM((TM_BIG + BF16_ROWS, D_MODEL), BF16),
                        pltpu.VMEM((TM_BIG, 1), F32)],
        compiler_params=_params(("arbitrary", "arbitrary"), VMEM_LIMIT_BIG),
        name="ffn",
    )(tok, tok, tok, mods, g_pre, g_post, w_up, w_up, conv_w, conv_w, w_down)


def _rope_tables():
    rows = SEQ // GRID_W
    row = jnp.repeat(jnp.arange(rows), GRID_W).astype(F32)
    col = jnp.tile(jnp.arange(GRID_W), rows).astype(F32)
    n_freq = 64 // 4
    inv = ROPE_THETA ** (-jnp.arange(n_freq, dtype=F32) / n_freq)
    ang = jnp.concatenate([row[:, None] * inv, col[:, None] * inv], axis=-1)
    cos, sin = jnp.cos(ang), jnp.sin(ang)
    cos_t = jnp.concatenate([cos, cos, cos, cos], axis=-1)
    sin_t = jnp.concatenate([-sin, sin, -sin, sin], axis=-1)
    cos_t = jnp.concatenate([cos_t, jnp.ones((TM, LANE), F32)], axis=0)
    sin_t = jnp.concatenate([sin_t, jnp.zeros((TM, LANE), F32)], axis=0)
    return cos_t, sin_t


W_IN_ROWS = 256


def _w_in_kernel(w_ref, o_ref):
    w = w_ref[...]
    n_pad = IN_COLS_PAD - COL_KR - 64
    o_ref[...] = jnp.concatenate(
        [w[:, :2048], w[:, 2752:], w[:, 2432:2688], w[:, 2048:2432], w[:, 2688:2752],
         jnp.zeros((w.shape[0], n_pad), w.dtype)], axis=-1).astype(BF16)


def _layout_w_in(w):
    k, n = w.shape[1:]
    return pl.pallas_call(
        _w_in_kernel,
        grid=(DEPTH, k // W_IN_ROWS),
        in_specs=[pl.BlockSpec((None, W_IN_ROWS, n), lambda l, i: (l, i, 0))],
        out_specs=pl.BlockSpec((None, W_IN_ROWS, IN_COLS_PAD), lambda l, i: (l, i, 0)),
        out_shape=jax.ShapeDtypeStruct((DEPTH, k, IN_COLS_PAD), BF16),
        compiler_params=_params(("arbitrary", "arbitrary")),
        name="w_in_layout",
    )(w)


def _layout_w_uq(w):
    w = w.reshape(DEPTH, 384, HEADS, 192)
    return jnp.concatenate([w[..., :128].reshape(DEPTH, 384, 512),
                            w[..., 128:].reshape(DEPTH, 384, 256)], axis=-1).astype(BF16)


def _layout_w_ukv(w):
    w = w.reshape(DEPTH, 256, HEADS, 256)
    return jnp.concatenate([w[..., :128].reshape(DEPTH, 256, 512),
                            w[..., 128:].reshape(DEPTH, 256, 512)], axis=-1).astype(BF16)


def kernel(x, c, ctx, c_ctx, ada_w, ada_b, g_pre_mix, g_post_mix, g_pre_ffn, g_post_ffn, w_in,
           pool_w, pool_scale, diff_lambda, diff_subln_g, mla_gq, mla_w_uq, mla_gkv, mla_w_ukv,
           conv_w, w_out, ffn_w_up, ffn_conv_w, ffn_w_down):
    toks = (x.reshape(NX, D_MODEL), ctx.reshape(NC, D_MODEL))
    cc = jnp.concatenate([c, c_ctx[None, :], jnp.zeros((3, D_MODEL), F32)], axis=0)
    mods = _ada(cc, ada_w, ada_b).reshape(DEPTH, 8, 6, D_MODEL)
    cos_t, sin_t = _rope_tables()

    def vec(p):
        return p.reshape(DEPTH, 1, p.shape[-1])

    w_in_b = _layout_w_in(w_in)
    w_uq_b, w_ukv_b = _layout_w_uq(mla_w_uq), _layout_w_ukv(mla_w_ukv)
    pool_w_b = pool_w.astype(BF16)
    w_up_b, w_down_b = ffn_w_up.astype(BF16), ffn_w_down.astype(BF16)

    for l in range(DEPTH):
        last = l == DEPTH - 1
        n_rows = NX if last else NT
        lam_init = 0.8 - 0.6 * math.exp(-0.3 * l)

        z = _in_proj(l, toks, mods, vec(g_pre_mix), w_in_b)
        ya, yd = _pool_conv(l, z, n_rows, pool_w_b, vec(pool_scale), conv_w)
        yb = _diff_attn(l, z, cos_t, sin_t, diff_lambda, vec(diff_subln_g), lam_init, not last)
        qn, qr, kn, v, kr2 = _mla_prep(l, z, vec(mla_gq), vec(mla_gkv), w_uq_b, w_ukv_b, cos_t, sin_t)
        ym = _mla_attn(qn, qr, kn, v, kr2, not last)
        tok = _out_proj(l, ya, yb, ym, yd, w_out, toks, mods, vec(g_post_mix))
        tok = _ffn(l, tok, mods, vec(g_pre_ffn), vec(g_post_ffn), w_up_b, ffn_conv_w, w_down_b, n_rows)
        toks = (tok,)
    return tok.reshape(BATCH, SEQ, D_MODEL)
```

```python
import functools
import math

import jax
import jax.numpy as jnp
from jax import lax
from jax.experimental import pallas as pl
from jax.experimental.pallas import tpu as pltpu

F32 = jnp.float32
BF16 = jnp.bfloat16

D_MODEL = 2048
BATCH = 4
SEQ = 2048
CTX_LEN = 256
GRID_W = 64
DEPTH = 2
NORM_EPS = 1e-6
ROPE_THETA = 10000.0
GROUP_W = 512
POOL_WINDOWS = (2, 4, 8, 16)
HEADS = 4
D_FF = 5632

NX = BATCH * SEQ
NC = BATCH * CTX_LEN
NT = NX + NC

COL_A, COL_Q, COL_K, COL_V = 0, 512, 1024, 1536
COL_DB, COL_DC, COL_DH = 2048, 2560, 3072
COL_CKV, COL_CQ, COL_KR = 3584, 3840, 4224
IN_COLS_PAD = 4608

LANE = 128
SUBLANE = 8
BF16_ROWS = 2 * SUBLANE
TM = 512
TM_BIG = 1024
TS = 256
QSUB = 256
TN_IN = IN_COLS_PAD // 3
TF = 512
TN_ADA = 1024
VMEM_LIMIT = 56 * 1024 * 1024
VMEM_LIMIT_BIG = 62 * 1024 * 1024


def _params(sem, vmem_limit=VMEM_LIMIT):
    return pltpu.CompilerParams(dimension_semantics=sem, vmem_limit_bytes=vmem_limit)


def _split_rows_specs(rows, n_lat_rows):
    n_lat = n_lat_rows // rows
    return [pl.BlockSpec((rows, D_MODEL), lambda i, *_: (jnp.minimum(i, n_lat - 1), 0)),
            pl.BlockSpec((rows, D_MODEL), lambda i, *_: (jnp.maximum(i - n_lat, 0), 0),
                         pipeline_mode=pl.Buffered(1))]


def _pick_rows(i, rows, x_ref, c_ref, fn):
    if c_ref is None:
        fn(x_ref)
        return

    @pl.when(i * rows < NX)
    def _():
        fn(x_ref)

    @pl.when(i * rows >= NX)
    def _():
        fn(c_ref)


def _rms(x, g):
    ms = jnp.mean(x * x, axis=-1, keepdims=True)
    return x * lax.rsqrt(ms + NORM_EPS) * g


def _nt_dot(a, b):
    return lax.dot_general(a, b, (((1,), (1,)), ((), ())), preferred_element_type=F32)


def _rope(v, cos, sin):
    lane = lax.broadcasted_iota(jnp.int32, v.shape, 1)
    sw = jnp.where((lane & 32) == 0, pltpu.roll(v, 96, 1), pltpu.roll(v, 32, 1))
    return v * cos + sw * sin


def _seq_pos(tile, rows):
    r = tile * rows + lax.broadcasted_iota(jnp.int32, (rows, 1), 0)
    seqlen = jnp.where(r < NX, SEQ, CTX_LEN)
    return r & (seqlen - 1), seqlen


def _tile_edges(tile, rows):
    r0 = tile * rows
    seqlen = jnp.where(r0 < NX, SEQ, CTX_LEN)
    has_prev = jnp.where((r0 & (seqlen - 1)) == 0, 0.0, 1.0)
    has_next = jnp.where(((r0 + rows) & (seqlen - 1)) == 0, 0.0, 1.0)
    return has_prev, has_next


def _conv3_rows(u, prev_row, next_row, c_ref, is_ctx):
    n = u.shape[0]
    row8 = lax.broadcasted_iota(jnp.int32, (SUBLANE, 1), 0)
    prev = pltpu.roll(u, 1, 0)
    prev = jnp.concatenate([jnp.where(row8 == 0, prev_row, prev[:SUBLANE]), prev[SUBLANE:]], axis=0)
    nxt = pltpu.roll(u, n - 1, 0)
    nxt = jnp.concatenate([nxt[:n - SUBLANE],
                           jnp.where(row8 == SUBLANE - 1, next_row, nxt[n - SUBLANE:])], axis=0)
    ctx_row = row8 + jnp.where(is_ctx, 0, 2 * SUBLANE)
    for p in range(CTX_LEN, n, CTX_LEN):
        prev = jnp.concatenate([prev[:p], jnp.where(ctx_row == 0, 0.0, prev[p:p + SUBLANE]),
                                prev[p + SUBLANE:]], axis=0)
        nxt = jnp.concatenate([nxt[:p - SUBLANE],
                               jnp.where(ctx_row == SUBLANE - 1, 0.0, nxt[p - SUBLANE:p]), nxt[p:]],
                              axis=0)
    return c_ref[0:1, :] * prev + c_ref[1:2, :] * u + c_ref[2:3, :] * nxt


def _first_axis(i, *_):
    return i


def _halo_specs(rows, width, n_rows, col, tile=_first_axis):
    r8 = rows // SUBLANE
    last8 = n_rows // SUBLANE - 1
    return [
        pl.BlockSpec((rows, width), lambda *g: (tile(*g), col)),
        pl.BlockSpec((SUBLANE, width), lambda *g: (jnp.maximum(tile(*g) * r8 - 1, 0), col)),
        pl.BlockSpec((SUBLANE, width), lambda *g: (jnp.minimum((tile(*g) + 1) * r8, last8), col)),
    ]


def _layer_spec(l, shape):
    return pl.BlockSpec((None,) + shape, lambda *_: (l,) + (0,) * len(shape))


def _mod_spec(l, rows, tile=_first_axis):
    def index(*g):
        r = tile(*g) * rows
        return (l, jnp.where(r < NX, r // SEQ, BATCH), 0, 0)
    return pl.BlockSpec((None, None, 6, D_MODEL), index)


def _ada_kernel(c_ref, w_ref, b_ref, o_ref):
    c = c_ref[...]
    a = c * jax.nn.sigmoid(c)
    o_ref[...] = jnp.dot(a.astype(BF16), w_ref[...].astype(BF16),
                         preferred_element_type=F32) + b_ref[...]


def _ada(cc, ada_w, ada_b):
    n = ada_w.shape[-1]
    return pl.pallas_call(
        _ada_kernel,
        grid=(DEPTH, n // TN_ADA),
        in_specs=[
            pl.BlockSpec((8, D_MODEL), lambda l, j: (0, 0)),
            pl.BlockSpec((None, D_MODEL, TN_ADA), lambda l, j: (l, 0, j)),
            pl.BlockSpec((None, 1, TN_ADA), lambda l, j: (l, 0, j)),
        ],
        out_specs=pl.BlockSpec((None, 8, TN_ADA), lambda l, j: (l, 0, j)),
        out_shape=jax.ShapeDtypeStruct((DEPTH, 8, n), F32),
        compiler_params=_params(("arbitrary", "arbitrary")),
        name="ada",
    )(cc, ada_w, ada_b.reshape(DEPTH, 1, n))


def _in_kernel(split, x_ref, *refs):
    c_ref, (mod_ref, g_ref, w_ref, z_ref, h_ref, r_ref) = ((refs[0], refs[1:]) if split
                                                              else (None, refs))

    @pl.when(pl.program_id(1) == 0)
    def _():
        def put(src):
            x = src[...]
            r_ref[...] = lax.rsqrt(jnp.mean(x * x, axis=-1, keepdims=True) + NORM_EPS)

            shape = (BF16_ROWS, D_MODEL)
            g = jnp.broadcast_to(g_ref[...], shape)
            scale = jnp.broadcast_to(1.0 + mod_ref[1:2, :], shape)
            shift = jnp.broadcast_to(mod_ref[0:1, :], shape)

            def slab(k, carry):
                rows = pl.ds(pl.multiple_of(k * BF16_ROWS, BF16_ROWS), BF16_ROWS)
                h_ref[rows, :] = (src[rows, :] * r_ref[rows, :] * g * scale + shift).astype(BF16)
                return carry

            lax.fori_loop(0, TM_BIG // BF16_ROWS, slab, 0, unroll=4)

        _pick_rows(pl.program_id(0), TM_BIG, x_ref, c_ref, put)

    z_ref[...] = _nt_dot(h_ref[...], w_ref[...])


def _in_proj(l, toks, mods, g, w):
    split = len(toks) == 2
    rows = (_split_rows_specs(TM_BIG, NX) if split
            else [pl.BlockSpec((TM_BIG, D_MODEL), lambda i, j: (i, 0))])
    return pl.pallas_call(
        functools.partial(_in_kernel, split),
        grid=(NT // TM_BIG, IN_COLS_PAD // TN_IN),
        in_specs=rows + [
            _mod_spec(l, TM_BIG),
            _layer_spec(l, (1, D_MODEL)),
            pl.BlockSpec((None, TN_IN, D_MODEL), lambda i, j: (l, j, 0)),
        ],
        out_specs=pl.BlockSpec((TM_BIG, TN_IN), lambda i, j: (i, j)),
        out_shape=jax.ShapeDtypeStruct((NT, IN_COLS_PAD), F32),
        scratch_shapes=[pltpu.VMEM((TM_BIG, D_MODEL), BF16), pltpu.VMEM((TM_BIG, 1), F32)],
        compiler_params=_params(("arbitrary", "arbitrary"), VMEM_LIMIT_BIG),
        name="in_proj",
    )(*toks, mods, g, w)


def _pool_conv_kernel(a_ref, ap_ref, an_ref, db_ref, dc_ref, dcp_ref, dcn_ref,
                      dh_ref, dhp_ref, dhn_ref, pw_ref, ps_ref, cw_ref,
                      ya_ref, yd_ref, e_ref):
    i = pl.program_id(0)
    pos, seqlen = _seq_pos(i, TS)
    pf, nf = _tile_edges(i, TS)

    e_ref[0:8, :] = ap_ref[...] * pf
    e_ref[8:8 + TS, :] = a_ref[...]
    e_ref[8 + TS:16 + TS, :] = an_ref[...] * nf
    for g, w in enumerate(POOL_WINDOWS):
        half = w // 2
        cs = slice(g * LANE, (g + 1) * LANE)
        acc = e_ref[8 - half:8 - half + TS, cs]
        for k in range(1, w):
            acc = acc + e_ref[8 - half + k:8 - half + k + TS, cs]
        cnt = (jnp.minimum(pos + half, seqlen) - jnp.maximum(pos - half, 0)).astype(F32)
        pooled = acc / cnt - a_ref[:, cs]
        y = jnp.dot(pooled.astype(BF16), pw_ref[g], preferred_element_type=F32)
        ya_ref[:, cs] = (y * ps_ref[:, cs]).astype(BF16)

    p_prev = dcp_ref[SUBLANE - 1:SUBLANE, :] * dhp_ref[SUBLANE - 1:SUBLANE, :] * pf
    p_next = dcn_ref[0:1, :] * dhn_ref[0:1, :] * nf
    conv = _conv3_rows(dc_ref[...] * dh_ref[...], p_prev, p_next, cw_ref, False)
    yd_ref[...] = (db_ref[...] * conv).astype(BF16)


def _pool_conv(l, z, n_rows, pool_w, pool_scale, conv_w):
    ca, cb, cc, ch = (COL_A // GROUP_W, COL_DB // GROUP_W, COL_DC // GROUP_W, COL_DH // GROUP_W)
    out = jax.ShapeDtypeStruct((n_rows, GROUP_W), BF16)
    return pl.pallas_call(
        _pool_conv_kernel,
        grid=(n_rows // TS,),
        in_specs=(_halo_specs(TS, GROUP_W, NT, ca)
                  + [pl.BlockSpec((TS, GROUP_W), lambda i: (i, cb))]
                  + _halo_specs(TS, GROUP_W, NT, cc) + _halo_specs(TS, GROUP_W, NT, ch)
                  + [_layer_spec(l, (len(POOL_WINDOWS), LANE, LANE)),
                     _layer_spec(l, (1, GROUP_W)), _layer_spec(l, (3, GROUP_W))]),
        out_specs=[pl.BlockSpec((TS, GROUP_W), lambda i: (i, 0))] * 2,
        out_shape=[out, out],
        scratch_shapes=[pltpu.VMEM((TS + 16, GROUP_W), F32)],
        compiler_params=_params(("arbitrary",)),
        name="pool_conv",
    )(z, z, z, z, z, z, z, z, z, z, pool_w, pool_scale, conv_w)


CBLK = NX // CTX_LEN


def _softmax_pv(s, v):
    e = jnp.exp(s - jnp.max(s, axis=-1, keepdims=True)).astype(BF16)
    return jnp.dot(e, v, preferred_element_type=F32)


def _with_ones(v):
    return jnp.concatenate([v, jnp.ones(v.shape, v.dtype)], axis=-1)


def _attn_calls(lat_fn, ctx_fn, name, lat_ins, lat_specs, ctx_ins, ctx_specs, scratch, with_ctx,
                l, w_f32):
    steps = BATCH * HEADS
    w_rows, w_cols = w_f32.shape[1] // steps, w_f32.shape[2]

    def lat_body(*refs):
        n = len(lat_ins)
        w_ref, (o_ref, wb_ref), rest = refs[n], refs[-len(scratch) - 2:-len(scratch)], refs[-len(scratch):]
        wb_ref[...] = w_ref[...].astype(BF16)
        lat_fn(*refs[:n], o_ref, *rest)

    y0 = [jnp.zeros((NT, GROUP_W), BF16)] if with_ctx else []
    y_shape = jax.ShapeDtypeStruct((NT if with_ctx else NX, GROUP_W), BF16)
    y, w_b = pl.pallas_call(
        lat_body,
        grid=(BATCH, HEADS),
        in_specs=(lat_specs
                  + [pl.BlockSpec((None, w_rows, w_cols), lambda b, h: (l, b * HEADS + h, 0))]
                  + [pl.BlockSpec(memory_space=pl.ANY)] * len(y0)),
        out_specs=[pl.BlockSpec((SEQ, LANE), lambda b, h: (b, h)),
                   pl.BlockSpec((w_rows, w_cols), lambda b, h: (b * HEADS + h, 0))],
        out_shape=[y_shape, jax.ShapeDtypeStruct(w_f32.shape[1:], BF16)],
        input_output_aliases={len(lat_ins) + 1: 0} if with_ctx else {},
        scratch_shapes=scratch,
        compiler_params=_params(("arbitrary", "arbitrary")),
        name=name,
    )(*lat_ins, w_f32, *y0)
    if not with_ctx:
        return y, w_b
    y = pl.pallas_call(
        lambda *refs: ctx_fn(*refs[:len(ctx_ins)], refs[-1]),
        grid=(BATCH, HEADS),
        in_specs=ctx_specs + [pl.BlockSpec(memory_space=pl.ANY)],
        out_specs=pl.BlockSpec((CTX_LEN, LANE), lambda b, h: (CBLK + b, h)),
        out_shape=y_shape,
        input_output_aliases={len(ctx_ins): 0},
        compiler_params=_params(("arbitrary", "arbitrary")),
        name=name + "_ctx",
    )(*ctx_ins, y)
    return y, w_b


def _diff_lambda(lam_ref, lam_init):
    lp = lam_ref[...]
    return (jnp.exp(jnp.sum(lp[0:1] * lp[1:2], keepdims=True))
            - jnp.exp(jnp.sum(lp[2:3] * lp[3:4], keepdims=True)) + lam_init)


def _diff_chain(q, k, v, lam, g, lam_init):
    q = q * (64 ** -0.5)
    lane = lax.broadcasted_iota(jnp.int32, q.shape, 1)
    o1 = _softmax_pv(_nt_dot(jnp.where(lane < 64, q, 0.0).astype(BF16), k[...]), v[...])
    o2 = _softmax_pv(_nt_dot(jnp.where(lane >= 64, q, 0.0).astype(BF16), k[...]), v[...])
    o = o1[:, :LANE] / o1[:, LANE:] - o2[:, :LANE] * (lam / o2[:, LANE:])
    return (_rms(o, g) * (1.0 - lam_init)).astype(BF16)


def _diff_kernel(lam_init, q_ref, kx_ref, kc_ref, vx_ref, vc_ref, cos_ref, sin_ref, lam_ref, g_ref,
                 o_ref, kb_ref, vb_ref):
    kb_ref[0:SEQ, :] = _rope(kx_ref[...], cos_ref[...], sin_ref[...]).astype(BF16)
    kb_ref[SEQ:, :] = kc_ref[...].astype(BF16)
    vb_ref[0:SEQ, 0:LANE] = vx_ref[...].astype(BF16)
    vb_ref[SEQ:, 0:LANE] = vc_ref[...].astype(BF16)
    vb_ref[:, LANE:] = jnp.ones((SEQ + CTX_LEN, LANE), BF16)
    lam = _diff_lambda(lam_ref, lam_init)
    for r in range(0, SEQ, QSUB):
        rows = slice(r, r + QSUB)
        q = _rope(q_ref[rows, :], cos_ref[rows, :], sin_ref[rows, :])
        o_ref[rows, :] = _diff_chain(q, kb_ref, vb_ref, lam, g_ref[...], lam_init)


def _diff_ctx_kernel(lam_init, q_ref, k_ref, v_ref, lam_ref, g_ref, o_ref):
    o_ref[...] = _diff_chain(q_ref[...], k_ref[...].astype(BF16),
                             _with_ones(v_ref[...].astype(BF16)),
                             _diff_lambda(lam_ref, lam_init), g_ref[...], lam_init)


def _diff_attn(l, z, cos_t, sin_t, lam_p, subln_g, lam_init, with_ctx, w_f32):
    cq, ck, cv = COL_Q // LANE, COL_K // LANE, COL_V // LANE

    def xs(col):
        return pl.BlockSpec((SEQ, LANE), lambda b, h: (b, col + h))

    def cs(col):
        return pl.BlockSpec((CTX_LEN, LANE), lambda b, h: (CBLK + b, col + h))

    table = pl.BlockSpec((SEQ, LANE), lambda b, h: (0, 0))
    params = [_layer_spec(l, (4, 64)), _layer_spec(l, (1, LANE))]
    return _attn_calls(
        functools.partial(_diff_kernel, lam_init), functools.partial(_diff_ctx_kernel, lam_init),
        "diff_attn",
        [z, z, z, z, z, cos_t, sin_t, lam_p, subln_g],
        [xs(cq), xs(ck), cs(ck), xs(cv), cs(cv), table, table] + params,
        [z, z, z, lam_p, subln_g],
        [cs(cq), cs(ck), cs(cv)] + params,
        [pltpu.VMEM((SEQ + CTX_LEN, LANE), BF16), pltpu.VMEM((SEQ + CTX_LEN, 2 * LANE), BF16)],
        with_ctx, l, w_f32)


def _mla_prep_kernel(ckv_ref, cq_ref, kr_ref, gq_ref, gkv_ref, wq_ref, wkv_ref, cos_ref, sin_ref,
                     qn_ref, qr_ref, kn_ref, v_ref, kr2_ref):
    cos = cos_ref[...]
    sin = sin_ref[...]
    q = jnp.dot(_rms(cq_ref[...], gq_ref[...]).astype(BF16), wq_ref[...],
                preferred_element_type=F32)
    qn_ref[...] = q[:, :GROUP_W].astype(BF16)
    for c in range(0, 2 * LANE, LANE):
        qr_ref[:, c:c + LANE] = _rope(q[:, GROUP_W + c:GROUP_W + c + LANE], cos, sin).astype(BF16)
    kv = jnp.dot(_rms(ckv_ref[...], gkv_ref[...]).astype(BF16), wkv_ref[...],
                 preferred_element_type=F32)
    kn_ref[...] = kv[:, :GROUP_W].astype(BF16)
    v_ref[...] = kv[:, GROUP_W:].astype(BF16)
    kr = _rope(kr_ref[...], cos, sin)
    kr2_ref[...] = (kr + pltpu.roll(kr, 64, 1)).astype(BF16)


def _mla_prep(l, z, gq, gkv, wq, wkv, cos_t, sin_t):
    def out(w):
        return jax.ShapeDtypeStruct((NT, w), BF16)

    def row(w):
        return pl.BlockSpec((TM, w), lambda i: (i, 0))

    rope = pl.BlockSpec((TM, LANE), lambda i: (jnp.where(i * TM < NX, i % (SEQ // TM), SEQ // TM), 0))
    return pl.pallas_call(
        _mla_prep_kernel,
        grid=(NT // TM,),
        in_specs=[
            pl.BlockSpec((TM, 256), lambda i: (i, COL_CKV // 256)),
            pl.BlockSpec((TM, 384), lambda i: (i, COL_CQ // 384)),
            pl.BlockSpec((TM, LANE), lambda i: (i, COL_KR // LANE)),
            _layer_spec(l, (1, 384)),
            _layer_spec(l, (1, 256)),
            _layer_spec(l, (384, 768)),
            _layer_spec(l, (256, 1024)),
            rope, rope,
        ],
        out_specs=[row(GROUP_W), row(2 * LANE), row(GROUP_W), row(GROUP_W), row(LANE)],
        out_shape=[out(GROUP_W), out(2 * LANE), out(GROUP_W), out(GROUP_W), out(LANE)],
        compiler_params=_params(("arbitrary",)),
        name="mla_prep",
    )(z, z, z, gq, gkv, wq, wkv, cos_t, sin_t)


def _mla_chain(h, qn, qr, k, v):
    lane = lax.broadcasted_iota(jnp.int32, qr.shape, 1)
    lo = (h % 2) * 64
    qr = jnp.where((lane >= lo) & (lane < lo + 64), qr, jnp.zeros_like(qr))
    s = _nt_dot(jnp.concatenate([qn, qr], axis=-1), k[...]) * ((128 + 64) ** -0.5)
    o = _softmax_pv(s, v[...])
    return (o[:, :LANE] / o[:, LANE:]).astype(BF16)


def _mla_kernel(qn_ref, qr_ref, knx_ref, knc_ref, krx_ref, krc_ref, vx_ref, vc_ref,
                o_ref, kb_ref, vb_ref):
    h = pl.program_id(1)
    kb_ref[0:SEQ, 0:LANE] = knx_ref[...]
    kb_ref[SEQ:, 0:LANE] = knc_ref[...]
    kb_ref[0:SEQ, LANE:] = krx_ref[...]
    kb_ref[SEQ:, LANE:] = krc_ref[...]
    vb_ref[0:SEQ, 0:LANE] = vx_ref[...]
    vb_ref[SEQ:, 0:LANE] = vc_ref[...]
    vb_ref[:, LANE:] = jnp.ones((SEQ + CTX_LEN, LANE), BF16)
    for r in range(0, SEQ, QSUB):
        rows = slice(r, r + QSUB)
        o_ref[rows, :] = _mla_chain(h, qn_ref[rows, :], qr_ref[rows, :], kb_ref, vb_ref)


def _mla_ctx_kernel(qn_ref, qr_ref, kn_ref, kr_ref, v_ref, o_ref):
    k = jnp.concatenate([kn_ref[...], kr_ref[...]], axis=-1)
    o_ref[...] = _mla_chain(pl.program_id(1), qn_ref[...], qr_ref[...], k, _with_ones(v_ref[...]))


def _mla_attn(l, qn, qr, kn, v, kr2, with_ctx, w_f32):
    def xs(col):
        return pl.BlockSpec((SEQ, LANE), lambda b, h: (b, col(h)))

    def cs(col):
        return pl.BlockSpec((CTX_LEN, LANE), lambda b, h: (CBLK + b, col(h)))

    head, pair, first = (lambda h: h), (lambda h: h // 2), (lambda h: 0)
    return _attn_calls(
        _mla_kernel, _mla_ctx_kernel, "mla_attn",
        [qn, qr, kn, kn, kr2, kr2, v, v],
        [xs(head), xs(pair), xs(head), cs(head), xs(first), cs(first), xs(head), cs(head)],
        [qn, qr, kn, kr2, v],
        [cs(head), cs(pair), cs(head), cs(first), cs(head)],
        [pltpu.VMEM((SEQ + CTX_LEN, 2 * LANE), BF16), pltpu.VMEM((SEQ + CTX_LEN, 2 * LANE), BF16)],
        with_ctx, l, w_f32)


def _out_kernel(split, ya_ref, yb_ref, ym_ref, yd_ref, w_ref, x_ref, *refs):
    c_ref, (mod_ref, gpost_ref, xo_ref, wb_ref) = (refs[0], refs[1:]) if split else (None, refs)

    @pl.when(pl.program_id(0) == 0)
    def _():
        wb_ref[...] = w_ref[...].astype(BF16)

    y = jnp.dot(ya_ref[...], wb_ref[0:GROUP_W, :], preferred_element_type=F32)
    y += jnp.dot(yb_ref[...], wb_ref[GROUP_W:2 * GROUP_W, :], preferred_element_type=F32)
    y += jnp.dot(ym_ref[...], wb_ref[2 * GROUP_W:3 * GROUP_W, :], preferred_element_type=F32)
    y += jnp.dot(yd_ref[...], wb_ref[3 * GROUP_W:, :], preferred_element_type=F32)
    delta = mod_ref[2:3, :] * _rms(y, gpost_ref[...])

    def add_residual(src):
        xo_ref[...] = src[...] + delta

    _pick_rows(pl.program_id(0), TM, x_ref, c_ref, add_residual)


def _out_proj(l, ya, yb, ym, yd, w, toks, mods, g_post):
    n = ya.shape[0]
    split = len(toks) == 2
    y_spec = pl.BlockSpec((TM, GROUP_W), lambda i: (i, 0))
    row = pl.BlockSpec((TM, D_MODEL), lambda i: (i, 0))
    return pl.pallas_call(
        functools.partial(_out_kernel, split),
        grid=(n // TM,),
        in_specs=[
            y_spec, y_spec, y_spec, y_spec,
            pl.BlockSpec((None, D_MODEL, D_MODEL), lambda i: (l, 0, 0),
                         pipeline_mode=pl.Buffered(1)),
        ] + (_split_rows_specs(TM, NX) if split else [row]) + [
            _mod_spec(l, TM),
            _layer_spec(l, (1, D_MODEL)),
        ],
        out_specs=row,
        out_shape=jax.ShapeDtypeStruct((n, D_MODEL), F32),
        scratch_shapes=[pltpu.VMEM((D_MODEL, D_MODEL), BF16)],
        compiler_params=_params(("arbitrary",)),
        name="out_proj",
    )(ya, yb, ym, yd, w, *toks, mods, g_post)


def _ffn_kernel(x_ref, xp_ref, xn_ref, mod_ref, gpre_ref, gpost_ref, wg_ref, wv_ref, cg_ref, cv_ref,
                wd_ref, xo_ref, hx_ref, r_ref):
    i = pl.program_id(0)
    j = pl.program_id(1)
    n = TM_BIG

    @pl.when(j == 0)
    def _():
        def hmod(x, r):
            return x * r * gpre_ref[...] * (1.0 + mod_ref[4:5, :]) + mod_ref[3:4, :]

        def rinv(x):
            return lax.rsqrt(jnp.mean(x * x, axis=-1, keepdims=True) + NORM_EPS)

        r_ref[...] = rinv(x_ref[...])

        def slab(k, carry):
            rows = pl.ds(pl.multiple_of(k * BF16_ROWS, BF16_ROWS), BF16_ROWS)
            hx_ref[rows, :] = hmod(x_ref[rows, :], r_ref[rows, :]).astype(BF16)
            return carry

        lax.fori_loop(0, n // BF16_ROWS, slab, 0, unroll=4)
        has_prev, has_next = _tile_edges(i, n)
        halo = jnp.concatenate([xp_ref[...], xn_ref[...]], axis=0)
        keep = jnp.where(lax.broadcasted_iota(jnp.int32, (BF16_ROWS, 1), 0) < SUBLANE,
                         has_prev, has_next)
        hx_ref[n:, :] = (hmod(halo, rinv(halo)) * keep).astype(BF16)
        xo_ref[...] = jnp.zeros_like(xo_ref)

    is_ctx = i * n >= NX
    hx = hx_ref[...]

    def conv_up(w_ref, c_ref):
        u = jnp.dot(hx, w_ref[...], preferred_element_type=F32)
        return _conv3_rows(u[:n], u[n + SUBLANE - 1:n + SUBLANE], u[n + SUBLANE:n + SUBLANE + 1],
                           c_ref, is_ctx)

    gate = conv_up(wg_ref, cg_ref)
    val = conv_up(wv_ref, cv_ref)
    act = (gate * jax.nn.sigmoid(gate) * val).astype(BF16)
    xo_ref[...] += jnp.dot(act, wd_ref[...], preferred_element_type=F32)

    @pl.when(j == pl.num_programs(1) - 1)
    def _():
        y = xo_ref[...]
        r_ref[...] = lax.rsqrt(jnp.mean(y * y, axis=-1, keepdims=True) + NORM_EPS)

        def slab(k, carry):
            rows = pl.ds(pl.multiple_of(k * BF16_ROWS, BF16_ROWS), BF16_ROWS)
            normed = xo_ref[rows, :] * r_ref[rows, :] * gpost_ref[...]
            xo_ref[rows, :] = x_ref[rows, :] + mod_ref[5:6, :] * normed
            return carry

        lax.fori_loop(0, n // BF16_ROWS, slab, 0, unroll=4)


def _ffn(l, tok, mods, g_pre, g_post, w_up, conv_w, w_down, n_rows):
    nj = D_FF // TF
    return pl.pallas_call(
        _ffn_kernel,
        grid=(n_rows // TM_BIG, nj),
        in_specs=(_halo_specs(TM_BIG, D_MODEL, n_rows, 0) + [
            _mod_spec(l, TM_BIG),
            _layer_spec(l, (1, D_MODEL)),
            _layer_spec(l, (1, D_MODEL)),
            pl.BlockSpec((D_MODEL, TF), lambda i, j: (0, j)),
            pl.BlockSpec((D_MODEL, TF), lambda i, j: (0, j + nj)),
            pl.BlockSpec((None, 3, TF), lambda i, j: (l, 0, j)),
            pl.BlockSpec((None, 3, TF), lambda i, j: (l, 0, j + nj)),
            pl.BlockSpec((TF, D_MODEL), lambda i, j: (j, 0)),
        ]),
        out_specs=pl.BlockSpec((TM_BIG, D_MODEL), lambda i, j: (i, 0)),
        out_shape=jax.ShapeDtypeStruct((n_rows, D_MODEL), F32),
        scratch_shapes=[pltpu.VMEM((TM_BIG + BF16_ROWS, D_MODEL), BF16),
                        pltpu.VMEM((TM_BIG, 1), F32)],
        compiler_params=_params(("arbitrary", "arbitrary"), VMEM_LIMIT_BIG),
        name="ffn",
    )(tok, tok, tok, mods, g_pre, g_post, w_up, w_up, conv_w, conv_w, w_down)


def _rope_tables():
    rows = SEQ // GRID_W
    row = jnp.repeat(jnp.arange(rows), GRID_W).astype(F32)
    col = jnp.tile(jnp.arange(GRID_W), rows).astype(F32)
    n_freq = 64 // 4
    inv = ROPE_THETA ** (-jnp.arange(n_freq, dtype=F32) / n_freq)
    ang = jnp.concatenate([row[:, None] * inv, col[:, None] * inv], axis=-1)
    cos, sin = jnp.cos(ang), jnp.sin(ang)
    cos_t = jnp.concatenate([cos, cos, cos, cos], axis=-1)
    sin_t = jnp.concatenate([-sin, sin, -sin, sin], axis=-1)
    cos_t = jnp.concatenate([cos_t, jnp.ones((TM, LANE), F32)], axis=0)
    sin_t = jnp.concatenate([sin_t, jnp.zeros((TM, LANE), F32)], axis=0)
    return cos_t, sin_t


W_IN_KB = 256


def _w_in_kernel(w_ref, o_ref):
    for dst, src, n in ((0, 0, 2048), (COL_DB, 2752, 1536), (COL_CKV, 2432, 256),
                        (COL_CQ, 2048, 384), (COL_KR, 2688, 64)):
        o_ref[dst:dst + n, :] = w_ref[src:src + n, :].astype(BF16)
    o_ref[COL_KR + 64:, :] = jnp.zeros((IN_COLS_PAD - COL_KR - 64, W_IN_KB), BF16)


def _layout_w_in(w):
    wt = jnp.swapaxes(w, 1, 2)
    n, k = wt.shape[1:]
    return pl.pallas_call(
        _w_in_kernel,
        grid=(DEPTH, k // W_IN_KB),
        in_specs=[pl.BlockSpec((None, n, W_IN_KB), lambda l, i: (l, 0, i))],
        out_specs=pl.BlockSpec((None, IN_COLS_PAD, W_IN_KB), lambda l, i: (l, 0, i)),
        out_shape=jax.ShapeDtypeStruct((DEPTH, IN_COLS_PAD, k), BF16),
        compiler_params=_params(("arbitrary", "arbitrary")),
        name="w_in_layout",
    )(wt)


def _layout_w_uq(w):
    w = w.reshape(DEPTH, 384, HEADS, 192)
    return jnp.concatenate([w[..., :128].reshape(DEPTH, 384, 512),
                            w[..., 128:].reshape(DEPTH, 384, 256)], axis=-1).astype(BF16)


def _layout_w_ukv(w):
    w = w.reshape(DEPTH, 256, HEADS, 256)
    return jnp.concatenate([w[..., :128].reshape(DEPTH, 256, 512),
                            w[..., 128:].reshape(DEPTH, 256, 512)], axis=-1).astype(BF16)


def kernel(x, c, ctx, c_ctx, ada_w, ada_b, g_pre_mix, g_post_mix, g_pre_ffn, g_post_ffn, w_in,
           pool_w, pool_scale, diff_lambda, diff_subln_g, mla_gq, mla_w_uq, mla_gkv, mla_w_ukv,
           conv_w, w_out, ffn_w_up, ffn_conv_w, ffn_w_down):
    toks = (x.reshape(NX, D_MODEL), ctx.reshape(NC, D_MODEL))
    cc = jnp.concatenate([c, c_ctx[None, :], jnp.zeros((3, D_MODEL), F32)], axis=0)
    mods = _ada(cc, ada_w, ada_b).reshape(DEPTH, 8, 6, D_MODEL)
    cos_t, sin_t = _rope_tables()

    def vec(p):
        return p.reshape(DEPTH, 1, p.shape[-1])

    w_in_b = _layout_w_in(w_in)
    w_uq_b, w_ukv_b = _layout_w_uq(mla_w_uq), _layout_w_ukv(mla_w_ukv)
    pool_w_b = pool_w.astype(BF16)

    for l in range(DEPTH):
        last = l == DEPTH - 1
        n_rows = NX if last else NT
        lam_init = 0.8 - 0.6 * math.exp(-0.3 * l)

        z = _in_proj(l, toks, mods, vec(g_pre_mix), w_in_b)
        ya, yd = _pool_conv(l, z, n_rows, pool_w_b, vec(pool_scale), conv_w)
        yb, w_up_b = _diff_attn(l, z, cos_t, sin_t, diff_lambda, vec(diff_subln_g), lam_init,
                                not last, ffn_w_up)
        qn, qr, kn, v, kr2 = _mla_prep(l, z, vec(mla_gq), vec(mla_gkv), w_uq_b, w_ukv_b, cos_t, sin_t)
        ym, w_down_b = _mla_attn(l, qn, qr, kn, v, kr2, not last, ffn_w_down)
        tok = _out_proj(l, ya, yb, ym, yd, w_out, toks, mods, vec(g_post_mix))
        tok = _ffn(l, tok, mods, vec(g_pre_ffn), vec(g_post_ffn), w_up_b, ffn_conv_w, w_down_b, n_rows)
        toks = (tok,)
    return tok.reshape(BATCH, SEQ, D_MODEL)
```

```python
import functools
import math

import jax
import jax.numpy as jnp
from jax import lax
from jax.experimental import pallas as pl
from jax.experimental.pallas import tpu as pltpu

F32 = jnp.float32
BF16 = jnp.bfloat16

D_MODEL = 2048
BATCH = 4
SEQ = 2048
CTX_LEN = 256
GRID_W = 64
DEPTH = 2
NORM_EPS = 1e-6
ROPE_THETA = 10000.0
GROUP_W = 512
POOL_WINDOWS = (2, 4, 8, 16)
HEADS = 4
D_FF = 5632

NX = BATCH * SEQ
NC = BATCH * CTX_LEN
NT = NX + NC

COL_A, COL_Q, COL_K, COL_V = 0, 512, 1024, 1536
COL_DB, COL_DC, COL_DH = 2048, 2560, 3072
COL_CKV, COL_CQ, COL_KR = 3584, 3840, 4224
IN_COLS_PAD = 4608

LANE = 128
SUBLANE = 8
BF16_ROWS = 2 * SUBLANE
TM = 512
TM_BIG = 1024
TS = 256
QSUB = 256
TN_IN = IN_COLS_PAD // 3
TF = 512
TN_ADA = 1024
VMEM_LIMIT = 56 * 1024 * 1024
VMEM_LIMIT_BIG = 62 * 1024 * 1024


def _params(sem, vmem_limit=VMEM_LIMIT):
    return pltpu.CompilerParams(dimension_semantics=sem, vmem_limit_bytes=vmem_limit)


def _split_rows_specs(rows, n_lat_rows):
    n_lat = n_lat_rows // rows
    return [pl.BlockSpec((rows, D_MODEL), lambda i, *_: (jnp.minimum(i, n_lat - 1), 0)),
            pl.BlockSpec((rows, D_MODEL), lambda i, *_: (jnp.maximum(i - n_lat, 0), 0),
                         pipeline_mode=pl.Buffered(1))]


def _pick_rows(i, rows, x_ref, c_ref, fn):
    if c_ref is None:
        fn(x_ref)
        return

    @pl.when(i * rows < NX)
    def _():
        fn(x_ref)

    @pl.when(i * rows >= NX)
    def _():
        fn(c_ref)


def _rms(x, g):
    ms = jnp.mean(x * x, axis=-1, keepdims=True)
    return x * lax.rsqrt(ms + NORM_EPS) * g


def _store_rinv(r_ref, x):
    r = lax.rsqrt(jnp.mean(x * x, axis=-1, keepdims=True) + NORM_EPS)
    r_ref[...] = jnp.broadcast_to(r, r_ref.shape)


def _load_rinv(r_ref, rows):
    return jnp.tile(r_ref[rows, :], (1, D_MODEL // LANE))


def _nt_dot(a, b):
    return lax.dot_general(a, b, (((1,), (1,)), ((), ())), preferred_element_type=F32)


def _rope(v, cos, sin):
    lane = lax.broadcasted_iota(jnp.int32, v.shape, 1)
    sw = jnp.where((lane & 32) == 0, pltpu.roll(v, 96, 1), pltpu.roll(v, 32, 1))
    return v * cos + sw * sin


def _seq_pos(tile, rows):
    r = tile * rows + lax.broadcasted_iota(jnp.int32, (rows, 1), 0)
    seqlen = jnp.where(r < NX, SEQ, CTX_LEN)
    return r & (seqlen - 1), seqlen


def _tile_edges(tile, rows):
    r0 = tile * rows
    seqlen = jnp.where(r0 < NX, SEQ, CTX_LEN)
    has_prev = jnp.where((r0 & (seqlen - 1)) == 0, 0.0, 1.0)
    has_next = jnp.where(((r0 + rows) & (seqlen - 1)) == 0, 0.0, 1.0)
    return has_prev, has_next


def _conv3_rows(u, prev_row, next_row, c_ref, is_ctx):
    n = u.shape[0]
    row8 = lax.broadcasted_iota(jnp.int32, (SUBLANE, 1), 0)
    prev = pltpu.roll(u, 1, 0)
    prev = jnp.concatenate([jnp.where(row8 == 0, prev_row, prev[:SUBLANE]), prev[SUBLANE:]], axis=0)
    nxt = pltpu.roll(u, n - 1, 0)
    nxt = jnp.concatenate([nxt[:n - SUBLANE],
                           jnp.where(row8 == SUBLANE - 1, next_row, nxt[n - SUBLANE:])], axis=0)
    ctx_row = row8 + jnp.where(is_ctx, 0, 2 * SUBLANE)
    for p in range(CTX_LEN, n, CTX_LEN):
        prev = jnp.concatenate([prev[:p], jnp.where(ctx_row == 0, 0.0, prev[p:p + SUBLANE]),
                                prev[p + SUBLANE:]], axis=0)
        nxt = jnp.concatenate([nxt[:p - SUBLANE],
                               jnp.where(ctx_row == SUBLANE - 1, 0.0, nxt[p - SUBLANE:p]), nxt[p:]],
                              axis=0)
    return c_ref[0:1, :] * prev + c_ref[1:2, :] * u + c_ref[2:3, :] * nxt


def _first_axis(i, *_):
    return i


def _halo_specs(rows, width, n_rows, col, tile=_first_axis):
    r8 = rows // SUBLANE
    last8 = n_rows // SUBLANE - 1
    return [
        pl.BlockSpec((rows, width), lambda *g: (tile(*g), col)),
        pl.BlockSpec((SUBLANE, width), lambda *g: (jnp.maximum(tile(*g) * r8 - 1, 0), col)),
        pl.BlockSpec((SUBLANE, width), lambda *g: (jnp.minimum((tile(*g) + 1) * r8, last8), col)),
    ]


def _layer_spec(l, shape):
    return pl.BlockSpec((None,) + shape, lambda *_: (l,) + (0,) * len(shape))


def _mod_spec(l, rows, tile=_first_axis):
    def index(*g):
        r = tile(*g) * rows
        return (l, jnp.where(r < NX, r // SEQ, BATCH), 0, 0)
    return pl.BlockSpec((None, None, 6, D_MODEL), index)


def _ada_kernel(c_ref, w_ref, b_ref, o_ref):
    c = c_ref[...]
    a = c * jax.nn.sigmoid(c)
    o_ref[...] = jnp.dot(a.astype(BF16), w_ref[...].astype(BF16),
                         preferred_element_type=F32) + b_ref[...]


def _ada(cc, ada_w, ada_b):
    n = ada_w.shape[-1]
    return pl.pallas_call(
        _ada_kernel,
        grid=(DEPTH, n // TN_ADA),
        in_specs=[
            pl.BlockSpec((8, D_MODEL), lambda l, j: (0, 0)),
            pl.BlockSpec((None, D_MODEL, TN_ADA), lambda l, j: (l, 0, j)),
            pl.BlockSpec((None, 1, TN_ADA), lambda l, j: (l, 0, j)),
        ],
        out_specs=pl.BlockSpec((None, 8, TN_ADA), lambda l, j: (l, 0, j)),
        out_shape=jax.ShapeDtypeStruct((DEPTH, 8, n), F32),
        compiler_params=_params(("arbitrary", "arbitrary")),
        name="ada",
    )(cc, ada_w, ada_b.reshape(DEPTH, 1, n))


def _in_kernel(split, x_ref, *refs):
    c_ref, (mod_ref, g_ref, w_ref, z_ref, h_ref, r_ref) = ((refs[0], refs[1:]) if split
                                                              else (None, refs))

    @pl.when(pl.program_id(1) == 0)
    def _():
        def put(src):
            _store_rinv(r_ref, src[...])

            shape = (BF16_ROWS, D_MODEL)
            g = jnp.broadcast_to(g_ref[...], shape)
            scale = jnp.broadcast_to(1.0 + mod_ref[1:2, :], shape)
            shift = jnp.broadcast_to(mod_ref[0:1, :], shape)

            def slab(k, carry):
                rows = pl.ds(pl.multiple_of(k * BF16_ROWS, BF16_ROWS), BF16_ROWS)
                y = src[rows, :] * _load_rinv(r_ref, rows) * g
                h_ref[rows, :] = (y * scale + shift).astype(BF16)
                return carry

            lax.fori_loop(0, TM_BIG // BF16_ROWS, slab, 0, unroll=4)

        _pick_rows(pl.program_id(0), TM_BIG, x_ref, c_ref, put)

    z_ref[...] = _nt_dot(h_ref[...], w_ref[...])


def _in_proj(l, toks, mods, g, w):
    split = len(toks) == 2
    rows = (_split_rows_specs(TM_BIG, NX) if split
            else [pl.BlockSpec((TM_BIG, D_MODEL), lambda i, j: (i, 0))])
    return pl.pallas_call(
        functools.partial(_in_kernel, split),
        grid=(NT // TM_BIG, IN_COLS_PAD // TN_IN),
        in_specs=rows + [
            _mod_spec(l, TM_BIG),
            _layer_spec(l, (1, D_MODEL)),
            pl.BlockSpec((None, TN_IN, D_MODEL), lambda i, j: (l, j, 0)),
        ],
        out_specs=pl.BlockSpec((TM_BIG, TN_IN), lambda i, j: (i, j)),
        out_shape=jax.ShapeDtypeStruct((NT, IN_COLS_PAD), F32),
        scratch_shapes=[pltpu.VMEM((TM_BIG, D_MODEL), BF16), pltpu.VMEM((TM_BIG, LANE), F32)],
        compiler_params=_params(("arbitrary", "arbitrary"), VMEM_LIMIT_BIG),
        name="in_proj",
    )(*toks, mods, g, w)


def _pool_conv_kernel(a_ref, ap_ref, an_ref, db_ref, dc_ref, dcp_ref, dcn_ref,
                      dh_ref, dhp_ref, dhn_ref, pw_ref, ps_ref, cw_ref,
                      ya_ref, yd_ref, e_ref):
    i = pl.program_id(0)
    pos, seqlen = _seq_pos(i, TS)
    pf, nf = _tile_edges(i, TS)

    e_ref[0:8, :] = ap_ref[...] * pf
    e_ref[8:8 + TS, :] = a_ref[...]
    e_ref[8 + TS:16 + TS, :] = an_ref[...] * nf
    for g, w in enumerate(POOL_WINDOWS):
        half = w // 2
        cs = slice(g * LANE, (g + 1) * LANE)
        acc = e_ref[8 - half:8 - half + TS, cs]
        for k in range(1, w):
            acc = acc + e_ref[8 - half + k:8 - half + k + TS, cs]
        cnt = (jnp.minimum(pos + half, seqlen) - jnp.maximum(pos - half, 0)).astype(F32)
        pooled = acc / cnt - a_ref[:, cs]
        y = jnp.dot(pooled.astype(BF16), pw_ref[g], preferred_element_type=F32)
        ya_ref[:, cs] = (y * ps_ref[:, cs]).astype(BF16)

    p_prev = dcp_ref[SUBLANE - 1:SUBLANE, :] * dhp_ref[SUBLANE - 1:SUBLANE, :] * pf
    p_next = dcn_ref[0:1, :] * dhn_ref[0:1, :] * nf
    conv = _conv3_rows(dc_ref[...] * dh_ref[...], p_prev, p_next, cw_ref, False)
    yd_ref[...] = (db_ref[...] * conv).astype(BF16)


def _pool_conv(l, z, n_rows, pool_w, pool_scale, conv_w):
    ca, cb, cc, ch = (COL_A // GROUP_W, COL_DB // GROUP_W, COL_DC // GROUP_W, COL_DH // GROUP_W)
    out = jax.ShapeDtypeStruct((n_rows, GROUP_W), BF16)
    return pl.pallas_call(
        _pool_conv_kernel,
        grid=(n_rows // TS,),
        in_specs=(_halo_specs(TS, GROUP_W, NT, ca)
                  + [pl.BlockSpec((TS, GROUP_W), lambda i: (i, cb))]
                  + _halo_specs(TS, GROUP_W, NT, cc) + _halo_specs(TS, GROUP_W, NT, ch)
                  + [_layer_spec(l, (len(POOL_WINDOWS), LANE, LANE)),
                     _layer_spec(l, (1, GROUP_W)), _layer_spec(l, (3, GROUP_W))]),
        out_specs=[pl.BlockSpec((TS, GROUP_W), lambda i: (i, 0))] * 2,
        out_shape=[out, out],
        scratch_shapes=[pltpu.VMEM((TS + 16, GROUP_W), F32)],
        compiler_params=_params(("arbitrary",)),
        name="pool_conv",
    )(z, z, z, z, z, z, z, z, z, z, pool_w, pool_scale, conv_w)


CBLK = NX // CTX_LEN


def _softmax_pv(s, v):
    e = jnp.exp(s - jnp.max(s, axis=-1, keepdims=True)).astype(BF16)
    return jnp.dot(e, v, preferred_element_type=F32)


def _with_ones(v):
    return jnp.concatenate([v, jnp.ones(v.shape, v.dtype)], axis=-1)


def _cast_rows(w_ref, wb_ref):
    wb_ref[...] = w_ref[...].astype(BF16)


def _cast_rows_pair_chunks(w_ref, wb_ref):
    for j in range(D_FF // TF):
        for half in range(2):
            src = half * D_FF + j * TF
            dst = (2 * j + half) * TF
            wb_ref[:, dst:dst + TF] = w_ref[:, src:src + TF].astype(BF16)


def _attn_calls(lat_fn, ctx_fn, name, lat_ins, lat_specs, ctx_ins, ctx_specs, scratch, with_ctx,
                l, w_f32, cast_fn):
    steps = BATCH * HEADS
    w_rows, w_cols = w_f32.shape[1] // steps, w_f32.shape[2]

    def lat_body(*refs):
        n = len(lat_ins)
        w_ref, (o_ref, wb_ref), rest = refs[n], refs[-len(scratch) - 2:-len(scratch)], refs[-len(scratch):]
        cast_fn(w_ref, wb_ref)
        lat_fn(*refs[:n], o_ref, *rest)

    y0 = [jnp.zeros((NT, GROUP_W), BF16)] if with_ctx else []
    y_shape = jax.ShapeDtypeStruct((NT if with_ctx else NX, GROUP_W), BF16)
    y, w_b = pl.pallas_call(
        lat_body,
        grid=(BATCH, HEADS),
        in_specs=(lat_specs
                  + [pl.BlockSpec((None, w_rows, w_cols), lambda b, h: (l, b * HEADS + h, 0))]
                  + [pl.BlockSpec(memory_space=pl.ANY)] * len(y0)),
        out_specs=[pl.BlockSpec((SEQ, LANE), lambda b, h: (b, h)),
                   pl.BlockSpec((w_rows, w_cols), lambda b, h: (b * HEADS + h, 0))],
        out_shape=[y_shape, jax.ShapeDtypeStruct(w_f32.shape[1:], BF16)],
        input_output_aliases={len(lat_ins) + 1: 0} if with_ctx else {},
        scratch_shapes=scratch,
        compiler_params=_params(("arbitrary", "arbitrary")),
        name=name,
    )(*lat_ins, w_f32, *y0)
    if not with_ctx:
        return y, w_b
    y = pl.pallas_call(
        lambda *refs: ctx_fn(*refs[:len(ctx_ins)], refs[-1]),
        grid=(BATCH, HEADS),
        in_specs=ctx_specs + [pl.BlockSpec(memory_space=pl.ANY)],
        out_specs=pl.BlockSpec((CTX_LEN, LANE), lambda b, h: (CBLK + b, h)),
        out_shape=y_shape,
        input_output_aliases={len(ctx_ins): 0},
        compiler_params=_params(("arbitrary", "arbitrary")),
        name=name + "_ctx",
    )(*ctx_ins, y)
    return y, w_b


def _diff_lambda(lam_ref, lam_init):
    lp = lam_ref[...]
    return (jnp.exp(jnp.sum(lp[0:1] * lp[1:2], keepdims=True))
            - jnp.exp(jnp.sum(lp[2:3] * lp[3:4], keepdims=True)) + lam_init)


def _diff_chain(q, k, v, lam, g, lam_init):
    q = q * (64 ** -0.5)
    lane = lax.broadcasted_iota(jnp.int32, q.shape, 1)
    o1 = _softmax_pv(_nt_dot(jnp.where(lane < 64, q, 0.0).astype(BF16), k[...]), v[...])
    o2 = _softmax_pv(_nt_dot(jnp.where(lane >= 64, q, 0.0).astype(BF16), k[...]), v[...])
    o = o1[:, :LANE] / o1[:, LANE:] - o2[:, :LANE] * (lam / o2[:, LANE:])
    return (_rms(o, g) * (1.0 - lam_init)).astype(BF16)


def _diff_kernel(lam_init, q_ref, kx_ref, kc_ref, vx_ref, vc_ref, cos_ref, sin_ref, lam_ref, g_ref,
                 o_ref, kb_ref, vb_ref):
    kb_ref[0:SEQ, :] = _rope(kx_ref[...], cos_ref[...], sin_ref[...]).astype(BF16)
    kb_ref[SEQ:, :] = kc_ref[...].astype(BF16)
    vb_ref[0:SEQ, 0:LANE] = vx_ref[...].astype(BF16)
    vb_ref[SEQ:, 0:LANE] = vc_ref[...].astype(BF16)
    vb_ref[:, LANE:] = jnp.ones((SEQ + CTX_LEN, LANE), BF16)
    lam = _diff_lambda(lam_ref, lam_init)
    for r in range(0, SEQ, QSUB):
        rows = slice(r, r + QSUB)
        q = _rope(q_ref[rows, :], cos_ref[rows, :], sin_ref[rows, :])
        o_ref[rows, :] = _diff_chain(q, kb_ref, vb_ref, lam, g_ref[...], lam_init)


def _diff_ctx_kernel(lam_init, q_ref, k_ref, v_ref, lam_ref, g_ref, o_ref):
    o_ref[...] = _diff_chain(q_ref[...], k_ref[...].astype(BF16),
                             _with_ones(v_ref[...].astype(BF16)),
                             _diff_lambda(lam_ref, lam_init), g_ref[...], lam_init)


def _diff_attn(l, z, cos_t, sin_t, lam_p, subln_g, lam_init, with_ctx, w_f32):
    cq, ck, cv = COL_Q // LANE, COL_K // LANE, COL_V // LANE

    def xs(col):
        return pl.BlockSpec((SEQ, LANE), lambda b, h: (b, col + h))

    def cs(col):
        return pl.BlockSpec((CTX_LEN, LANE), lambda b, h: (CBLK + b, col + h))

    table = pl.BlockSpec((SEQ, LANE), lambda b, h: (0, 0))
    params = [_layer_spec(l, (4, 64)), _layer_spec(l, (1, LANE))]
    return _attn_calls(
        functools.partial(_diff_kernel, lam_init), functools.partial(_diff_ctx_kernel, lam_init),
        "diff_attn",
        [z, z, z, z, z, cos_t, sin_t, lam_p, subln_g],
        [xs(cq), xs(ck), cs(ck), xs(cv), cs(cv), table, table] + params,
        [z, z, z, lam_p, subln_g],
        [cs(cq), cs(ck), cs(cv)] + params,
        [pltpu.VMEM((SEQ + CTX_LEN, LANE), BF16), pltpu.VMEM((SEQ + CTX_LEN, 2 * LANE), BF16)],
        with_ctx, l, w_f32, _cast_rows_pair_chunks)


def _mla_prep_kernel(ckv_ref, cq_ref, kr_ref, gq_ref, gkv_ref, wq_ref, wkv_ref, cos_ref, sin_ref,
                     qn_ref, qr_ref, kn_ref, v_ref, kr2_ref):
    cos = cos_ref[...]
    sin = sin_ref[...]
    q = jnp.dot(_rms(cq_ref[...], gq_ref[...]).astype(BF16), wq_ref[...],
                preferred_element_type=F32)
    qn_ref[...] = q[:, :GROUP_W].astype(BF16)
    for c in range(0, 2 * LANE, LANE):
        qr_ref[:, c:c + LANE] = _rope(q[:, GROUP_W + c:GROUP_W + c + LANE], cos, sin).astype(BF16)
    kv = jnp.dot(_rms(ckv_ref[...], gkv_ref[...]).astype(BF16), wkv_ref[...],
                 preferred_element_type=F32)
    kn_ref[...] = kv[:, :GROUP_W].astype(BF16)
    v_ref[...] = kv[:, GROUP_W:].astype(BF16)
    kr = _rope(kr_ref[...], cos, sin)
    kr2_ref[...] = (kr + pltpu.roll(kr, 64, 1)).astype(BF16)


def _mla_prep(l, z, gq, gkv, wq, wkv, cos_t, sin_t):
    def out(w):
        return jax.ShapeDtypeStruct((NT, w), BF16)

    def row(w):
        return pl.BlockSpec((TM, w), lambda i: (i, 0))

    rope = pl.BlockSpec((TM, LANE), lambda i: (jnp.where(i * TM < NX, i % (SEQ // TM), SEQ // TM), 0))
    return pl.pallas_call(
        _mla_prep_kernel,
        grid=(NT // TM,),
        in_specs=[
            pl.BlockSpec((TM, 256), lambda i: (i, COL_CKV // 256)),
            pl.BlockSpec((TM, 384), lambda i: (i, COL_CQ // 384)),
            pl.BlockSpec((TM, LANE), lambda i: (i, COL_KR // LANE)),
            _layer_spec(l, (1, 384)),
            _layer_spec(l, (1, 256)),
            _layer_spec(l, (384, 768)),
            _layer_spec(l, (256, 1024)),
            rope, rope,
        ],
        out_specs=[row(GROUP_W), row(2 * LANE), row(GROUP_W), row(GROUP_W), row(LANE)],
        out_shape=[out(GROUP_W), out(2 * LANE), out(GROUP_W), out(GROUP_W), out(LANE)],
        compiler_params=_params(("arbitrary",)),
        name="mla_prep",
    )(z, z, z, gq, gkv, wq, wkv, cos_t, sin_t)


def _mla_chain(h, qn, qr, k, v):
    lane = lax.broadcasted_iota(jnp.int32, qr.shape, 1)
    lo = (h % 2) * 64
    qr = jnp.where((lane >= lo) & (lane < lo + 64), qr, jnp.zeros_like(qr))
    s = _nt_dot(jnp.concatenate([qn, qr], axis=-1), k[...]) * ((128 + 64) ** -0.5)
    o = _softmax_pv(s, v[...])
    return (o[:, :LANE] / o[:, LANE:]).astype(BF16)


def _mla_kernel(qn_ref, qr_ref, knx_ref, knc_ref, krx_ref, krc_ref, vx_ref, vc_ref,
                o_ref, kb_ref, vb_ref):
    h = pl.program_id(1)
    kb_ref[0:SEQ, 0:LANE] = knx_ref[...]
    kb_ref[SEQ:, 0:LANE] = knc_ref[...]
    kb_ref[0:SEQ, LANE:] = krx_ref[...]
    kb_ref[SEQ:, LANE:] = krc_ref[...]
    vb_ref[0:SEQ, 0:LANE] = vx_ref[...]
    vb_ref[SEQ:, 0:LANE] = vc_ref[...]
    vb_ref[:, LANE:] = jnp.ones((SEQ + CTX_LEN, LANE), BF16)
    for r in range(0, SEQ, QSUB):
        rows = slice(r, r + QSUB)
        o_ref[rows, :] = _mla_chain(h, qn_ref[rows, :], qr_ref[rows, :], kb_ref, vb_ref)


def _mla_ctx_kernel(qn_ref, qr_ref, kn_ref, kr_ref, v_ref, o_ref):
    k = jnp.concatenate([kn_ref[...], kr_ref[...]], axis=-1)
    o_ref[...] = _mla_chain(pl.program_id(1), qn_ref[...], qr_ref[...], k, _with_ones(v_ref[...]))


def _mla_attn(l, qn, qr, kn, v, kr2, with_ctx, w_f32):
    def xs(col):
        return pl.BlockSpec((SEQ, LANE), lambda b, h: (b, col(h)))

    def cs(col):
        return pl.BlockSpec((CTX_LEN, LANE), lambda b, h: (CBLK + b, col(h)))

    head, pair, first = (lambda h: h), (lambda h: h // 2), (lambda h: 0)
    return _attn_calls(
        _mla_kernel, _mla_ctx_kernel, "mla_attn",
        [qn, qr, kn, kn, kr2, kr2, v, v],
        [xs(head), xs(pair), xs(head), cs(head), xs(first), cs(first), xs(head), cs(head)],
        [qn, qr, kn, kr2, v],
        [cs(head), cs(pair), cs(head), cs(first), cs(head)],
        [pltpu.VMEM((SEQ + CTX_LEN, 2 * LANE), BF16), pltpu.VMEM((SEQ + CTX_LEN, 2 * LANE), BF16)],
        with_ctx, l, w_f32, _cast_rows)


def _out_kernel(split, ya_ref, yb_ref, ym_ref, yd_ref, w_ref, x_ref, *refs):
    c_ref, (mod_ref, gpost_ref, xo_ref, wb_ref) = (refs[0], refs[1:]) if split else (None, refs)

    @pl.when(pl.program_id(0) == 0)
    def _():
        wb_ref[...] = w_ref[...].astype(BF16)

    y = jnp.dot(ya_ref[...], wb_ref[0:GROUP_W, :], preferred_element_type=F32)
    y += jnp.dot(yb_ref[...], wb_ref[GROUP_W:2 * GROUP_W, :], preferred_element_type=F32)
    y += jnp.dot(ym_ref[...], wb_ref[2 * GROUP_W:3 * GROUP_W, :], preferred_element_type=F32)
    y += jnp.dot(yd_ref[...], wb_ref[3 * GROUP_W:, :], preferred_element_type=F32)
    delta = mod_ref[2:3, :] * _rms(y, gpost_ref[...])

    def add_residual(src):
        xo_ref[...] = src[...] + delta

    _pick_rows(pl.program_id(0), TM, x_ref, c_ref, add_residual)


def _out_proj(l, ya, yb, ym, yd, w, toks, mods, g_post):
    n = ya.shape[0]
    split = len(toks) == 2
    y_spec = pl.BlockSpec((TM, GROUP_W), lambda i: (i, 0))
    row = pl.BlockSpec((TM, D_MODEL), lambda i: (i, 0))
    return pl.pallas_call(
        functools.partial(_out_kernel, split),
        grid=(n // TM,),
        in_specs=[
            y_spec, y_spec, y_spec, y_spec,
            pl.BlockSpec((None, D_MODEL, D_MODEL), lambda i: (l, 0, 0),
                         pipeline_mode=pl.Buffered(1)),
        ] + (_split_rows_specs(TM, NX) if split else [row]) + [
            _mod_spec(l, TM),
            _layer_spec(l, (1, D_MODEL)),
        ],
        out_specs=row,
        out_shape=jax.ShapeDtypeStruct((n, D_MODEL), F32),
        scratch_shapes=[pltpu.VMEM((D_MODEL, D_MODEL), BF16)],
        compiler_params=_params(("arbitrary",)),
        name="out_proj",
    )(ya, yb, ym, yd, w, *toks, mods, g_post)


def _ffn_kernel(x_ref, xp_ref, xn_ref, mod_ref, gpre_ref, gpost_ref, wgv_ref, cg_ref, cv_ref,
                wd_ref, xo_ref, hx_ref, r_ref):
    i = pl.program_id(0)
    j = pl.program_id(1)
    n = TM_BIG

    @pl.when(j == 0)
    def _():
        slab_shape = (BF16_ROWS, D_MODEL)
        g = jnp.broadcast_to(gpre_ref[...], slab_shape)
        scale = jnp.broadcast_to(1.0 + mod_ref[4:5, :], slab_shape)
        shift = jnp.broadcast_to(mod_ref[3:4, :], slab_shape)

        def hmod(x, r):
            return x * r * g * scale + shift

        def rinv(x):
            return lax.rsqrt(jnp.mean(x * x, axis=-1, keepdims=True) + NORM_EPS)

        _store_rinv(r_ref, x_ref[...])

        def slab(k, carry):
            rows = pl.ds(pl.multiple_of(k * BF16_ROWS, BF16_ROWS), BF16_ROWS)
            hx_ref[rows, :] = hmod(x_ref[rows, :], _load_rinv(r_ref, rows)).astype(BF16)
            return carry

        lax.fori_loop(0, n // BF16_ROWS, slab, 0, unroll=4)
        has_prev, has_next = _tile_edges(i, n)
        halo = jnp.concatenate([xp_ref[...], xn_ref[...]], axis=0)
        keep = jnp.where(lax.broadcasted_iota(jnp.int32, (BF16_ROWS, 1), 0) < SUBLANE,
                         has_prev, has_next)
        hx_ref[n:, :] = (hmod(halo, rinv(halo)) * keep).astype(BF16)
        xo_ref[...] = jnp.zeros_like(xo_ref)

    is_ctx = i * n >= NX
    hx = hx_ref[...]

    u = jnp.dot(hx, wgv_ref[...], preferred_element_type=F32)

    def conv(cols, c_ref):
        return _conv3_rows(u[:n, cols], u[n + SUBLANE - 1:n + SUBLANE, cols],
                           u[n + SUBLANE:n + SUBLANE + 1, cols], c_ref, is_ctx)

    gate = conv(slice(0, TF), cg_ref)
    val = conv(slice(TF, 2 * TF), cv_ref)
    act = (gate * jax.nn.sigmoid(gate) * val).astype(BF16)
    xo_ref[...] += jnp.dot(act, wd_ref[...], preferred_element_type=F32)

    @pl.when(j == pl.num_programs(1) - 1)
    def _():
        _store_rinv(r_ref, xo_ref[...])
        slab_shape = (BF16_ROWS, D_MODEL)
        g = jnp.broadcast_to(gpost_ref[...], slab_shape)
        res_gate = jnp.broadcast_to(mod_ref[5:6, :], slab_shape)

        def slab(k, carry):
            rows = pl.ds(pl.multiple_of(k * BF16_ROWS, BF16_ROWS), BF16_ROWS)
            normed = xo_ref[rows, :] * _load_rinv(r_ref, rows) * g
            xo_ref[rows, :] = x_ref[rows, :] + res_gate * normed
            return carry

        lax.fori_loop(0, n // BF16_ROWS, slab, 0, unroll=4)


def _ffn(l, tok, mods, g_pre, g_post, w_up, conv_w, w_down, n_rows):
    nj = D_FF // TF
    return pl.pallas_call(
        _ffn_kernel,
        grid=(n_rows // TM_BIG, nj),
        in_specs=(_halo_specs(TM_BIG, D_MODEL, n_rows, 0) + [
            _mod_spec(l, TM_BIG),
            _layer_spec(l, (1, D_MODEL)),
            _layer_spec(l, (1, D_MODEL)),
            pl.BlockSpec((D_MODEL, 2 * TF), lambda i, j: (0, j)),
            pl.BlockSpec((None, 3, TF), lambda i, j: (l, 0, j)),
            pl.BlockSpec((None, 3, TF), lambda i, j: (l, 0, j + nj)),
            pl.BlockSpec((TF, D_MODEL), lambda i, j: (j, 0)),
        ]),
        out_specs=pl.BlockSpec((TM_BIG, D_MODEL), lambda i, j: (i, 0)),
        out_shape=jax.ShapeDtypeStruct((n_rows, D_MODEL), F32),
        scratch_shapes=[pltpu.VMEM((TM_BIG + BF16_ROWS, D_MODEL), BF16),
                        pltpu.VMEM((TM_BIG, LANE), F32)],
        compiler_params=_params(("arbitrary", "arbitrary"), VMEM_LIMIT_BIG),
        name="ffn",
    )(tok, tok, tok, mods, g_pre, g_post, w_up, conv_w, conv_w, w_down)


def _rope_tables():
    rows = SEQ // GRID_W
    row = jnp.repeat(jnp.arange(rows), GRID_W).astype(F32)
    col = jnp.tile(jnp.arange(GRID_W), rows).astype(F32)
    n_freq = 64 // 4
    inv = ROPE_THETA ** (-jnp.arange(n_freq, dtype=F32) / n_freq)
    ang = jnp.concatenate([row[:, None] * inv, col[:, None] * inv], axis=-1)
    cos, sin = jnp.cos(ang), jnp.sin(ang)
    cos_t = jnp.concatenate([cos, cos, cos, cos], axis=-1)
    sin_t = jnp.concatenate([-sin, sin, -sin, sin], axis=-1)
    cos_t = jnp.concatenate([cos_t, jnp.ones((TM, LANE), F32)], axis=0)
    sin_t = jnp.concatenate([sin_t, jnp.zeros((TM, LANE), F32)], axis=0)
    return cos_t, sin_t


W_IN_KB = 256


def _w_in_kernel(w_ref, o_ref):
    for dst, src, n in ((0, 0, 2048), (COL_DB, 2752, 1536), (COL_CKV, 2432, 256),
                        (COL_CQ, 2048, 384), (COL_KR, 2688, 64)):
        o_ref[dst:dst + n, :] = w_ref[src:src + n, :].astype(BF16)
    o_ref[COL_KR + 64:, :] = jnp.zeros((IN_COLS_PAD - COL_KR - 64, W_IN_KB), BF16)


def _layout_w_in(w):
    wt = jnp.swapaxes(w, 1, 2)
    n, k = wt.shape[1:]
    return pl.pallas_call(
        _w_in_kernel,
        grid=(DEPTH, k // W_IN_KB),
        in_specs=[pl.BlockSpec((None, n, W_IN_KB), lambda l, i: (l, 0, i))],
        out_specs=pl.BlockSpec((None, IN_COLS_PAD, W_IN_KB), lambda l, i: (l, 0, i)),
        out_shape=jax.ShapeDtypeStruct((DEPTH, IN_COLS_PAD, k), BF16),
        compiler_params=_params(("arbitrary", "arbitrary")),
        name="w_in_layout",
    )(wt)


def _layout_w_uq(w):
    w = w.reshape(DEPTH, 384, HEADS, 192)
    return jnp.concatenate([w[..., :128].reshape(DEPTH, 384, 512),
                            w[..., 128:].reshape(DEPTH, 384, 256)], axis=-1).astype(BF16)


def _layout_w_ukv(w):
    w = w.reshape(DEPTH, 256, HEADS, 256)
    return jnp.concatenate([w[..., :128].reshape(DEPTH, 256, 512),
                            w[..., 128:].reshape(DEPTH, 256, 512)], axis=-1).astype(BF16)


def kernel(x, c, ctx, c_ctx, ada_w, ada_b, g_pre_mix, g_post_mix, g_pre_ffn, g_post_ffn, w_in,
           pool_w, pool_scale, diff_lambda, diff_subln_g, mla_gq, mla_w_uq, mla_gkv, mla_w_ukv,
           conv_w, w_out, ffn_w_up, ffn_conv_w, ffn_w_down):
    toks = (x.reshape(NX, D_MODEL), ctx.reshape(NC, D_MODEL))
    cc = jnp.concatenate([c, c_ctx[None, :], jnp.zeros((3, D_MODEL), F32)], axis=0)
    mods = _ada(cc, ada_w, ada_b).reshape(DEPTH, 8, 6, D_MODEL)
    cos_t, sin_t = _rope_tables()

    def vec(p):
        return p.reshape(DEPTH, 1, p.shape[-1])

    w_in_b = _layout_w_in(w_in)
    w_uq_b, w_ukv_b = _layout_w_uq(mla_w_uq), _layout_w_ukv(mla_w_ukv)
    pool_w_b = pool_w.astype(BF16)

    for l in range(DEPTH):
        last = l == DEPTH - 1
        n_rows = NX if last else NT
        lam_init = 0.8 - 0.6 * math.exp(-0.3 * l)

        z = _in_proj(l, toks, mods, vec(g_pre_mix), w_in_b)
        ya, yd = _pool_conv(l, z, n_rows, pool_w_b, vec(pool_scale), conv_w)
        yb, w_up_b = _diff_attn(l, z, cos_t, sin_t, diff_lambda, vec(diff_subln_g), lam_init,
                                not last, ffn_w_up)
        qn, qr, kn, v, kr2 = _mla_prep(l, z, vec(mla_gq), vec(mla_gkv), w_uq_b, w_ukv_b, cos_t, sin_t)
        ym, w_down_b = _mla_attn(l, qn, qr, kn, v, kr2, not last, ffn_w_down)
        tok = _out_proj(l, ya, yb, ym, yd, w_out, toks, mods, vec(g_post_mix))
        tok = _ffn(l, tok, mods, vec(g_pre_ffn), vec(g_post_ffn), w_up_b, ffn_conv_w, w_down_b, n_rows)
        toks = (tok,)
    return tok.reshape(BATCH, SEQ, D_MODEL)
```

```python
import functools
import math

import jax
import jax.numpy as jnp
from jax import lax
from jax.experimental import pallas as pl
from jax.experimental.pallas import tpu as pltpu

F32 = jnp.float32
BF16 = jnp.bfloat16

D_MODEL = 2048
BATCH = 4
SEQ = 2048
CTX_LEN = 256
GRID_W = 64
DEPTH = 2
NORM_EPS = 1e-6
ROPE_THETA = 10000.0
GROUP_W = 512
POOL_WINDOWS = (2, 4, 8, 16)
HEADS = 4
D_FF = 5632

NX = BATCH * SEQ
NC = BATCH * CTX_LEN
NT = NX + NC

COL_A, COL_Q, COL_K, COL_V = 0, 512, 1024, 1536
COL_DB, COL_DC, COL_DH = 2048, 2560, 3072
COL_CKV, COL_CQ, COL_KR = 3584, 3840, 4224
IN_COLS_PAD = 4352

LANE = 128
SUBLANE = 8
BF16_ROWS = 2 * SUBLANE
TM = 512
TM_BIG = 1024
TS = 256
QSUB = 256
TF = 512
TN_ADA = 1024
VMEM_LIMIT = 56 * 1024 * 1024
VMEM_LIMIT_BIG = 62 * 1024 * 1024


def _params(sem, vmem_limit=VMEM_LIMIT):
    return pltpu.CompilerParams(dimension_semantics=sem, vmem_limit_bytes=vmem_limit)


def _split_rows_specs(rows, n_lat_rows):
    n_lat = n_lat_rows // rows
    return [pl.BlockSpec((rows, D_MODEL), lambda i, *_: (jnp.minimum(i, n_lat - 1), 0)),
            pl.BlockSpec((rows, D_MODEL), lambda i, *_: (jnp.maximum(i - n_lat, 0), 0),
                         pipeline_mode=pl.Buffered(1))]


def _pick_rows(i, rows, x_ref, c_ref, fn):
    if c_ref is None:
        fn(x_ref)
        return

    @pl.when(i * rows < NX)
    def _():
        fn(x_ref)

    @pl.when(i * rows >= NX)
    def _():
        fn(c_ref)


def _rms(x, g):
    ms = jnp.mean(x * x, axis=-1, keepdims=True)
    return x * lax.rsqrt(ms + NORM_EPS) * g


def _store_rinv(r_ref, x):
    r = lax.rsqrt(jnp.mean(x * x, axis=-1, keepdims=True) + NORM_EPS)
    r_ref[...] = jnp.broadcast_to(r, r_ref.shape)


def _load_rinv(r_ref, rows):
    return jnp.tile(r_ref[rows, :], (1, D_MODEL // LANE))


def _nt_dot(a, b):
    return lax.dot_general(a, b, (((1,), (1,)), ((), ())), preferred_element_type=F32)


def _rope(v, cos, sin):
    lane = lax.broadcasted_iota(jnp.int32, v.shape, 1)
    sw = jnp.where((lane & 32) == 0, pltpu.roll(v, 96, 1), pltpu.roll(v, 32, 1))
    return v * cos + sw * sin


def _seq_pos(tile, rows):
    r = tile * rows + lax.broadcasted_iota(jnp.int32, (rows, 1), 0)
    seqlen = jnp.where(r < NX, SEQ, CTX_LEN)
    return r & (seqlen - 1), seqlen


def _tile_edges(tile, rows):
    r0 = tile * rows
    seqlen = jnp.where(r0 < NX, SEQ, CTX_LEN)
    has_prev = jnp.where((r0 & (seqlen - 1)) == 0, 0.0, 1.0)
    has_next = jnp.where(((r0 + rows) & (seqlen - 1)) == 0, 0.0, 1.0)
    return has_prev, has_next


def _conv3_rows(u, prev_row, next_row, c_ref, is_ctx):
    n = u.shape[0]
    row8 = lax.broadcasted_iota(jnp.int32, (SUBLANE, 1), 0)
    prev = pltpu.roll(u, 1, 0)
    prev = jnp.concatenate([jnp.where(row8 == 0, prev_row, prev[:SUBLANE]), prev[SUBLANE:]], axis=0)
    nxt = pltpu.roll(u, n - 1, 0)
    nxt = jnp.concatenate([nxt[:n - SUBLANE],
                           jnp.where(row8 == SUBLANE - 1, next_row, nxt[n - SUBLANE:])], axis=0)
    ctx_row = row8 + jnp.where(is_ctx, 0, 2 * SUBLANE)
    for p in range(CTX_LEN, n, CTX_LEN):
        prev = jnp.concatenate([prev[:p], jnp.where(ctx_row == 0, 0.0, prev[p:p + SUBLANE]),
                                prev[p + SUBLANE:]], axis=0)
        nxt = jnp.concatenate([nxt[:p - SUBLANE],
                               jnp.where(ctx_row == SUBLANE - 1, 0.0, nxt[p - SUBLANE:p]), nxt[p:]],
                              axis=0)
    return c_ref[0:1, :] * prev + c_ref[1:2, :] * u + c_ref[2:3, :] * nxt


def _first_axis(i, *_):
    return i


def _halo_specs(rows, width, n_rows, col, tile=_first_axis):
    r8 = rows // SUBLANE
    last8 = n_rows // SUBLANE - 1
    return [
        pl.BlockSpec((rows, width), lambda *g: (tile(*g), col)),
        pl.BlockSpec((SUBLANE, width), lambda *g: (jnp.maximum(tile(*g) * r8 - 1, 0), col)),
        pl.BlockSpec((SUBLANE, width), lambda *g: (jnp.minimum((tile(*g) + 1) * r8, last8), col)),
    ]


def _layer_spec(l, shape):
    return pl.BlockSpec((None,) + shape, lambda *_: (l,) + (0,) * len(shape))


def _mod_spec(l, rows, tile=_first_axis):
    def index(*g):
        r = tile(*g) * rows
        return (l, jnp.where(r < NX, r // SEQ, BATCH), 0, 0)
    return pl.BlockSpec((None, None, 6, D_MODEL), index)


def _ada_kernel(c_ref, w_ref, b_ref, o_ref):
    c = c_ref[...]
    a = c * jax.nn.sigmoid(c)
    o_ref[...] = jnp.dot(a.astype(BF16), w_ref[...].astype(BF16),
                         preferred_element_type=F32) + b_ref[...]


def _ada(cc, ada_w, ada_b):
    n = ada_w.shape[-1]
    return pl.pallas_call(
        _ada_kernel,
        grid=(DEPTH, n // TN_ADA),
        in_specs=[
            pl.BlockSpec((8, D_MODEL), lambda l, j: (0, 0)),
            pl.BlockSpec((None, D_MODEL, TN_ADA), lambda l, j: (l, 0, j)),
            pl.BlockSpec((None, 1, TN_ADA), lambda l, j: (l, 0, j)),
        ],
        out_specs=pl.BlockSpec((None, 8, TN_ADA), lambda l, j: (l, 0, j)),
        out_shape=jax.ShapeDtypeStruct((DEPTH, 8, n), F32),
        compiler_params=_params(("arbitrary", "arbitrary")),
        name="ada",
    )(cc, ada_w, ada_b.reshape(DEPTH, 1, n))


def _in_kernel(split, x_ref, *refs):
    c_ref, (mod_ref, g_ref, w_ref, z_ref, h_ref, r_ref) = ((refs[0], refs[1:]) if split
                                                              else (None, refs))

    def put(src):
        _store_rinv(r_ref, src[...])

        shape = (BF16_ROWS, D_MODEL)
        g = jnp.broadcast_to(g_ref[...], shape)
        scale = jnp.broadcast_to(1.0 + mod_ref[1:2, :], shape)
        shift = jnp.broadcast_to(mod_ref[0:1, :], shape)

        def slab(k, carry):
            rows = pl.ds(pl.multiple_of(k * BF16_ROWS, BF16_ROWS), BF16_ROWS)
            y = src[rows, :] * _load_rinv(r_ref, rows) * g
            h_ref[rows, :] = (y * scale + shift).astype(BF16)
            return carry

        lax.fori_loop(0, TM // BF16_ROWS, slab, 0, unroll=4)

    _pick_rows(pl.program_id(0), TM, x_ref, c_ref, put)
    z_ref[...] = _nt_dot(h_ref[...], w_ref[...])


def _in_proj(l, toks, mods, g, w):
    split = len(toks) == 2
    rows = (_split_rows_specs(TM, NX) if split
            else [pl.BlockSpec((TM, D_MODEL), lambda i: (i, 0))])
    return pl.pallas_call(
        functools.partial(_in_kernel, split),
        grid=(NT // TM,),
        in_specs=rows + [
            _mod_spec(l, TM),
            _layer_spec(l, (1, D_MODEL)),
            pl.BlockSpec((None, IN_COLS_PAD, D_MODEL), lambda i: (l, 0, 0),
                         pipeline_mode=pl.Buffered(1)),
        ],
        out_specs=pl.BlockSpec((TM, IN_COLS_PAD), lambda i: (i, 0)),
        out_shape=jax.ShapeDtypeStruct((NT, IN_COLS_PAD), F32),
        scratch_shapes=[pltpu.VMEM((TM, D_MODEL), BF16), pltpu.VMEM((TM, LANE), F32)],
        compiler_params=_params(("arbitrary",), VMEM_LIMIT_BIG),
        name="in_proj",
    )(*toks, mods, g, w)


def _pool_conv_kernel(a_ref, ap_ref, an_ref, db_ref, dc_ref, dcp_ref, dcn_ref,
                      dh_ref, dhp_ref, dhn_ref, pw_ref, ps_ref, cw_ref,
                      ya_ref, yd_ref, e_ref):
    i = pl.program_id(0)
    pos, seqlen = _seq_pos(i, TS)
    pf, nf = _tile_edges(i, TS)

    e_ref[0:8, :] = ap_ref[...] * pf
    e_ref[8:8 + TS, :] = a_ref[...]
    e_ref[8 + TS:16 + TS, :] = an_ref[...] * nf
    for g, w in enumerate(POOL_WINDOWS):
        half = w // 2
        cs = slice(g * LANE, (g + 1) * LANE)
        acc = e_ref[8 - half:8 - half + TS, cs]
        for k in range(1, w):
            acc = acc + e_ref[8 - half + k:8 - half + k + TS, cs]
        cnt = (jnp.minimum(pos + half, seqlen) - jnp.maximum(pos - half, 0)).astype(F32)
        pooled = acc * (1.0 / cnt) - a_ref[:, cs]
        y = jnp.dot(pooled.astype(BF16), pw_ref[g], preferred_element_type=F32)
        ya_ref[:, cs] = (y * ps_ref[:, cs]).astype(BF16)

    p_prev = dcp_ref[SUBLANE - 1:SUBLANE, :] * dhp_ref[SUBLANE - 1:SUBLANE, :] * pf
    p_next = dcn_ref[0:1, :] * dhn_ref[0:1, :] * nf
    conv = _conv3_rows(dc_ref[...] * dh_ref[...], p_prev, p_next, cw_ref, False)
    yd_ref[...] = (db_ref[...] * conv).astype(BF16)


def _pool_conv(l, z, n_rows, pool_w, pool_scale, conv_w):
    ca, cb, cc, ch = (COL_A // GROUP_W, COL_DB // GROUP_W, COL_DC // GROUP_W, COL_DH // GROUP_W)
    out = jax.ShapeDtypeStruct((n_rows, GROUP_W), BF16)
    return pl.pallas_call(
        _pool_conv_kernel,
        grid=(n_rows // TS,),
        in_specs=(_halo_specs(TS, GROUP_W, NT, ca)
                  + [pl.BlockSpec((TS, GROUP_W), lambda i: (i, cb))]
                  + _halo_specs(TS, GROUP_W, NT, cc) + _halo_specs(TS, GROUP_W, NT, ch)
                  + [_layer_spec(l, (len(POOL_WINDOWS), LANE, LANE)),
                     _layer_spec(l, (1, GROUP_W)), _layer_spec(l, (3, GROUP_W))]),
        out_specs=[pl.BlockSpec((TS, GROUP_W), lambda i: (i, 0))] * 2,
        out_shape=[out, out],
        scratch_shapes=[pltpu.VMEM((TS + 16, GROUP_W), F32)],
        compiler_params=_params(("arbitrary",)),
        name="pool_conv",
    )(z, z, z, z, z, z, z, z, z, z, pool_w, pool_scale, conv_w)


CBLK = NX // CTX_LEN


def _softmax_pv(s, v):
    e = jnp.exp(s - jnp.max(s, axis=-1, keepdims=True)).astype(BF16)
    return jnp.dot(e, v, preferred_element_type=F32)


def _with_ones(v):
    return jnp.concatenate([v, jnp.ones(v.shape, v.dtype)], axis=-1)


def _cast_rows(w_ref, wb_ref):
    wb_ref[...] = w_ref[...].astype(BF16)


def _cast_rows_pair_chunks(w_ref, wb_ref):
    for j in range(D_FF // TF):
        for half in range(2):
            src = half * D_FF + j * TF
            dst = (2 * j + half) * TF
            wb_ref[:, dst:dst + TF] = w_ref[:, src:src + TF].astype(BF16)


def _attn_calls(lat_fn, ctx_fn, name, lat_ins, lat_specs, ctx_ins, ctx_specs, scratch, with_ctx,
                l, w_f32, cast_fn):
    steps = BATCH * HEADS
    w_rows, w_cols = w_f32.shape[1] // steps, w_f32.shape[2]

    def lat_body(*refs):
        n = len(lat_ins)
        w_ref, (o_ref, wb_ref), rest = refs[n], refs[-len(scratch) - 2:-len(scratch)], refs[-len(scratch):]
        cast_fn(w_ref, wb_ref)
        lat_fn(*refs[:n], o_ref, *rest)

    y0 = [jnp.zeros((NT, GROUP_W), BF16)] if with_ctx else []
    y_shape = jax.ShapeDtypeStruct((NT if with_ctx else NX, GROUP_W), BF16)
    y, w_b = pl.pallas_call(
        lat_body,
        grid=(BATCH, HEADS),
        in_specs=(lat_specs
                  + [pl.BlockSpec((None, w_rows, w_cols), lambda b, h: (l, b * HEADS + h, 0))]
                  + [pl.BlockSpec(memory_space=pl.ANY)] * len(y0)),
        out_specs=[pl.BlockSpec((SEQ, LANE), lambda b, h: (b, h)),
                   pl.BlockSpec((w_rows, w_cols), lambda b, h: (b * HEADS + h, 0))],
        out_shape=[y_shape, jax.ShapeDtypeStruct(w_f32.shape[1:], BF16)],
        input_output_aliases={len(lat_ins) + 1: 0} if with_ctx else {},
        scratch_shapes=scratch,
        compiler_params=_params(("arbitrary", "arbitrary")),
        name=name,
    )(*lat_ins, w_f32, *y0)
    if not with_ctx:
        return y, w_b
    y = pl.pallas_call(
        lambda *refs: ctx_fn(*refs[:len(ctx_ins)], refs[-1]),
        grid=(BATCH, HEADS),
        in_specs=ctx_specs + [pl.BlockSpec(memory_space=pl.ANY)],
        out_specs=pl.BlockSpec((CTX_LEN, LANE), lambda b, h: (CBLK + b, h)),
        out_shape=y_shape,
        input_output_aliases={len(ctx_ins): 0},
        compiler_params=_params(("arbitrary", "arbitrary")),
        name=name + "_ctx",
    )(*ctx_ins, y)
    return y, w_b


def _diff_lambda(lam_ref, lam_init):
    lp = lam_ref[...]
    return (jnp.exp(jnp.sum(lp[0:1] * lp[1:2], keepdims=True))
            - jnp.exp(jnp.sum(lp[2:3] * lp[3:4], keepdims=True)) + lam_init)


def _diff_chain(q, k, v, lam, g, lam_init):
    q = q * (64 ** -0.5)
    lane = lax.broadcasted_iota(jnp.int32, q.shape, 1)
    o1 = _softmax_pv(_nt_dot(jnp.where(lane < 64, q, 0.0).astype(BF16), k[...]), v[...])
    o2 = _softmax_pv(_nt_dot(jnp.where(lane >= 64, q, 0.0).astype(BF16), k[...]), v[...])
    o = o1[:, :LANE] / o1[:, LANE:] - o2[:, :LANE] * (lam / o2[:, LANE:])
    return (_rms(o, g) * (1.0 - lam_init)).astype(BF16)


def _diff_kernel(lam_init, q_ref, kx_ref, kc_ref, vx_ref, vc_ref, cos_ref, sin_ref, lam_ref, g_ref,
                 o_ref, kb_ref, vb_ref):
    kb_ref[0:SEQ, :] = _rope(kx_ref[...], cos_ref[...], sin_ref[...]).astype(BF16)
    kb_ref[SEQ:, :] = kc_ref[...].astype(BF16)
    vb_ref[0:SEQ, 0:LANE] = vx_ref[...].astype(BF16)
    vb_ref[SEQ:, 0:LANE] = vc_ref[...].astype(BF16)
    vb_ref[:, LANE:] = jnp.ones((SEQ + CTX_LEN, LANE), BF16)
    lam = _diff_lambda(lam_ref, lam_init)
    for r in range(0, SEQ, QSUB):
        rows = slice(r, r + QSUB)
        q = _rope(q_ref[rows, :], cos_ref[rows, :], sin_ref[rows, :])
        o_ref[rows, :] = _diff_chain(q, kb_ref, vb_ref, lam, g_ref[...], lam_init)


def _diff_ctx_kernel(lam_init, q_ref, k_ref, v_ref, lam_ref, g_ref, o_ref):
    o_ref[...] = _diff_chain(q_ref[...], k_ref[...].astype(BF16),
                             _with_ones(v_ref[...].astype(BF16)),
                             _diff_lambda(lam_ref, lam_init), g_ref[...], lam_init)


def _diff_attn(l, z, cos_t, sin_t, lam_p, subln_g, lam_init, with_ctx, w_f32):
    cq, ck, cv = COL_Q // LANE, COL_K // LANE, COL_V // LANE

    def xs(col):
        return pl.BlockSpec((SEQ, LANE), lambda b, h: (b, col + h))

    def cs(col):
        return pl.BlockSpec((CTX_LEN, LANE), lambda b, h: (CBLK + b, col + h))

    table = pl.BlockSpec((SEQ, LANE), lambda b, h: (0, 0))
    params = [_layer_spec(l, (4, 64)), _layer_spec(l, (1, LANE))]
    return _attn_calls(
        functools.partial(_diff_kernel, lam_init), functools.partial(_diff_ctx_kernel, lam_init),
        "diff_attn",
        [z, z, z, z, z, cos_t, sin_t, lam_p, subln_g],
        [xs(cq), xs(ck), cs(ck), xs(cv), cs(cv), table, table] + params,
        [z, z, z, lam_p, subln_g],
        [cs(cq), cs(ck), cs(cv)] + params,
        [pltpu.VMEM((SEQ + CTX_LEN, LANE), BF16), pltpu.VMEM((SEQ + CTX_LEN, 2 * LANE), BF16)],
        with_ctx, l, w_f32, _cast_rows_pair_chunks)


def _mla_prep_kernel(ckv_ref, cq_ref, kr_ref, gq_ref, gkv_ref, wq_ref, wkv_ref, cos_ref, sin_ref,
                     qn_ref, qr_ref, kn_ref, v_ref, kr2_ref):
    cos = cos_ref[...]
    sin = sin_ref[...]
    q = jnp.dot(_rms(cq_ref[...], gq_ref[...]).astype(BF16), wq_ref[...],
                preferred_element_type=F32)
    qn_ref[...] = q[:, :GROUP_W].astype(BF16)
    for c in range(0, 2 * LANE, LANE):
        qr_ref[:, c:c + LANE] = _rope(q[:, GROUP_W + c:GROUP_W + c + LANE], cos, sin).astype(BF16)
    kv = jnp.dot(_rms(ckv_ref[...], gkv_ref[...]).astype(BF16), wkv_ref[...],
                 preferred_element_type=F32)
    kn_ref[...] = kv[:, :GROUP_W].astype(BF16)
    v_ref[...] = kv[:, GROUP_W:].astype(BF16)
    kr = _rope(kr_ref[...], cos, sin)
    kr2_ref[...] = (kr + pltpu.roll(kr, 64, 1)).astype(BF16)


def _mla_prep(l, z, gq, gkv, wq, wkv, cos_t, sin_t):
    def out(w):
        return jax.ShapeDtypeStruct((NT, w), BF16)

    def row(w):
        return pl.BlockSpec((TM, w), lambda i: (i, 0))

    rope = pl.BlockSpec((TM, LANE), lambda i: (jnp.where(i * TM < NX, i % (SEQ // TM), SEQ // TM), 0))
    return pl.pallas_call(
        _mla_prep_kernel,
        grid=(NT // TM,),
        in_specs=[
            pl.BlockSpec((TM, 256), lambda i: (i, COL_CKV // 256)),
            pl.BlockSpec((TM, 384), lambda i: (i, COL_CQ // 384)),
            pl.BlockSpec((TM, LANE), lambda i: (i, COL_KR // LANE)),
            _layer_spec(l, (1, 384)),
            _layer_spec(l, (1, 256)),
            _layer_spec(l, (384, 768)),
            _layer_spec(l, (256, 1024)),
            rope, rope,
        ],
        out_specs=[row(GROUP_W), row(2 * LANE), row(GROUP_W), row(GROUP_W), row(LANE)],
        out_shape=[out(GROUP_W), out(2 * LANE), out(GROUP_W), out(GROUP_W), out(LANE)],
        compiler_params=_params(("arbitrary",)),
        name="mla_prep",
    )(z, z, z, gq, gkv, wq, wkv, cos_t, sin_t)


def _mla_chain(h, qn, qr, k, v):
    lane = lax.broadcasted_iota(jnp.int32, qr.shape, 1)
    lo = (h % 2) * 64
    qr = jnp.where((lane >= lo) & (lane < lo + 64), qr, jnp.zeros_like(qr))
    s = _nt_dot(jnp.concatenate([qn, qr], axis=-1), k[...]) * ((128 + 64) ** -0.5)
    o = _softmax_pv(s, v[...])
    return (o[:, :LANE] / o[:, LANE:]).astype(BF16)


def _mla_kernel(qn_ref, qr_ref, knx_ref, knc_ref, krx_ref, krc_ref, vx_ref, vc_ref,
                o_ref, kb_ref, vb_ref):
    h = pl.program_id(1)
    kb_ref[0:SEQ, 0:LANE] = knx_ref[...]
    kb_ref[SEQ:, 0:LANE] = knc_ref[...]
    kb_ref[0:SEQ, LANE:] = krx_ref[...]
    kb_ref[SEQ:, LANE:] = krc_ref[...]
    vb_ref[0:SEQ, 0:LANE] = vx_ref[...]
    vb_ref[SEQ:, 0:LANE] = vc_ref[...]
    vb_ref[:, LANE:] = jnp.ones((SEQ + CTX_LEN, LANE), BF16)
    for r in range(0, SEQ, QSUB):
        rows = slice(r, r + QSUB)
        o_ref[rows, :] = _mla_chain(h, qn_ref[rows, :], qr_ref[rows, :], kb_ref, vb_ref)


def _mla_ctx_kernel(qn_ref, qr_ref, kn_ref, kr_ref, v_ref, o_ref):
    k = jnp.concatenate([kn_ref[...], kr_ref[...]], axis=-1)
    o_ref[...] = _mla_chain(pl.program_id(1), qn_ref[...], qr_ref[...], k, _with_ones(v_ref[...]))


def _mla_attn(l, qn, qr, kn, v, kr2, with_ctx, w_f32):
    def xs(col):
        return pl.BlockSpec((SEQ, LANE), lambda b, h: (b, col(h)))

    def cs(col):
        return pl.BlockSpec((CTX_LEN, LANE), lambda b, h: (CBLK + b, col(h)))

    head, pair, first = (lambda h: h), (lambda h: h // 2), (lambda h: 0)
    return _attn_calls(
        _mla_kernel, _mla_ctx_kernel, "mla_attn",
        [qn, qr, kn, kn, kr2, kr2, v, v],
        [xs(head), xs(pair), xs(head), cs(head), xs(first), cs(first), xs(head), cs(head)],
        [qn, qr, kn, kr2, v],
        [cs(head), cs(pair), cs(head), cs(first), cs(head)],
        [pltpu.VMEM((SEQ + CTX_LEN, 2 * LANE), BF16), pltpu.VMEM((SEQ + CTX_LEN, 2 * LANE), BF16)],
        with_ctx, l, w_f32, _cast_rows)


def _out_kernel(split, ya_ref, yb_ref, ym_ref, yd_ref, w_ref, x_ref, *refs):
    c_ref, (mod_ref, gpost_ref, xo_ref, wb_ref) = (refs[0], refs[1:]) if split else (None, refs)

    @pl.when(pl.program_id(0) == 0)
    def _():
        wb_ref[...] = w_ref[...].astype(BF16)

    y = jnp.dot(ya_ref[...], wb_ref[0:GROUP_W, :], preferred_element_type=F32)
    y += jnp.dot(yb_ref[...], wb_ref[GROUP_W:2 * GROUP_W, :], preferred_element_type=F32)
    y += jnp.dot(ym_ref[...], wb_ref[2 * GROUP_W:3 * GROUP_W, :], preferred_element_type=F32)
    y += jnp.dot(yd_ref[...], wb_ref[3 * GROUP_W:, :], preferred_element_type=F32)
    delta = mod_ref[2:3, :] * _rms(y, gpost_ref[...])

    def add_residual(src):
        xo_ref[...] = src[...] + delta

    _pick_rows(pl.program_id(0), TM, x_ref, c_ref, add_residual)


def _out_proj(l, ya, yb, ym, yd, w, toks, mods, g_post):
    n = ya.shape[0]
    split = len(toks) == 2
    y_spec = pl.BlockSpec((TM, GROUP_W), lambda i: (i, 0))
    row = pl.BlockSpec((TM, D_MODEL), lambda i: (i, 0))
    return pl.pallas_call(
        functools.partial(_out_kernel, split),
        grid=(n // TM,),
        in_specs=[
            y_spec, y_spec, y_spec, y_spec,
            pl.BlockSpec((None, D_MODEL, D_MODEL), lambda i: (l, 0, 0),
                         pipeline_mode=pl.Buffered(1)),
        ] + (_split_rows_specs(TM, NX) if split else [row]) + [
            _mod_spec(l, TM),
            _layer_spec(l, (1, D_MODEL)),
        ],
        out_specs=row,
        out_shape=jax.ShapeDtypeStruct((n, D_MODEL), F32),
        scratch_shapes=[pltpu.VMEM((D_MODEL, D_MODEL), BF16)],
        compiler_params=_params(("arbitrary",)),
        name="out_proj",
    )(ya, yb, ym, yd, w, *toks, mods, g_post)


def _ffn_kernel(x_ref, xp_ref, xn_ref, mod_ref, gpre_ref, gpost_ref, wgv_ref, cg_ref, cv_ref,
                wd_ref, xo_ref, hx_ref, r_ref):
    i = pl.program_id(0)
    j = pl.program_id(1)
    n = TM_BIG

    @pl.when(j == 0)
    def _():
        slab_shape = (BF16_ROWS, D_MODEL)
        g = jnp.broadcast_to(gpre_ref[...], slab_shape)
        scale = jnp.broadcast_to(1.0 + mod_ref[4:5, :], slab_shape)
        shift = jnp.broadcast_to(mod_ref[3:4, :], slab_shape)

        def hmod(x, r):
            return x * r * g * scale + shift

        def rinv(x):
            return lax.rsqrt(jnp.mean(x * x, axis=-1, keepdims=True) + NORM_EPS)

        _store_rinv(r_ref, x_ref[...])

        def slab(k, carry):
            rows = pl.ds(pl.multiple_of(k * BF16_ROWS, BF16_ROWS), BF16_ROWS)
            hx_ref[rows, :] = hmod(x_ref[rows, :], _load_rinv(r_ref, rows)).astype(BF16)
            return carry

        lax.fori_loop(0, n // BF16_ROWS, slab, 0, unroll=4)
        has_prev, has_next = _tile_edges(i, n)
        halo = jnp.concatenate([xp_ref[...], xn_ref[...]], axis=0)
        keep = jnp.where(lax.broadcasted_iota(jnp.int32, (BF16_ROWS, 1), 0) < SUBLANE,
                         has_prev, has_next)
        hx_ref[n:, :] = (hmod(halo, rinv(halo)) * keep).astype(BF16)
        xo_ref[...] = jnp.zeros_like(xo_ref)

    is_ctx = i * n >= NX
    hx = hx_ref[...]

    u = jnp.dot(hx, wgv_ref[...], preferred_element_type=F32)

    def conv(cols, c_ref):
        return _conv3_rows(u[:n, cols], u[n + SUBLANE - 1:n + SUBLANE, cols],
                           u[n + SUBLANE:n + SUBLANE + 1, cols], c_ref, is_ctx)

    gate = conv(slice(0, TF), cg_ref)
    val = conv(slice(TF, 2 * TF), cv_ref)
    act = (gate * jax.nn.sigmoid(gate) * val).astype(BF16)
    xo_ref[...] += jnp.dot(act, wd_ref[...], preferred_element_type=F32)

    @pl.when(j == pl.num_programs(1) - 1)
    def _():
        _store_rinv(r_ref, xo_ref[...])
        slab_shape = (BF16_ROWS, D_MODEL)
        g = jnp.broadcast_to(gpost_ref[...], slab_shape)
        res_gate = jnp.broadcast_to(mod_ref[5:6, :], slab_shape)

        def slab(k, carry):
            rows = pl.ds(pl.multiple_of(k * BF16_ROWS, BF16_ROWS), BF16_ROWS)
            normed = xo_ref[rows, :] * _load_rinv(r_ref, rows) * g
            xo_ref[rows, :] = x_ref[rows, :] + res_gate * normed
            return carry

        lax.fori_loop(0, n // BF16_ROWS, slab, 0, unroll=4)


def _ffn(l, tok, mods, g_pre, g_post, w_up, conv_w, w_down, n_rows):
    nj = D_FF // TF
    return pl.pallas_call(
        _ffn_kernel,
        grid=(n_rows // TM_BIG, nj),
        in_specs=(_halo_specs(TM_BIG, D_MODEL, n_rows, 0) + [
            _mod_spec(l, TM_BIG),
            _layer_spec(l, (1, D_MODEL)),
            _layer_spec(l, (1, D_MODEL)),
            pl.BlockSpec((D_MODEL, 2 * TF), lambda i, j: (0, j)),
            pl.BlockSpec((None, 3, TF), lambda i, j: (l, 0, j)),
            pl.BlockSpec((None, 3, TF), lambda i, j: (l, 0, j + nj)),
            pl.BlockSpec((TF, D_MODEL), lambda i, j: (j, 0)),
        ]),
        out_specs=pl.BlockSpec((TM_BIG, D_MODEL), lambda i, j: (i, 0)),
        out_shape=jax.ShapeDtypeStruct((n_rows, D_MODEL), F32),
        scratch_shapes=[pltpu.VMEM((TM_BIG + BF16_ROWS, D_MODEL), BF16),
                        pltpu.VMEM((TM_BIG, LANE), F32)],
        compiler_params=_params(("arbitrary", "arbitrary"), VMEM_LIMIT_BIG),
        name="ffn",
    )(tok, tok, tok, mods, g_pre, g_post, w_up, conv_w, conv_w, w_down)


def _rope_tables():
    rows = SEQ // GRID_W
    row = jnp.repeat(jnp.arange(rows), GRID_W).astype(F32)
    col = jnp.tile(jnp.arange(GRID_W), rows).astype(F32)
    n_freq = 64 // 4
    inv = ROPE_THETA ** (-jnp.arange(n_freq, dtype=F32) / n_freq)
    ang = jnp.concatenate([row[:, None] * inv, col[:, None] * inv], axis=-1)
    cos, sin = jnp.cos(ang), jnp.sin(ang)
    cos_t = jnp.concatenate([cos, cos, cos, cos], axis=-1)
    sin_t = jnp.concatenate([-sin, sin, -sin, sin], axis=-1)
    cos_t = jnp.concatenate([cos_t, jnp.ones((TM, LANE), F32)], axis=0)
    sin_t = jnp.concatenate([sin_t, jnp.zeros((TM, LANE), F32)], axis=0)
    return cos_t, sin_t


W_IN_KB = 256


def _w_in_kernel(w_ref, o_ref):
    for dst, src, n in ((0, 0, 2048), (COL_DB, 2752, 1536), (COL_CKV, 2432, 256),
                        (COL_CQ, 2048, 384), (COL_KR, 2688, 64)):
        o_ref[dst:dst + n, :] = w_ref[src:src + n, :].astype(BF16)
    o_ref[COL_KR + 64:, :] = jnp.zeros((IN_COLS_PAD - COL_KR - 64, W_IN_KB), BF16)


def _layout_w_in(w):
    wt = jnp.swapaxes(w, 1, 2)
    n, k = wt.shape[1:]
    return pl.pallas_call(
        _w_in_kernel,
        grid=(DEPTH, k // W_IN_KB),
        in_specs=[pl.BlockSpec((None, n, W_IN_KB), lambda l, i: (l, 0, i))],
        out_specs=pl.BlockSpec((None, IN_COLS_PAD, W_IN_KB), lambda l, i: (l, 0, i)),
        out_shape=jax.ShapeDtypeStruct((DEPTH, IN_COLS_PAD, k), BF16),
        compiler_params=_params(("arbitrary", "arbitrary")),
        name="w_in_layout",
    )(wt)


def _layout_w_uq(w):
    w = w.reshape(DEPTH, 384, HEADS, 192)
    return jnp.concatenate([w[..., :128].reshape(DEPTH, 384, 512),
                            w[..., 128:].reshape(DEPTH, 384, 256)], axis=-1).astype(BF16)


def _layout_w_ukv(w):
    w = w.reshape(DEPTH, 256, HEADS, 256)
    return jnp.concatenate([w[..., :128].reshape(DEPTH, 256, 512),
                            w[..., 128:].reshape(DEPTH, 256, 512)], axis=-1).astype(BF16)


def kernel(x, c, ctx, c_ctx, ada_w, ada_b, g_pre_mix, g_post_mix, g_pre_ffn, g_post_ffn, w_in,
           pool_w, pool_scale, diff_lambda, diff_subln_g, mla_gq, mla_w_uq, mla_gkv, mla_w_ukv,
           conv_w, w_out, ffn_w_up, ffn_conv_w, ffn_w_down):
    toks = (x.reshape(NX, D_MODEL), ctx.reshape(NC, D_MODEL))
    cc = jnp.concatenate([c, c_ctx[None, :], jnp.zeros((3, D_MODEL), F32)], axis=0)
    mods = _ada(cc, ada_w, ada_b).reshape(DEPTH, 8, 6, D_MODEL)
    cos_t, sin_t = _rope_tables()

    def vec(p):
        return p.reshape(DEPTH, 1, p.shape[-1])

    w_in_b = _layout_w_in(w_in)
    w_uq_b, w_ukv_b = _layout_w_uq(mla_w_uq), _layout_w_ukv(mla_w_ukv)
    pool_w_b = pool_w.astype(BF16)

    for l in range(DEPTH):
        last = l == DEPTH - 1
        n_rows = NX if last else NT
        lam_init = 0.8 - 0.6 * math.exp(-0.3 * l)

        z = _in_proj(l, toks, mods, vec(g_pre_mix), w_in_b)
        ya, yd = _pool_conv(l, z, n_rows, pool_w_b, vec(pool_scale), conv_w)
        yb, w_up_b = _diff_attn(l, z, cos_t, sin_t, diff_lambda, vec(diff_subln_g), lam_init,
                                not last, ffn_w_up)
        qn, qr, kn, v, kr2 = _mla_prep(l, z, vec(mla_gq), vec(mla_gkv), w_uq_b, w_ukv_b, cos_t, sin_t)
        ym, w_down_b = _mla_attn(l, qn, qr, kn, v, kr2, not last, ffn_w_down)
        tok = _out_proj(l, ya, yb, ym, yd, w_out, toks, mods, vec(g_post_mix))
        tok = _ffn(l, tok, mods, vec(g_pre_ffn), vec(g_post_ffn), w_up_b, ffn_conv_w, w_down_b, n_rows)
        toks = (tok,)
    return tok.reshape(BATCH, SEQ, D_MODEL)
```

```python
import functools
import math

import jax
import jax.numpy as jnp
from jax import lax
from jax.experimental import pallas as pl
from jax.experimental.pallas import tpu as pltpu

F32 = jnp.float32
BF16 = jnp.bfloat16

D_MODEL = 2048
BATCH = 4
SEQ = 2048
CTX_LEN = 256
GRID_W = 64
DEPTH = 2
NORM_EPS = 1e-6
ROPE_THETA = 10000.0
GROUP_W = 512
POOL_WINDOWS = (2, 4, 8, 16)
HEADS = 4
D_FF = 5632

NX = BATCH * SEQ
NC = BATCH * CTX_LEN
NT = NX + NC

COL_A, COL_Q, COL_K, COL_V = 0, 512, 1024, 1536
COL_DB, COL_DC, COL_DH = 2048, 2560, 3072
COL_CKV, COL_CQ, COL_KR = 3584, 3840, 4224
IN_COLS_PAD = 4352

LANE = 128
SUBLANE = 8
BF16_ROWS = 2 * SUBLANE
TM = 512
TM_BIG = 1024
TS = 256
QSUB = 256
TF = 512
TN_ADA = 1024
VMEM_LIMIT = 56 * 1024 * 1024
VMEM_LIMIT_BIG = 62 * 1024 * 1024


def _params(sem, vmem_limit=VMEM_LIMIT):
    return pltpu.CompilerParams(dimension_semantics=sem, vmem_limit_bytes=vmem_limit)


def _split_rows_specs(rows, n_lat_rows):
    n_lat = n_lat_rows // rows
    return [pl.BlockSpec((rows, D_MODEL), lambda i, *_: (jnp.minimum(i, n_lat - 1), 0)),
            pl.BlockSpec((rows, D_MODEL), lambda i, *_: (jnp.maximum(i - n_lat, 0), 0),
                         pipeline_mode=pl.Buffered(1))]


def _pick_rows(i, rows, x_ref, c_ref, fn):
    if c_ref is None:
        fn(x_ref)
        return

    @pl.when(i * rows < NX)
    def _():
        fn(x_ref)

    @pl.when(i * rows >= NX)
    def _():
        fn(c_ref)


def _rms(x, g):
    ms = jnp.mean(x * x, axis=-1, keepdims=True)
    return x * lax.rsqrt(ms + NORM_EPS) * g


def _store_rinv(r_ref, x):
    r = lax.rsqrt(jnp.mean(x * x, axis=-1, keepdims=True) + NORM_EPS)
    r_ref[...] = jnp.broadcast_to(r, r_ref.shape)


def _load_rinv(r_ref, rows):
    return jnp.tile(r_ref[rows, :], (1, D_MODEL // LANE))


def _nt_dot(a, b):
    return lax.dot_general(a, b, (((1,), (1,)), ((), ())), preferred_element_type=F32)


def _rope(v, cos, sin):
    lane = lax.broadcasted_iota(jnp.int32, v.shape, 1)
    sw = jnp.where((lane & 32) == 0, pltpu.roll(v, 96, 1), pltpu.roll(v, 32, 1))
    return v * cos + sw * sin


def _seq_pos(tile, rows):
    r = tile * rows + lax.broadcasted_iota(jnp.int32, (rows, 1), 0)
    seqlen = jnp.where(r < NX, SEQ, CTX_LEN)
    return r & (seqlen - 1), seqlen


def _tile_edges(tile, rows):
    r0 = tile * rows
    seqlen = jnp.where(r0 < NX, SEQ, CTX_LEN)
    has_prev = jnp.where((r0 & (seqlen - 1)) == 0, 0.0, 1.0)
    has_next = jnp.where(((r0 + rows) & (seqlen - 1)) == 0, 0.0, 1.0)
    return has_prev, has_next


def _conv3_rows(u, prev_row, next_row, c, is_ctx):
    n = u.shape[0]
    row8 = lax.broadcasted_iota(jnp.int32, (SUBLANE, 1), 0)
    prev = pltpu.roll(u, 1, 0)
    prev = jnp.concatenate([jnp.where(row8 == 0, prev_row, prev[:SUBLANE]), prev[SUBLANE:]], axis=0)
    nxt = pltpu.roll(u, n - 1, 0)
    nxt = jnp.concatenate([nxt[:n - SUBLANE],
                           jnp.where(row8 == SUBLANE - 1, next_row, nxt[n - SUBLANE:])], axis=0)
    ctx_row = row8 + jnp.where(is_ctx, 0, 2 * SUBLANE)
    for p in range(CTX_LEN, n, CTX_LEN):
        prev = jnp.concatenate([prev[:p], jnp.where(ctx_row == 0, 0.0, prev[p:p + SUBLANE]),
                                prev[p + SUBLANE:]], axis=0)
        nxt = jnp.concatenate([nxt[:p - SUBLANE],
                               jnp.where(ctx_row == SUBLANE - 1, 0.0, nxt[p - SUBLANE:p]), nxt[p:]],
                              axis=0)
    return c[0:1, :] * prev + c[1:2, :] * u + c[2:3, :] * nxt


def _first_axis(i, *_):
    return i


def _halo_specs(rows, width, n_rows, col, tile=_first_axis):
    r8 = rows // SUBLANE
    last8 = n_rows // SUBLANE - 1
    return [
        pl.BlockSpec((rows, width), lambda *g: (tile(*g), col)),
        pl.BlockSpec((SUBLANE, width), lambda *g: (jnp.maximum(tile(*g) * r8 - 1, 0), col)),
        pl.BlockSpec((SUBLANE, width), lambda *g: (jnp.minimum((tile(*g) + 1) * r8, last8), col)),
    ]


def _layer_spec(l, shape):
    return pl.BlockSpec((None,) + shape, lambda *_: (l,) + (0,) * len(shape))


def _mod_spec(l, rows, tile=_first_axis):
    def index(*g):
        r = tile(*g) * rows
        return (l, jnp.where(r < NX, r // SEQ, BATCH), 0, 0)
    return pl.BlockSpec((None, None, 6, D_MODEL), index)


def _ada_kernel(c_ref, w_ref, b_ref, o_ref):
    c = c_ref[...]
    a = c * jax.nn.sigmoid(c)
    o_ref[...] = jnp.dot(a.astype(BF16), w_ref[...].astype(BF16),
                         preferred_element_type=F32) + b_ref[...]


def _ada(cc, ada_w, ada_b):
    n = ada_w.shape[-1]
    return pl.pallas_call(
        _ada_kernel,
        grid=(DEPTH, n // TN_ADA),
        in_specs=[
            pl.BlockSpec((8, D_MODEL), lambda l, j: (0, 0)),
            pl.BlockSpec((None, D_MODEL, TN_ADA), lambda l, j: (l, 0, j)),
            pl.BlockSpec((None, 1, TN_ADA), lambda l, j: (l, 0, j)),
        ],
        out_specs=pl.BlockSpec((None, 8, TN_ADA), lambda l, j: (l, 0, j)),
        out_shape=jax.ShapeDtypeStruct((DEPTH, 8, n), F32),
        compiler_params=_params(("arbitrary", "arbitrary")),
        name="ada",
    )(cc, ada_w, ada_b.reshape(DEPTH, 1, n))


def _in_kernel(split, x_ref, *refs):
    c_ref, (mod_ref, g_ref, w_ref, z_ref, h_ref, r_ref) = ((refs[0], refs[1:]) if split
                                                              else (None, refs))

    def put(src):
        _store_rinv(r_ref, src[...])

        shape = (BF16_ROWS, D_MODEL)
        g = jnp.broadcast_to(g_ref[...], shape)
        scale = jnp.broadcast_to(1.0 + mod_ref[1:2, :], shape)
        shift = jnp.broadcast_to(mod_ref[0:1, :], shape)

        def slab(k, carry):
            rows = pl.ds(pl.multiple_of(k * BF16_ROWS, BF16_ROWS), BF16_ROWS)
            y = src[rows, :] * _load_rinv(r_ref, rows) * g
            h_ref[rows, :] = (y * scale + shift).astype(BF16)
            return carry

        lax.fori_loop(0, TM // BF16_ROWS, slab, 0, unroll=4)

    _pick_rows(pl.program_id(0), TM, x_ref, c_ref, put)
    z_ref[...] = _nt_dot(h_ref[...], w_ref[...])


def _in_proj(l, toks, mods, g, w):
    split = len(toks) == 2
    rows = (_split_rows_specs(TM, NX) if split
            else [pl.BlockSpec((TM, D_MODEL), lambda i: (i, 0))])
    return pl.pallas_call(
        functools.partial(_in_kernel, split),
        grid=(NT // TM,),
        in_specs=rows + [
            _mod_spec(l, TM),
            _layer_spec(l, (1, D_MODEL)),
            pl.BlockSpec((None, IN_COLS_PAD, D_MODEL), lambda i: (l, 0, 0),
                         pipeline_mode=pl.Buffered(1)),
        ],
        out_specs=pl.BlockSpec((TM, IN_COLS_PAD), lambda i: (i, 0)),
        out_shape=jax.ShapeDtypeStruct((NT, IN_COLS_PAD), F32),
        scratch_shapes=[pltpu.VMEM((TM, D_MODEL), BF16), pltpu.VMEM((TM, LANE), F32)],
        compiler_params=_params(("arbitrary",), VMEM_LIMIT_BIG),
        name="in_proj",
    )(*toks, mods, g, w)


def _pool_conv_kernel(a_ref, ap_ref, an_ref, db_ref, dc_ref, dcp_ref, dcn_ref,
                      dh_ref, dhp_ref, dhn_ref, pw_ref, ps_ref, cw_ref,
                      ya_ref, yd_ref, e_ref):
    i = pl.program_id(0)
    pos, seqlen = _seq_pos(i, TS)
    pf, nf = _tile_edges(i, TS)

    e_ref[0:8, :] = ap_ref[...] * pf
    e_ref[8:8 + TS, :] = a_ref[...]
    e_ref[8 + TS:16 + TS, :] = an_ref[...] * nf
    for g, w in enumerate(POOL_WINDOWS):
        half = w // 2
        cs = slice(g * LANE, (g + 1) * LANE)
        c, s = e_ref[:, cs], 1
        while s < w:
            c = c + pltpu.roll(c, TS + 16 - s, 0)
            s *= 2
        acc = c[8 - half:8 - half + TS]
        cnt = (jnp.minimum(pos + half, seqlen) - jnp.maximum(pos - half, 0)).astype(F32)
        pooled = acc * (1.0 / cnt) - a_ref[:, cs]
        y = jnp.dot(pooled.astype(BF16), pw_ref[g], preferred_element_type=F32)
        ya_ref[:, cs] = (y * ps_ref[:, cs]).astype(BF16)

    p_prev = dcp_ref[SUBLANE - 1:SUBLANE, :] * dhp_ref[SUBLANE - 1:SUBLANE, :] * pf
    p_next = dcn_ref[0:1, :] * dhn_ref[0:1, :] * nf
    conv = _conv3_rows(dc_ref[...] * dh_ref[...], p_prev, p_next, cw_ref[...], False)
    yd_ref[...] = (db_ref[...] * conv).astype(BF16)


def _pool_conv(l, z, n_rows, pool_w, pool_scale, conv_w):
    ca, cb, cc, ch = (COL_A // GROUP_W, COL_DB // GROUP_W, COL_DC // GROUP_W, COL_DH // GROUP_W)
    out = jax.ShapeDtypeStruct((n_rows, GROUP_W), BF16)
    return pl.pallas_call(
        _pool_conv_kernel,
        grid=(n_rows // TS,),
        in_specs=(_halo_specs(TS, GROUP_W, NT, ca)
                  + [pl.BlockSpec((TS, GROUP_W), lambda i: (i, cb))]
                  + _halo_specs(TS, GROUP_W, NT, cc) + _halo_specs(TS, GROUP_W, NT, ch)
                  + [_layer_spec(l, (len(POOL_WINDOWS), LANE, LANE)),
                     _layer_spec(l, (1, GROUP_W)), _layer_spec(l, (3, GROUP_W))]),
        out_specs=[pl.BlockSpec((TS, GROUP_W), lambda i: (i, 0))] * 2,
        out_shape=[out, out],
        scratch_shapes=[pltpu.VMEM((TS + 16, GROUP_W), F32)],
        compiler_params=_params(("arbitrary",)),
        name="pool_conv",
    )(z, z, z, z, z, z, z, z, z, z, pool_w, pool_scale, conv_w)


CBLK = NX // CTX_LEN


def _softmax_pv(s, v):
    e = jnp.exp(s - jnp.max(s, axis=-1, keepdims=True)).astype(BF16)
    return jnp.dot(e, v, preferred_element_type=F32)


def _with_ones(v):
    return jnp.concatenate([v, jnp.ones(v.shape, v.dtype)], axis=-1)


def _cast_rows(w_ref, wb_ref):
    wb_ref[...] = w_ref[...].astype(BF16)


def _cast_rows_pair_chunks(w_ref, wb_ref):
    for k in range(D_FF // TF):
        for part in range(2):
            src = part * D_FF + k * TF
            dst = (2 * k + part) * TF
            wb_ref[:, dst:dst + TF] = w_ref[:, src:src + TF].astype(BF16)


def _attn_calls(lat_fn, ctx_fn, name, lat_ins, lat_specs, ctx_ins, ctx_specs, scratch, with_ctx,
                l, w_f32, cast_fn):
    steps = BATCH * HEADS
    w_rows, w_cols = w_f32.shape[1] // steps, w_f32.shape[2]

    def lat_body(*refs):
        n = len(lat_ins)
        w_ref, (o_ref, wb_ref), rest = refs[n], refs[-len(scratch) - 2:-len(scratch)], refs[-len(scratch):]
        cast_fn(w_ref, wb_ref)
        lat_fn(*refs[:n], o_ref, *rest)

    y0 = [jnp.zeros((NT, GROUP_W), BF16)] if with_ctx else []
    y_shape = jax.ShapeDtypeStruct((NT if with_ctx else NX, GROUP_W), BF16)
    y, w_b = pl.pallas_call(
        lat_body,
        grid=(BATCH, HEADS),
        in_specs=(lat_specs
                  + [pl.BlockSpec((None, w_rows, w_cols), lambda b, h: (l, b * HEADS + h, 0))]
                  + [pl.BlockSpec(memory_space=pl.ANY)] * len(y0)),
        out_specs=[pl.BlockSpec((SEQ, LANE), lambda b, h: (b, h)),
                   pl.BlockSpec((w_rows, w_cols), lambda b, h: (b * HEADS + h, 0))],
        out_shape=[y_shape, jax.ShapeDtypeStruct(w_f32.shape[1:], BF16)],
        input_output_aliases={len(lat_ins) + 1: 0} if with_ctx else {},
        scratch_shapes=scratch,
        compiler_params=_params(("arbitrary", "arbitrary")),
        name=name,
    )(*lat_ins, w_f32, *y0)
    if not with_ctx:
        return y, w_b
    y = pl.pallas_call(
        lambda *refs: ctx_fn(*refs[:len(ctx_ins)], refs[-1]),
        grid=(BATCH, HEADS),
        in_specs=ctx_specs + [pl.BlockSpec(memory_space=pl.ANY)],
        out_specs=pl.BlockSpec((CTX_LEN, LANE), lambda b, h: (CBLK + b, h)),
        out_shape=y_shape,
        input_output_aliases={len(ctx_ins): 0},
        compiler_params=_params(("arbitrary", "arbitrary")),
        name=name + "_ctx",
    )(*ctx_ins, y)
    return y, w_b


def _diff_lambda(lam_ref, lam_init):
    lp = lam_ref[...]
    return (jnp.exp(jnp.sum(lp[0:1] * lp[1:2], keepdims=True))
            - jnp.exp(jnp.sum(lp[2:3] * lp[3:4], keepdims=True)) + lam_init)


def _diff_chain(q, k, v, lam, g, lam_init):
    q = q * (64 ** -0.5)
    lane = lax.broadcasted_iota(jnp.int32, q.shape, 1)
    o1 = _softmax_pv(_nt_dot(jnp.where(lane < 64, q, 0.0).astype(BF16), k[...]), v[...])
    o2 = _softmax_pv(_nt_dot(jnp.where(lane >= 64, q, 0.0).astype(BF16), k[...]), v[...])
    o = o1[:, :LANE] / o1[:, LANE:] - o2[:, :LANE] * (lam / o2[:, LANE:])
    return (_rms(o, g) * (1.0 - lam_init)).astype(BF16)


def _diff_kernel(lam_init, q_ref, kx_ref, kc_ref, vx_ref, vc_ref, cos_ref, sin_ref, lam_ref, g_ref,
                 o_ref, kb_ref, vb_ref):
    kb_ref[0:SEQ, :] = _rope(kx_ref[...], cos_ref[...], sin_ref[...]).astype(BF16)
    kb_ref[SEQ:, :] = kc_ref[...].astype(BF16)
    vb_ref[0:SEQ, 0:LANE] = vx_ref[...].astype(BF16)
    vb_ref[SEQ:, 0:LANE] = vc_ref[...].astype(BF16)
    vb_ref[:, LANE:] = jnp.ones((SEQ + CTX_LEN, LANE), BF16)
    lam = _diff_lambda(lam_ref, lam_init)
    for r in range(0, SEQ, QSUB):
        rows = slice(r, r + QSUB)
        q = _rope(q_ref[rows, :], cos_ref[rows, :], sin_ref[rows, :])
        o_ref[rows, :] = _diff_chain(q, kb_ref, vb_ref, lam, g_ref[...], lam_init)


def _diff_ctx_kernel(lam_init, q_ref, k_ref, v_ref, lam_ref, g_ref, o_ref):
    o_ref[...] = _diff_chain(q_ref[...], k_ref[...].astype(BF16),
                             _with_ones(v_ref[...].astype(BF16)),
                             _diff_lambda(lam_ref, lam_init), g_ref[...], lam_init)


def _diff_attn(l, z, cos_t, sin_t, lam_p, subln_g, lam_init, with_ctx, w_f32):
    cq, ck, cv = COL_Q // LANE, COL_K // LANE, COL_V // LANE

    def xs(col):
        return pl.BlockSpec((SEQ, LANE), lambda b, h: (b, col + h))

    def cs(col):
        return pl.BlockSpec((CTX_LEN, LANE), lambda b, h: (CBLK + b, col + h))

    table = pl.BlockSpec((SEQ, LANE), lambda b, h: (0, 0))
    params = [_layer_spec(l, (4, 64)), _layer_spec(l, (1, LANE))]
    return _attn_calls(
        functools.partial(_diff_kernel, lam_init), functools.partial(_diff_ctx_kernel, lam_init),
        "diff_attn",
        [z, z, z, z, z, cos_t, sin_t, lam_p, subln_g],
        [xs(cq), xs(ck), cs(ck), xs(cv), cs(cv), table, table] + params,
        [z, z, z, lam_p, subln_g],
        [cs(cq), cs(ck), cs(cv)] + params,
        [pltpu.VMEM((SEQ + CTX_LEN, LANE), BF16), pltpu.VMEM((SEQ + CTX_LEN, 2 * LANE), BF16)],
        with_ctx, l, w_f32, _cast_rows_pair_chunks)


def _mla_prep_kernel(ckv_ref, cq_ref, kr_ref, gq_ref, gkv_ref, wq_ref, wkv_ref, cos_ref, sin_ref,
                     qn_ref, qr_ref, kn_ref, v_ref, kr2_ref):
    cos = cos_ref[...]
    sin = sin_ref[...]
    q = jnp.dot(_rms(cq_ref[...], gq_ref[...]).astype(BF16), wq_ref[...],
                preferred_element_type=F32)
    qn_ref[...] = q[:, :GROUP_W].astype(BF16)
    for c in range(0, 2 * LANE, LANE):
        qr_ref[:, c:c + LANE] = _rope(q[:, GROUP_W + c:GROUP_W + c + LANE], cos, sin).astype(BF16)
    kv = jnp.dot(_rms(ckv_ref[...], gkv_ref[...]).astype(BF16), wkv_ref[...],
                 preferred_element_type=F32)
    kn_ref[...] = kv[:, :GROUP_W].astype(BF16)
    v_ref[...] = kv[:, GROUP_W:].astype(BF16)
    kr = _rope(kr_ref[...], cos, sin)
    kr2_ref[...] = (kr + pltpu.roll(kr, 64, 1)).astype(BF16)


def _mla_prep(l, z, gq, gkv, wq, wkv, cos_t, sin_t):
    def out(w):
        return jax.ShapeDtypeStruct((NT, w), BF16)

    def row(w):
        return pl.BlockSpec((TM, w), lambda i: (i, 0))

    rope = pl.BlockSpec((TM, LANE), lambda i: (jnp.where(i * TM < NX, i % (SEQ // TM), SEQ // TM), 0))
    return pl.pallas_call(
        _mla_prep_kernel,
        grid=(NT // TM,),
        in_specs=[
            pl.BlockSpec((TM, 256), lambda i: (i, COL_CKV // 256)),
            pl.BlockSpec((TM, 384), lambda i: (i, COL_CQ // 384)),
            pl.BlockSpec((TM, LANE), lambda i: (i, COL_KR // LANE)),
            _layer_spec(l, (1, 384)),
            _layer_spec(l, (1, 256)),
            _layer_spec(l, (384, 768)),
            _layer_spec(l, (256, 1024)),
            rope, rope,
        ],
        out_specs=[row(GROUP_W), row(2 * LANE), row(GROUP_W), row(GROUP_W), row(LANE)],
        out_shape=[out(GROUP_W), out(2 * LANE), out(GROUP_W), out(GROUP_W), out(LANE)],
        compiler_params=_params(("arbitrary",)),
        name="mla_prep",
    )(z, z, z, gq, gkv, wq, wkv, cos_t, sin_t)


def _mla_chain(h, qn, qr, k, v):
    lane = lax.broadcasted_iota(jnp.int32, qr.shape, 1)
    lo = (h % 2) * 64
    qr = jnp.where((lane >= lo) & (lane < lo + 64), qr, jnp.zeros_like(qr))
    s = _nt_dot(jnp.concatenate([qn, qr], axis=-1), k[...]) * ((128 + 64) ** -0.5)
    o = _softmax_pv(s, v[...])
    return (o[:, :LANE] / o[:, LANE:]).astype(BF16)


def _mla_kernel(qn_ref, qr_ref, knx_ref, knc_ref, krx_ref, krc_ref, vx_ref, vc_ref,
                o_ref, kb_ref, vb_ref):
    h = pl.program_id(1)
    kb_ref[0:SEQ, 0:LANE] = knx_ref[...]
    kb_ref[SEQ:, 0:LANE] = knc_ref[...]
    kb_ref[0:SEQ, LANE:] = krx_ref[...]
    kb_ref[SEQ:, LANE:] = krc_ref[...]
    vb_ref[0:SEQ, 0:LANE] = vx_ref[...]
    vb_ref[SEQ:, 0:LANE] = vc_ref[...]
    vb_ref[:, LANE:] = jnp.ones((SEQ + CTX_LEN, LANE), BF16)
    for r in range(0, SEQ, QSUB):
        rows = slice(r, r + QSUB)
        o_ref[rows, :] = _mla_chain(h, qn_ref[rows, :], qr_ref[rows, :], kb_ref, vb_ref)


def _mla_ctx_kernel(qn_ref, qr_ref, kn_ref, kr_ref, v_ref, o_ref):
    k = jnp.concatenate([kn_ref[...], kr_ref[...]], axis=-1)
    o_ref[...] = _mla_chain(pl.program_id(1), qn_ref[...], qr_ref[...], k, _with_ones(v_ref[...]))


def _mla_attn(l, qn, qr, kn, v, kr2, with_ctx, w_f32):
    def xs(col):
        return pl.BlockSpec((SEQ, LANE), lambda b, h: (b, col(h)))

    def cs(col):
        return pl.BlockSpec((CTX_LEN, LANE), lambda b, h: (CBLK + b, col(h)))

    head, pair, first = (lambda h: h), (lambda h: h // 2), (lambda h: 0)
    return _attn_calls(
        _mla_kernel, _mla_ctx_kernel, "mla_attn",
        [qn, qr, kn, kn, kr2, kr2, v, v],
        [xs(head), xs(pair), xs(head), cs(head), xs(first), cs(first), xs(head), cs(head)],
        [qn, qr, kn, kr2, v],
        [cs(head), cs(pair), cs(head), cs(first), cs(head)],
        [pltpu.VMEM((SEQ + CTX_LEN, 2 * LANE), BF16), pltpu.VMEM((SEQ + CTX_LEN, 2 * LANE), BF16)],
        with_ctx, l, w_f32, _cast_rows)


def _out_kernel(split, ya_ref, yb_ref, ym_ref, yd_ref, w_ref, x_ref, *refs):
    c_ref, (mod_ref, gpost_ref, xo_ref, wb_ref) = (refs[0], refs[1:]) if split else (None, refs)

    @pl.when(pl.program_id(0) == 0)
    def _():
        wb_ref[...] = w_ref[...].astype(BF16)

    y = jnp.dot(ya_ref[...], wb_ref[0:GROUP_W, :], preferred_element_type=F32)
    y += jnp.dot(yb_ref[...], wb_ref[GROUP_W:2 * GROUP_W, :], preferred_element_type=F32)
    y += jnp.dot(ym_ref[...], wb_ref[2 * GROUP_W:3 * GROUP_W, :], preferred_element_type=F32)
    y += jnp.dot(yd_ref[...], wb_ref[3 * GROUP_W:, :], preferred_element_type=F32)
    delta = mod_ref[2:3, :] * _rms(y, gpost_ref[...])

    def add_residual(src):
        xo_ref[...] = src[...] + delta

    _pick_rows(pl.program_id(0), TM, x_ref, c_ref, add_residual)


def _out_proj(l, ya, yb, ym, yd, w, toks, mods, g_post):
    n = ya.shape[0]
    split = len(toks) == 2
    y_spec = pl.BlockSpec((TM, GROUP_W), lambda i: (i, 0))
    row = pl.BlockSpec((TM, D_MODEL), lambda i: (i, 0))
    return pl.pallas_call(
        functools.partial(_out_kernel, split),
        grid=(n // TM,),
        in_specs=[
            y_spec, y_spec, y_spec, y_spec,
            pl.BlockSpec((None, D_MODEL, D_MODEL), lambda i: (l, 0, 0),
                         pipeline_mode=pl.Buffered(1)),
        ] + (_split_rows_specs(TM, NX) if split else [row]) + [
            _mod_spec(l, TM),
            _layer_spec(l, (1, D_MODEL)),
        ],
        out_specs=row,
        out_shape=jax.ShapeDtypeStruct((n, D_MODEL), F32),
        scratch_shapes=[pltpu.VMEM((D_MODEL, D_MODEL), BF16)],
        compiler_params=_params(("arbitrary",)),
        name="out_proj",
    )(ya, yb, ym, yd, w, *toks, mods, g_post)


def _ffn_kernel(x_ref, xp_ref, xn_ref, mod_ref, gpre_ref, gpost_ref, wgv_ref, cg_ref, cv_ref,
                wd_ref, xo_ref, hx_ref, r_ref):
    i = pl.program_id(0)
    j = pl.program_id(1)
    n = TM_BIG

    @pl.when(j == 0)
    def _():
        slab_shape = (BF16_ROWS, D_MODEL)
        g = jnp.broadcast_to(gpre_ref[...], slab_shape)
        scale = jnp.broadcast_to(1.0 + mod_ref[4:5, :], slab_shape)
        shift = jnp.broadcast_to(mod_ref[3:4, :], slab_shape)

        def hmod(x, r):
            return x * r * g * scale + shift

        def rinv(x):
            return lax.rsqrt(jnp.mean(x * x, axis=-1, keepdims=True) + NORM_EPS)

        _store_rinv(r_ref, x_ref[...])

        def slab(k, carry):
            rows = pl.ds(pl.multiple_of(k * BF16_ROWS, BF16_ROWS), BF16_ROWS)
            dst = pl.ds(pl.multiple_of((k + 1) * BF16_ROWS, BF16_ROWS), BF16_ROWS)
            hx_ref[dst, :] = hmod(x_ref[rows, :], _load_rinv(r_ref, rows)).astype(BF16)
            return carry

        lax.fori_loop(0, n // BF16_ROWS, slab, 0, unroll=4)
        has_prev, has_next = _tile_edges(i, n)
        halo = jnp.concatenate([xp_ref[...], xn_ref[...]], axis=0)
        keep = jnp.where(lax.broadcasted_iota(jnp.int32, (BF16_ROWS, 1), 0) < SUBLANE,
                         has_prev, has_next)
        halo = hmod(halo, rinv(halo)) * keep
        zeros = jnp.zeros((SUBLANE, D_MODEL), F32)
        hx_ref[:BF16_ROWS, :] = jnp.concatenate([zeros, halo[:SUBLANE]], axis=0).astype(BF16)
        hx_ref[BF16_ROWS + n:, :] = jnp.concatenate([halo[SUBLANE:], zeros], axis=0).astype(BF16)
        xo_ref[...] = jnp.zeros_like(xo_ref)

    is_ctx = i * n >= NX
    same_seq = jnp.where(is_ctx, 0.0, 1.0)
    half = n // 2
    us = [jnp.dot(hx_ref[r0:r0 + half + 2 * BF16_ROWS, :], wgv_ref[...],
                  preferred_element_type=F32) for r0 in (0, half)]
    for part, u in enumerate(us):
        r0 = part * half
        prev_row = u[BF16_ROWS - 1:BF16_ROWS]
        next_row = u[BF16_ROWS + half:BF16_ROWS + half + 1]
        if part == 0:
            next_row = next_row * same_seq
        else:
            prev_row = prev_row * same_seq
        main = u[BF16_ROWS:BF16_ROWS + half]

        def conv(cols, c_ref):
            return _conv3_rows(main[:, cols], prev_row[:, cols], next_row[:, cols], c_ref[...],
                               is_ctx)

        gate = conv(slice(0, TF), cg_ref)
        val = conv(slice(TF, 2 * TF), cv_ref)
        act = (gate * jax.nn.sigmoid(gate) * val).astype(BF16)
        xo_ref[r0:r0 + half, :] += jnp.dot(act, wd_ref[...], preferred_element_type=F32)

    @pl.when(j == pl.num_programs(1) - 1)
    def _():
        _store_rinv(r_ref, xo_ref[...])
        slab_shape = (BF16_ROWS, D_MODEL)
        g = jnp.broadcast_to(gpost_ref[...], slab_shape)
        res_gate = jnp.broadcast_to(mod_ref[5:6, :], slab_shape)

        def slab(k, carry):
            rows = pl.ds(pl.multiple_of(k * BF16_ROWS, BF16_ROWS), BF16_ROWS)
            normed = xo_ref[rows, :] * _load_rinv(r_ref, rows) * g
            xo_ref[rows, :] = x_ref[rows, :] + res_gate * normed
            return carry

        lax.fori_loop(0, n // BF16_ROWS, slab, 0, unroll=4)


def _ffn(l, tok, mods, g_pre, g_post, w_up, conv_w, w_down, n_rows):
    nj = D_FF // TF
    return pl.pallas_call(
        _ffn_kernel,
        grid=(n_rows // TM_BIG, nj),
        in_specs=(_halo_specs(TM_BIG, D_MODEL, n_rows, 0) + [
            _mod_spec(l, TM_BIG),
            _layer_spec(l, (1, D_MODEL)),
            _layer_spec(l, (1, D_MODEL)),
            pl.BlockSpec((D_MODEL, 2 * TF), lambda i, j: (0, j)),
            pl.BlockSpec((None, 3, TF), lambda i, j: (l, 0, j)),
            pl.BlockSpec((None, 3, TF), lambda i, j: (l, 0, j + nj)),
            pl.BlockSpec((TF, D_MODEL), lambda i, j: (j, 0)),
        ]),
        out_specs=pl.BlockSpec((TM_BIG, D_MODEL), lambda i, j: (i, 0)),
        out_shape=jax.ShapeDtypeStruct((n_rows, D_MODEL), F32),
        scratch_shapes=[pltpu.VMEM((TM_BIG + 2 * BF16_ROWS, D_MODEL), BF16),
                        pltpu.VMEM((TM_BIG, LANE), F32)],
        compiler_params=_params(("arbitrary", "arbitrary"), VMEM_LIMIT_BIG),
        name="ffn",
    )(tok, tok, tok, mods, g_pre, g_post, w_up, conv_w, conv_w, w_down)


def _rope_tables():
    rows = SEQ // GRID_W
    row = jnp.repeat(jnp.arange(rows), GRID_W).astype(F32)
    col = jnp.tile(jnp.arange(GRID_W), rows).astype(F32)
    n_freq = 64 // 4
    inv = ROPE_THETA ** (-jnp.arange(n_freq, dtype=F32) / n_freq)
    ang = jnp.concatenate([row[:, None] * inv, col[:, None] * inv], axis=-1)
    cos, sin = jnp.cos(ang), jnp.sin(ang)
    cos_t = jnp.concatenate([cos, cos, cos, cos], axis=-1)
    sin_t = jnp.concatenate([-sin, sin, -sin, sin], axis=-1)
    cos_t = jnp.concatenate([cos_t, jnp.ones((TM, LANE), F32)], axis=0)
    sin_t = jnp.concatenate([sin_t, jnp.zeros((TM, LANE), F32)], axis=0)
    return cos_t, sin_t


W_IN_KB = 256


def _w_in_kernel(w_ref, o_ref):
    for dst, src, n in ((0, 0, 2048), (COL_DB, 2752, 1536), (COL_CKV, 2432, 256),
                        (COL_CQ, 2048, 384), (COL_KR, 2688, 64)):
        o_ref[dst:dst + n, :] = w_ref[src:src + n, :].astype(BF16)
    o_ref[COL_KR + 64:, :] = jnp.zeros((IN_COLS_PAD - COL_KR - 64, W_IN_KB), BF16)


def _layout_w_in(w):
    wt = jnp.swapaxes(w, 1, 2)
    n, k = wt.shape[1:]
    return pl.pallas_call(
        _w_in_kernel,
        grid=(DEPTH, k // W_IN_KB),
        in_specs=[pl.BlockSpec((None, n, W_IN_KB), lambda l, i: (l, 0, i))],
        out_specs=pl.BlockSpec((None, IN_COLS_PAD, W_IN_KB), lambda l, i: (l, 0, i)),
        out_shape=jax.ShapeDtypeStruct((DEPTH, IN_COLS_PAD, k), BF16),
        compiler_params=_params(("arbitrary", "arbitrary")),
        name="w_in_layout",
    )(wt)


def _layout_w_uq(w):
    w = w.reshape(DEPTH, 384, HEADS, 192)
    return jnp.concatenate([w[..., :128].reshape(DEPTH, 384, 512),
                            w[..., 128:].reshape(DEPTH, 384, 256)], axis=-1).astype(BF16)


def _layout_w_ukv(w):
    w = w.reshape(DEPTH, 256, HEADS, 256)
    return jnp.concatenate([w[..., :128].reshape(DEPTH, 256, 512),
                            w[..., 128:].reshape(DEPTH, 256, 512)], axis=-1).astype(BF16)


def kernel(x, c, ctx, c_ctx, ada_w, ada_b, g_pre_mix, g_post_mix, g_pre_ffn, g_post_ffn, w_in,
           pool_w, pool_scale, diff_lambda, diff_subln_g, mla_gq, mla_w_uq, mla_gkv, mla_w_ukv,
           conv_w, w_out, ffn_w_up, ffn_conv_w, ffn_w_down):
    toks = (x.reshape(NX, D_MODEL), ctx.reshape(NC, D_MODEL))
    cc = jnp.concatenate([c, c_ctx[None, :], jnp.zeros((3, D_MODEL), F32)], axis=0)
    mods = _ada(cc, ada_w, ada_b).reshape(DEPTH, 8, 6, D_MODEL)
    cos_t, sin_t = _rope_tables()

    def vec(p):
        return p.reshape(DEPTH, 1, p.shape[-1])

    w_in_b = _layout_w_in(w_in)
    w_uq_b, w_ukv_b = _layout_w_uq(mla_w_uq), _layout_w_ukv(mla_w_ukv)
    pool_w_b = pool_w.astype(BF16)

    for l in range(DEPTH):
        last = l == DEPTH - 1
        n_rows = NX if last else NT
        lam_init = 0.8 - 0.6 * math.exp(-0.3 * l)

        z = _in_proj(l, toks, mods, vec(g_pre_mix), w_in_b)
        ya, yd = _pool_conv(l, z, n_rows, pool_w_b, vec(pool_scale), conv_w)
        yb, w_up_b = _diff_attn(l, z, cos_t, sin_t, diff_lambda, vec(diff_subln_g), lam_init,
                                not last, ffn_w_up)
        qn, qr, kn, v, kr2 = _mla_prep(l, z, vec(mla_gq), vec(mla_gkv), w_uq_b, w_ukv_b, cos_t, sin_t)
        ym, w_down_b = _mla_attn(l, qn, qr, kn, v, kr2, not last, ffn_w_down)
        tok = _out_proj(l, ya, yb, ym, yd, w_out, toks, mods, vec(g_post_mix))
        tok = _ffn(l, tok, mods, vec(g_pre_ffn), vec(g_post_ffn), w_up_b, ffn_conv_w, w_down_b, n_rows)
        toks = (tok,)
    return tok.reshape(BATCH, SEQ, D_MODEL)
```

```python
import functools
import math

import jax
import jax.numpy as jnp
from jax import lax
from jax.experimental import pallas as pl
from jax.experimental.pallas import tpu as pltpu

F32 = jnp.float32
BF16 = jnp.bfloat16

D_MODEL = 2048
BATCH = 4
SEQ = 2048
CTX_LEN = 256
GRID_W = 64
DEPTH = 2
NORM_EPS = 1e-6
ROPE_THETA = 10000.0
GROUP_W = 512
POOL_WINDOWS = (2, 4, 8, 16)
HEADS = 4
D_FF = 5632

NX = BATCH * SEQ
NC = BATCH * CTX_LEN
NT = NX + NC

COL_A, COL_Q, COL_K, COL_V = 0, 512, 1024, 1536
COL_DB, COL_DC, COL_DH = 2048, 2560, 3072
COL_CKV, COL_CQ, COL_KR = 3584, 3840, 4224
IN_COLS_PAD = 4352

LANE = 128
SUBLANE = 8
BF16_ROWS = 2 * SUBLANE
TM = 512
TM_BIG = 1024
TS = 256
QSUB = 256
TF = 512
TN_ADA = 1024
VMEM_LIMIT = 56 * 1024 * 1024
VMEM_LIMIT_BIG = 62 * 1024 * 1024


def _params(sem, vmem_limit=VMEM_LIMIT):
    return pltpu.CompilerParams(dimension_semantics=sem, vmem_limit_bytes=vmem_limit)


def _split_rows_specs(rows, n_lat_rows):
    n_lat = n_lat_rows // rows
    return [pl.BlockSpec((rows, D_MODEL), lambda i, *_: (jnp.minimum(i, n_lat - 1), 0)),
            pl.BlockSpec((rows, D_MODEL), lambda i, *_: (jnp.maximum(i - n_lat, 0), 0),
                         pipeline_mode=pl.Buffered(1))]


def _pick_rows(i, rows, x_ref, c_ref, fn):
    if c_ref is None:
        fn(x_ref)
        return

    @pl.when(i * rows < NX)
    def _():
        fn(x_ref)

    @pl.when(i * rows >= NX)
    def _():
        fn(c_ref)


def _rms(x, g):
    ms = jnp.mean(x * x, axis=-1, keepdims=True)
    return x * lax.rsqrt(ms + NORM_EPS) * g


def _store_rinv(r_ref, x):
    r = lax.rsqrt(jnp.mean(x * x, axis=-1, keepdims=True) + NORM_EPS)
    r_ref[...] = jnp.broadcast_to(r, r_ref.shape)


def _load_rinv(r_ref, rows):
    return jnp.tile(r_ref[rows, :], (1, D_MODEL // LANE))


def _nt_dot(a, b):
    return lax.dot_general(a, b, (((1,), (1,)), ((), ())), preferred_element_type=F32)


def _rope(v, cos, sin):
    lane = lax.broadcasted_iota(jnp.int32, v.shape, 1)
    sw = jnp.where((lane & 32) == 0, pltpu.roll(v, 96, 1), pltpu.roll(v, 32, 1))
    return v * cos + sw * sin


def _seq_pos(tile, rows):
    r = tile * rows + lax.broadcasted_iota(jnp.int32, (rows, 1), 0)
    seqlen = jnp.where(r < NX, SEQ, CTX_LEN)
    return r & (seqlen - 1), seqlen


def _tile_edges(tile, rows):
    r0 = tile * rows
    seqlen = jnp.where(r0 < NX, SEQ, CTX_LEN)
    has_prev = jnp.where((r0 & (seqlen - 1)) == 0, 0.0, 1.0)
    has_next = jnp.where(((r0 + rows) & (seqlen - 1)) == 0, 0.0, 1.0)
    return has_prev, has_next


def _conv3_rows(u, prev_row, next_row, c, is_ctx):
    n = u.shape[0]
    row8 = lax.broadcasted_iota(jnp.int32, (SUBLANE, 1), 0)
    prev = pltpu.roll(u, 1, 0)
    prev = jnp.concatenate([jnp.where(row8 == 0, prev_row, prev[:SUBLANE]), prev[SUBLANE:]], axis=0)
    nxt = pltpu.roll(u, n - 1, 0)
    nxt = jnp.concatenate([nxt[:n - SUBLANE],
                           jnp.where(row8 == SUBLANE - 1, next_row, nxt[n - SUBLANE:])], axis=0)
    ctx_row = row8 + jnp.where(is_ctx, 0, 2 * SUBLANE)
    for p in range(CTX_LEN, n, CTX_LEN):
        prev = jnp.concatenate([prev[:p], jnp.where(ctx_row == 0, 0.0, prev[p:p + SUBLANE]),
                                prev[p + SUBLANE:]], axis=0)
        nxt = jnp.concatenate([nxt[:p - SUBLANE],
                               jnp.where(ctx_row == SUBLANE - 1, 0.0, nxt[p - SUBLANE:p]), nxt[p:]],
                              axis=0)
    return c[0:1, :] * prev + c[1:2, :] * u + c[2:3, :] * nxt


def _first_axis(i, *_):
    return i


def _halo_specs(rows, width, n_rows, col, tile=_first_axis):
    r8 = rows // SUBLANE
    last8 = n_rows // SUBLANE - 1
    return [
        pl.BlockSpec((rows, width), lambda *g: (tile(*g), col)),
        pl.BlockSpec((SUBLANE, width), lambda *g: (jnp.maximum(tile(*g) * r8 - 1, 0), col)),
        pl.BlockSpec((SUBLANE, width), lambda *g: (jnp.minimum((tile(*g) + 1) * r8, last8), col)),
    ]


def _layer_spec(l, shape):
    return pl.BlockSpec((None,) + shape, lambda *_: (l,) + (0,) * len(shape))


def _mod_spec(l, rows, tile=_first_axis):
    def index(*g):
        r = tile(*g) * rows
        return (l, jnp.where(r < NX, r // SEQ, BATCH), 0, 0)
    return pl.BlockSpec((None, None, 6, D_MODEL), index)


def _ada_kernel(c_ref, w_ref, b_ref, o_ref):
    c = c_ref[...]
    a = c * jax.nn.sigmoid(c)
    o_ref[...] = jnp.dot(a.astype(BF16), w_ref[...].astype(BF16),
                         preferred_element_type=F32) + b_ref[...]


def _ada(cc, ada_w, ada_b):
    n = ada_w.shape[-1]
    return pl.pallas_call(
        _ada_kernel,
        grid=(DEPTH, n // TN_ADA),
        in_specs=[
            pl.BlockSpec((8, D_MODEL), lambda l, j: (0, 0)),
            pl.BlockSpec((None, D_MODEL, TN_ADA), lambda l, j: (l, 0, j)),
            pl.BlockSpec((None, 1, TN_ADA), lambda l, j: (l, 0, j)),
        ],
        out_specs=pl.BlockSpec((None, 8, TN_ADA), lambda l, j: (l, 0, j)),
        out_shape=jax.ShapeDtypeStruct((DEPTH, 8, n), F32),
        compiler_params=_params(("arbitrary", "arbitrary")),
        name="ada",
    )(cc, ada_w, ada_b.reshape(DEPTH, 1, n))


def _in_kernel(split, x_ref, *refs):
    c_ref, (mod_ref, g_ref, w_ref, z_ref, h_ref, r_ref) = ((refs[0], refs[1:]) if split
                                                              else (None, refs))

    def put(src):
        _store_rinv(r_ref, src[...])

        shape = (BF16_ROWS, D_MODEL)
        g = jnp.broadcast_to(g_ref[...], shape)
        scale = jnp.broadcast_to(1.0 + mod_ref[1:2, :], shape)
        shift = jnp.broadcast_to(mod_ref[0:1, :], shape)

        def slab(k, carry):
            rows = pl.ds(pl.multiple_of(k * BF16_ROWS, BF16_ROWS), BF16_ROWS)
            y = src[rows, :] * _load_rinv(r_ref, rows) * g
            h_ref[rows, :] = (y * scale + shift).astype(BF16)
            return carry

        lax.fori_loop(0, TM // BF16_ROWS, slab, 0, unroll=4)

    _pick_rows(pl.program_id(0), TM, x_ref, c_ref, put)
    z_ref[...] = _nt_dot(h_ref[...], w_ref[...])


def _in_proj(l, toks, mods, g, w):
    split = len(toks) == 2
    rows = (_split_rows_specs(TM, NX) if split
            else [pl.BlockSpec((TM, D_MODEL), lambda i: (i, 0))])
    return pl.pallas_call(
        functools.partial(_in_kernel, split),
        grid=(NT // TM,),
        in_specs=rows + [
            _mod_spec(l, TM),
            _layer_spec(l, (1, D_MODEL)),
            pl.BlockSpec((None, IN_COLS_PAD, D_MODEL), lambda i: (l, 0, 0),
                         pipeline_mode=pl.Buffered(1)),
        ],
        out_specs=pl.BlockSpec((TM, IN_COLS_PAD), lambda i: (i, 0)),
        out_shape=jax.ShapeDtypeStruct((NT, IN_COLS_PAD), F32),
        scratch_shapes=[pltpu.VMEM((TM, D_MODEL), BF16), pltpu.VMEM((TM, LANE), F32)],
        compiler_params=_params(("arbitrary",), VMEM_LIMIT_BIG),
        name="in_proj",
    )(*toks, mods, g, w)


def _pool_conv_kernel(a_ref, ap_ref, an_ref, db_ref, dc_ref, dcp_ref, dcn_ref,
                      dh_ref, dhp_ref, dhn_ref, pw_ref, ps_ref, cw_ref,
                      ya_ref, yd_ref, e_ref):
    i = pl.program_id(0)
    pos, seqlen = _seq_pos(i, TS)
    pf, nf = _tile_edges(i, TS)

    e_ref[0:8, :] = ap_ref[...] * pf
    e_ref[8:8 + TS, :] = a_ref[...]
    e_ref[8 + TS:16 + TS, :] = an_ref[...] * nf
    for g, w in enumerate(POOL_WINDOWS):
        half = w // 2
        cs = slice(g * LANE, (g + 1) * LANE)
        c, s = e_ref[:, cs], 1
        while s < w:
            c = c + pltpu.roll(c, TS + 16 - s, 0)
            s *= 2
        acc = c[8 - half:8 - half + TS]
        cnt = (jnp.minimum(pos + half, seqlen) - jnp.maximum(pos - half, 0)).astype(F32)
        pooled = acc * (1.0 / cnt) - a_ref[:, cs]
        y = jnp.dot(pooled.astype(BF16), pw_ref[g], preferred_element_type=F32)
        ya_ref[:, cs] = (y * ps_ref[:, cs]).astype(BF16)

    p_prev = dcp_ref[SUBLANE - 1:SUBLANE, :] * dhp_ref[SUBLANE - 1:SUBLANE, :] * pf
    p_next = dcn_ref[0:1, :] * dhn_ref[0:1, :] * nf
    conv = _conv3_rows(dc_ref[...] * dh_ref[...], p_prev, p_next, cw_ref[...], False)
    yd_ref[...] = (db_ref[...] * conv).astype(BF16)


def _pool_conv(l, z, n_rows, pool_w, pool_scale, conv_w):
    ca, cb, cc, ch = (COL_A // GROUP_W, COL_DB // GROUP_W, COL_DC // GROUP_W, COL_DH // GROUP_W)
    out = jax.ShapeDtypeStruct((n_rows, GROUP_W), BF16)
    return pl.pallas_call(
        _pool_conv_kernel,
        grid=(n_rows // TS,),
        in_specs=(_halo_specs(TS, GROUP_W, NT, ca)
                  + [pl.BlockSpec((TS, GROUP_W), lambda i: (i, cb))]
                  + _halo_specs(TS, GROUP_W, NT, cc) + _halo_specs(TS, GROUP_W, NT, ch)
                  + [_layer_spec(l, (len(POOL_WINDOWS), LANE, LANE)),
                     _layer_spec(l, (1, GROUP_W)), _layer_spec(l, (3, GROUP_W))]),
        out_specs=[pl.BlockSpec((TS, GROUP_W), lambda i: (i, 0))] * 2,
        out_shape=[out, out],
        scratch_shapes=[pltpu.VMEM((TS + 16, GROUP_W), F32)],
        compiler_params=_params(("arbitrary",)),
        name="pool_conv",
    )(z, z, z, z, z, z, z, z, z, z, pool_w, pool_scale, conv_w)


CBLK = NX // CTX_LEN


def _softmax_pv(s, v):
    e = jnp.exp(s - jnp.max(s, axis=-1, keepdims=True)).astype(BF16)
    return jnp.dot(e, v, preferred_element_type=F32)


def _with_ones(v):
    return jnp.concatenate([v, jnp.ones(v.shape, v.dtype)], axis=-1)


def _cast_rows(w_ref, wb_ref):
    wb_ref[...] = w_ref[...].astype(BF16)


def _cast_rows_pair_chunks(w_ref, wb_ref):
    for k in range(D_FF // TF):
        for part in range(2):
            src = part * D_FF + k * TF
            dst = (2 * k + part) * TF
            wb_ref[:, dst:dst + TF] = w_ref[:, src:src + TF].astype(BF16)


def _attn_calls(lat_fn, ctx_fn, name, lat_ins, lat_specs, ctx_ins, ctx_specs, scratch, with_ctx,
                l, w_f32, cast_fn):
    steps = BATCH * HEADS
    w_rows, w_cols = w_f32.shape[1] // steps, w_f32.shape[2]

    def lat_body(*refs):
        n = len(lat_ins)
        w_ref, (o_ref, wb_ref), rest = refs[n], refs[-len(scratch) - 2:-len(scratch)], refs[-len(scratch):]
        cast_fn(w_ref, wb_ref)
        lat_fn(*refs[:n], o_ref, *rest)

    y0 = [jnp.zeros((NT, GROUP_W), BF16)] if with_ctx else []
    y_shape = jax.ShapeDtypeStruct((NT if with_ctx else NX, GROUP_W), BF16)
    y, w_b = pl.pallas_call(
        lat_body,
        grid=(BATCH, HEADS),
        in_specs=(lat_specs
                  + [pl.BlockSpec((None, w_rows, w_cols), lambda b, h: (l, b * HEADS + h, 0))]
                  + [pl.BlockSpec(memory_space=pl.ANY)] * len(y0)),
        out_specs=[pl.BlockSpec((SEQ, LANE), lambda b, h: (b, h)),
                   pl.BlockSpec((w_rows, w_cols), lambda b, h: (b * HEADS + h, 0))],
        out_shape=[y_shape, jax.ShapeDtypeStruct(w_f32.shape[1:], BF16)],
        input_output_aliases={len(lat_ins) + 1: 0} if with_ctx else {},
        scratch_shapes=scratch,
        compiler_params=_params(("arbitrary", "arbitrary")),
        name=name,
    )(*lat_ins, w_f32, *y0)
    if not with_ctx:
        return y, w_b
    y = pl.pallas_call(
        lambda *refs: ctx_fn(*refs[:len(ctx_ins)], refs[-1]),
        grid=(BATCH, HEADS),
        in_specs=ctx_specs + [pl.BlockSpec(memory_space=pl.ANY)],
        out_specs=pl.BlockSpec((CTX_LEN, LANE), lambda b, h: (CBLK + b, h)),
        out_shape=y_shape,
        input_output_aliases={len(ctx_ins): 0},
        compiler_params=_params(("arbitrary", "arbitrary")),
        name=name + "_ctx",
    )(*ctx_ins, y)
    return y, w_b


def _diff_lambda(lam_ref, lam_init):
    lp = lam_ref[...]
    return (jnp.exp(jnp.sum(lp[0:1] * lp[1:2], keepdims=True))
            - jnp.exp(jnp.sum(lp[2:3] * lp[3:4], keepdims=True)) + lam_init)


def _diff_scores(q, k):
    q = q * (64 ** -0.5)
    lane = lax.broadcasted_iota(jnp.int32, q.shape, 1)
    return (_nt_dot(jnp.where(lane < 64, q, 0.0).astype(BF16), k[...]),
            _nt_dot(jnp.where(lane >= 64, q, 0.0).astype(BF16), k[...]))


def _diff_out(s, v, lam, g, lam_init):
    o1 = _softmax_pv(s[0], v[...])
    o2 = _softmax_pv(s[1], v[...])
    o = o1[:, :LANE] / o1[:, LANE:] - o2[:, :LANE] * (lam / o2[:, LANE:])
    return (_rms(o, g) * (1.0 - lam_init)).astype(BF16)


def _diff_chain(q, k, v, lam, g, lam_init):
    return _diff_out(_diff_scores(q, k), v, lam, g, lam_init)


def _diff_kernel(lam_init, q_ref, kx_ref, kc_ref, vx_ref, vc_ref, cos_ref, sin_ref, lam_ref, g_ref,
                 o_ref, kb_ref, vb_ref):
    kb_ref[0:SEQ, :] = _rope(kx_ref[...], cos_ref[...], sin_ref[...]).astype(BF16)
    kb_ref[SEQ:, :] = kc_ref[...].astype(BF16)
    vb_ref[0:SEQ, 0:LANE] = vx_ref[...].astype(BF16)
    vb_ref[SEQ:, 0:LANE] = vc_ref[...].astype(BF16)
    vb_ref[:, LANE:] = jnp.ones((SEQ + CTX_LEN, LANE), BF16)
    lam = _diff_lambda(lam_ref, lam_init)
    def scores(rows):
        return _diff_scores(_rope(q_ref[rows, :], cos_ref[rows, :], sin_ref[rows, :]), kb_ref)

    chunks = [slice(r, r + QSUB) for r in range(0, SEQ, QSUB)]
    s = scores(chunks[0])
    for c, rows in enumerate(chunks):
        s_next = scores(chunks[c + 1]) if c + 1 < len(chunks) else None
        o_ref[rows, :] = _diff_out(s, vb_ref, lam, g_ref[...], lam_init)
        s = s_next


def _diff_ctx_kernel(lam_init, q_ref, k_ref, v_ref, lam_ref, g_ref, o_ref):
    o_ref[...] = _diff_chain(q_ref[...], k_ref[...].astype(BF16),
                             _with_ones(v_ref[...].astype(BF16)),
                             _diff_lambda(lam_ref, lam_init), g_ref[...], lam_init)


def _diff_attn(l, z, cos_t, sin_t, lam_p, subln_g, lam_init, with_ctx, w_f32):
    cq, ck, cv = COL_Q // LANE, COL_K // LANE, COL_V // LANE

    def xs(col):
        return pl.BlockSpec((SEQ, LANE), lambda b, h: (b, col + h))

    def cs(col):
        return pl.BlockSpec((CTX_LEN, LANE), lambda b, h: (CBLK + b, col + h))

    table = pl.BlockSpec((SEQ, LANE), lambda b, h: (0, 0))
    params = [_layer_spec(l, (4, 64)), _layer_spec(l, (1, LANE))]
    return _attn_calls(
        functools.partial(_diff_kernel, lam_init), functools.partial(_diff_ctx_kernel, lam_init),
        "diff_attn",
        [z, z, z, z, z, cos_t, sin_t, lam_p, subln_g],
        [xs(cq), xs(ck), cs(ck), xs(cv), cs(cv), table, table] + params,
        [z, z, z, lam_p, subln_g],
        [cs(cq), cs(ck), cs(cv)] + params,
        [pltpu.VMEM((SEQ + CTX_LEN, LANE), BF16), pltpu.VMEM((SEQ + CTX_LEN, 2 * LANE), BF16)],
        with_ctx, l, w_f32, _cast_rows_pair_chunks)


def _mla_prep_kernel(ckv_ref, cq_ref, kr_ref, gq_ref, gkv_ref, wq_ref, wkv_ref, cos_ref, sin_ref,
                     qn_ref, qr_ref, kn_ref, v_ref, kr2_ref):
    cos = cos_ref[...]
    sin = sin_ref[...]
    q = jnp.dot(_rms(cq_ref[...], gq_ref[...]).astype(BF16), wq_ref[...],
                preferred_element_type=F32)
    qn_ref[...] = q[:, :GROUP_W].astype(BF16)
    for c in range(0, 2 * LANE, LANE):
        qr_ref[:, c:c + LANE] = _rope(q[:, GROUP_W + c:GROUP_W + c + LANE], cos, sin).astype(BF16)
    kv = jnp.dot(_rms(ckv_ref[...], gkv_ref[...]).astype(BF16), wkv_ref[...],
                 preferred_element_type=F32)
    kn_ref[...] = kv[:, :GROUP_W].astype(BF16)
    v_ref[...] = kv[:, GROUP_W:].astype(BF16)
    kr = _rope(kr_ref[...], cos, sin)
    kr2_ref[...] = (kr + pltpu.roll(kr, 64, 1)).astype(BF16)


def _mla_prep(l, z, gq, gkv, wq, wkv, cos_t, sin_t):
    def out(w):
        return jax.ShapeDtypeStruct((NT, w), BF16)

    def row(w):
        return pl.BlockSpec((TM, w), lambda i: (i, 0))

    rope = pl.BlockSpec((TM, LANE), lambda i: (jnp.where(i * TM < NX, i % (SEQ // TM), SEQ // TM), 0))
    return pl.pallas_call(
        _mla_prep_kernel,
        grid=(NT // TM,),
        in_specs=[
            pl.BlockSpec((TM, 256), lambda i: (i, COL_CKV // 256)),
            pl.BlockSpec((TM, 384), lambda i: (i, COL_CQ // 384)),
            pl.BlockSpec((TM, LANE), lambda i: (i, COL_KR // LANE)),
            _layer_spec(l, (1, 384)),
            _layer_spec(l, (1, 256)),
            _layer_spec(l, (384, 768)),
            _layer_spec(l, (256, 1024)),
            rope, rope,
        ],
        out_specs=[row(GROUP_W), row(2 * LANE), row(GROUP_W), row(GROUP_W), row(LANE)],
        out_shape=[out(GROUP_W), out(2 * LANE), out(GROUP_W), out(GROUP_W), out(LANE)],
        compiler_params=_params(("arbitrary",)),
        name="mla_prep",
    )(z, z, z, gq, gkv, wq, wkv, cos_t, sin_t)


def _mla_scores(h, qn, qr, k):
    lane = lax.broadcasted_iota(jnp.int32, qr.shape, 1)
    lo = (h % 2) * 64
    qr = jnp.where((lane >= lo) & (lane < lo + 64), qr, jnp.zeros_like(qr))
    return _nt_dot(jnp.concatenate([qn, qr], axis=-1), k[...]) * ((128 + 64) ** -0.5)


def _mla_out(s, v):
    o = _softmax_pv(s, v[...])
    return (o[:, :LANE] / o[:, LANE:]).astype(BF16)


def _mla_chain(h, qn, qr, k, v):
    return _mla_out(_mla_scores(h, qn, qr, k), v)


def _mla_kernel(qn_ref, qr_ref, knx_ref, knc_ref, krx_ref, krc_ref, vx_ref, vc_ref,
                o_ref, kb_ref, vb_ref):
    h = pl.program_id(1)
    kb_ref[0:SEQ, 0:LANE] = knx_ref[...]
    kb_ref[SEQ:, 0:LANE] = knc_ref[...]
    kb_ref[0:SEQ, LANE:] = krx_ref[...]
    kb_ref[SEQ:, LANE:] = krc_ref[...]
    vb_ref[0:SEQ, 0:LANE] = vx_ref[...]
    vb_ref[SEQ:, 0:LANE] = vc_ref[...]
    vb_ref[:, LANE:] = jnp.ones((SEQ + CTX_LEN, LANE), BF16)
    def scores(rows):
        return _mla_scores(h, qn_ref[rows, :], qr_ref[rows, :], kb_ref)

    chunks = [slice(r, r + QSUB) for r in range(0, SEQ, QSUB)]
    s = scores(chunks[0])
    for c, rows in enumerate(chunks):
        s_next = scores(chunks[c + 1]) if c + 1 < len(chunks) else None
        o_ref[rows, :] = _mla_out(s, vb_ref)
        s = s_next


def _mla_ctx_kernel(qn_ref, qr_ref, kn_ref, kr_ref, v_ref, o_ref):
    k = jnp.concatenate([kn_ref[...], kr_ref[...]], axis=-1)
    o_ref[...] = _mla_chain(pl.program_id(1), qn_ref[...], qr_ref[...], k, _with_ones(v_ref[...]))


def _mla_attn(l, qn, qr, kn, v, kr2, with_ctx, w_f32):
    def xs(col):
        return pl.BlockSpec((SEQ, LANE), lambda b, h: (b, col(h)))

    def cs(col):
        return pl.BlockSpec((CTX_LEN, LANE), lambda b, h: (CBLK + b, col(h)))

    head, pair, first = (lambda h: h), (lambda h: h // 2), (lambda h: 0)
    return _attn_calls(
        _mla_kernel, _mla_ctx_kernel, "mla_attn",
        [qn, qr, kn, kn, kr2, kr2, v, v],
        [xs(head), xs(pair), xs(head), cs(head), xs(first), cs(first), xs(head), cs(head)],
        [qn, qr, kn, kr2, v],
        [cs(head), cs(pair), cs(head), cs(first), cs(head)],
        [pltpu.VMEM((SEQ + CTX_LEN, 2 * LANE), BF16), pltpu.VMEM((SEQ + CTX_LEN, 2 * LANE), BF16)],
        with_ctx, l, w_f32, _cast_rows)


def _out_kernel(split, ya_ref, yb_ref, ym_ref, yd_ref, w_ref, x_ref, *refs):
    c_ref, (mod_ref, gpost_ref, xo_ref, wb_ref) = (refs[0], refs[1:]) if split else (None, refs)

    @pl.when(pl.program_id(0) == 0)
    def _():
        wb_ref[...] = w_ref[...].astype(BF16)

    y = jnp.dot(ya_ref[...], wb_ref[0:GROUP_W, :], preferred_element_type=F32)
    y += jnp.dot(yb_ref[...], wb_ref[GROUP_W:2 * GROUP_W, :], preferred_element_type=F32)
    y += jnp.dot(ym_ref[...], wb_ref[2 * GROUP_W:3 * GROUP_W, :], preferred_element_type=F32)
    y += jnp.dot(yd_ref[...], wb_ref[3 * GROUP_W:, :], preferred_element_type=F32)
    delta = mod_ref[2:3, :] * _rms(y, gpost_ref[...])

    def add_residual(src):
        xo_ref[...] = src[...] + delta

    _pick_rows(pl.program_id(0), TM, x_ref, c_ref, add_residual)


def _out_proj(l, ya, yb, ym, yd, w, toks, mods, g_post):
    n = ya.shape[0]
    split = len(toks) == 2
    y_spec = pl.BlockSpec((TM, GROUP_W), lambda i: (i, 0))
    row = pl.BlockSpec((TM, D_MODEL), lambda i: (i, 0))
    return pl.pallas_call(
        functools.partial(_out_kernel, split),
        grid=(n // TM,),
        in_specs=[
            y_spec, y_spec, y_spec, y_spec,
            pl.BlockSpec((None, D_MODEL, D_MODEL), lambda i: (l, 0, 0),
                         pipeline_mode=pl.Buffered(1)),
        ] + (_split_rows_specs(TM, NX) if split else [row]) + [
            _mod_spec(l, TM),
            _layer_spec(l, (1, D_MODEL)),
        ],
        out_specs=row,
        out_shape=jax.ShapeDtypeStruct((n, D_MODEL), F32),
        scratch_shapes=[pltpu.VMEM((D_MODEL, D_MODEL), BF16)],
        compiler_params=_params(("arbitrary",)),
        name="out_proj",
    )(ya, yb, ym, yd, w, *toks, mods, g_post)


def _ffn_kernel(x_ref, xp_ref, xn_ref, mod_ref, gpre_ref, gpost_ref, wgv_ref, cg_ref, cv_ref,
                wd_ref, xo_ref, hx_ref, r_ref):
    i = pl.program_id(0)
    j = pl.program_id(1)
    n = TM_BIG

    @pl.when(j == 0)
    def _():
        slab_shape = (BF16_ROWS, D_MODEL)
        g = jnp.broadcast_to(gpre_ref[...], slab_shape)
        scale = jnp.broadcast_to(1.0 + mod_ref[4:5, :], slab_shape)
        shift = jnp.broadcast_to(mod_ref[3:4, :], slab_shape)

        def hmod(x, r):
            return x * r * g * scale + shift

        def rinv(x):
            return lax.rsqrt(jnp.mean(x * x, axis=-1, keepdims=True) + NORM_EPS)

        _store_rinv(r_ref, x_ref[...])

        def slab(k, carry):
            rows = pl.ds(pl.multiple_of(k * BF16_ROWS, BF16_ROWS), BF16_ROWS)
            dst = pl.ds(pl.multiple_of((k + 1) * BF16_ROWS, BF16_ROWS), BF16_ROWS)
            hx_ref[dst, :] = hmod(x_ref[rows, :], _load_rinv(r_ref, rows)).astype(BF16)
            return carry

        lax.fori_loop(0, n // BF16_ROWS, slab, 0, unroll=4)
        has_prev, has_next = _tile_edges(i, n)
        halo = jnp.concatenate([xp_ref[...], xn_ref[...]], axis=0)
        keep = jnp.where(lax.broadcasted_iota(jnp.int32, (BF16_ROWS, 1), 0) < SUBLANE,
                         has_prev, has_next)
        halo = hmod(halo, rinv(halo)) * keep
        zeros = jnp.zeros((SUBLANE, D_MODEL), F32)
        hx_ref[:BF16_ROWS, :] = jnp.concatenate([zeros, halo[:SUBLANE]], axis=0).astype(BF16)
        hx_ref[BF16_ROWS + n:, :] = jnp.concatenate([halo[SUBLANE:], zeros], axis=0).astype(BF16)
        xo_ref[...] = jnp.zeros_like(xo_ref)

    is_ctx = i * n >= NX
    same_seq = jnp.where(is_ctx, 0.0, 1.0)
    half = n // 2
    us = [jnp.dot(hx_ref[r0:r0 + half + 2 * BF16_ROWS, :], wgv_ref[...],
                  preferred_element_type=F32) for r0 in (0, half)]
    for part, u in enumerate(us):
        r0 = part * half
        prev_row = u[BF16_ROWS - 1:BF16_ROWS]
        next_row = u[BF16_ROWS + half:BF16_ROWS + half + 1]
        if part == 0:
            next_row = next_row * same_seq
        else:
            prev_row = prev_row * same_seq
        main = u[BF16_ROWS:BF16_ROWS + half]

        def conv(cols, c_ref):
            return _conv3_rows(main[:, cols], prev_row[:, cols], next_row[:, cols], c_ref[...],
                               is_ctx)

        gate = conv(slice(0, TF), cg_ref)
        val = conv(slice(TF, 2 * TF), cv_ref)
        act = (gate * jax.nn.sigmoid(gate) * val).astype(BF16)
        xo_ref[r0:r0 + half, :] += jnp.dot(act, wd_ref[...], preferred_element_type=F32)

    @pl.when(j == pl.num_programs(1) - 1)
    def _():
        _store_rinv(r_ref, xo_ref[...])
        slab_shape = (BF16_ROWS, D_MODEL)
        g = jnp.broadcast_to(gpost_ref[...], slab_shape)
        res_gate = jnp.broadcast_to(mod_ref[5:6, :], slab_shape)

        def slab(k, carry):
            rows = pl.ds(pl.multiple_of(k * BF16_ROWS, BF16_ROWS), BF16_ROWS)
            normed = xo_ref[rows, :] * _load_rinv(r_ref, rows) * g
            xo_ref[rows, :] = x_ref[rows, :] + res_gate * normed
            return carry

        lax.fori_loop(0, n // BF16_ROWS, slab, 0, unroll=4)


def _ffn(l, tok, mods, g_pre, g_post, w_up, conv_w, w_down, n_rows):
    nj = D_FF // TF
    return pl.pallas_call(
        _ffn_kernel,
        grid=(n_rows // TM_BIG, nj),
        in_specs=(_halo_specs(TM_BIG, D_MODEL, n_rows, 0) + [
            _mod_spec(l, TM_BIG),
            _layer_spec(l, (1, D_MODEL)),
            _layer_spec(l, (1, D_MODEL)),
            pl.BlockSpec((D_MODEL, 2 * TF), lambda i, j: (0, j)),
            pl.BlockSpec((None, 3, TF), lambda i, j: (l, 0, j)),
            pl.BlockSpec((None, 3, TF), lambda i, j: (l, 0, j + nj)),
            pl.BlockSpec((TF, D_MODEL), lambda i, j: (j, 0)),
        ]),
        out_specs=pl.BlockSpec((TM_BIG, D_MODEL), lambda i, j: (i, 0)),
        out_shape=jax.ShapeDtypeStruct((n_rows, D_MODEL), F32),
        scratch_shapes=[pltpu.VMEM((TM_BIG + 2 * BF16_ROWS, D_MODEL), BF16),
                        pltpu.VMEM((TM_BIG, LANE), F32)],
        compiler_params=_params(("arbitrary", "arbitrary"), VMEM_LIMIT_BIG),
        name="ffn",
    )(tok, tok, tok, mods, g_pre, g_post, w_up, conv_w, conv_w, w_down)


def _rope_tables():
    rows = SEQ // GRID_W
    row = jnp.repeat(jnp.arange(rows), GRID_W).astype(F32)
    col = jnp.tile(jnp.arange(GRID_W), rows).astype(F32)
    n_freq = 64 // 4
    inv = ROPE_THETA ** (-jnp.arange(n_freq, dtype=F32) / n_freq)
    ang = jnp.concatenate([row[:, None] * inv, col[:, None] * inv], axis=-1)
    cos, sin = jnp.cos(ang), jnp.sin(ang)
    cos_t = jnp.concatenate([cos, cos, cos, cos], axis=-1)
    sin_t = jnp.concatenate([-sin, sin, -sin, sin], axis=-1)
    cos_t = jnp.concatenate([cos_t, jnp.ones((TM, LANE), F32)], axis=0)
    sin_t = jnp.concatenate([sin_t, jnp.zeros((TM, LANE), F32)], axis=0)
    return cos_t, sin_t


W_IN_KB = 256


def _w_in_kernel(w_ref, o_ref):
    for dst, src, n in ((0, 0, 2048), (COL_DB, 2752, 1536), (COL_CKV, 2432, 256),
                        (COL_CQ, 2048, 384), (COL_KR, 2688, 64)):
        o_ref[dst:dst + n, :] = w_ref[src:src + n, :].astype(BF16)
    o_ref[COL_KR + 64:, :] = jnp.zeros((IN_COLS_PAD - COL_KR - 64, W_IN_KB), BF16)


def _layout_w_in(w):
    wt = jnp.swapaxes(w, 1, 2)
    n, k = wt.shape[1:]
    return pl.pallas_call(
        _w_in_kernel,
        grid=(DEPTH, k // W_IN_KB),
        in_specs=[pl.BlockSpec((None, n, W_IN_KB), lambda l, i: (l, 0, i))],
        out_specs=pl.BlockSpec((None, IN_COLS_PAD, W_IN_KB), lambda l, i: (l, 0, i)),
        out_shape=jax.ShapeDtypeStruct((DEPTH, IN_COLS_PAD, k), BF16),
        compiler_params=_params(("arbitrary", "arbitrary")),
        name="w_in_layout",
    )(wt)


def _layout_w_uq(w):
    w = w.reshape(DEPTH, 384, HEADS, 192)
    return jnp.concatenate([w[..., :128].reshape(DEPTH, 384, 512),
                            w[..., 128:].reshape(DEPTH, 384, 256)], axis=-1).astype(BF16)


def _layout_w_ukv(w):
    w = w.reshape(DEPTH, 256, HEADS, 256)
    return jnp.concatenate([w[..., :128].reshape(DEPTH, 256, 512),
                            w[..., 128:].reshape(DEPTH, 256, 512)], axis=-1).astype(BF16)


def kernel(x, c, ctx, c_ctx, ada_w, ada_b, g_pre_mix, g_post_mix, g_pre_ffn, g_post_ffn, w_in,
           pool_w, pool_scale, diff_lambda, diff_subln_g, mla_gq, mla_w_uq, mla_gkv, mla_w_ukv,
           conv_w, w_out, ffn_w_up, ffn_conv_w, ffn_w_down):
    toks = (x.reshape(NX, D_MODEL), ctx.reshape(NC, D_MODEL))
    cc = jnp.concatenate([c, c_ctx[None, :], jnp.zeros((3, D_MODEL), F32)], axis=0)
    mods = _ada(cc, ada_w, ada_b).reshape(DEPTH, 8, 6, D_MODEL)
    cos_t, sin_t = _rope_tables()

    def vec(p):
        return p.reshape(DEPTH, 1, p.shape[-1])

    w_in_b = _layout_w_in(w_in)
    w_uq_b, w_ukv_b = _layout_w_uq(mla_w_uq), _layout_w_ukv(mla_w_ukv)
    pool_w_b = pool_w.astype(BF16)

    for l in range(DEPTH):
        last = l == DEPTH - 1
        n_rows = NX if last else NT
        lam_init = 0.8 - 0.6 * math.exp(-0.3 * l)

        z = _in_proj(l, toks, mods, vec(g_pre_mix), w_in_b)
        ya, yd = _pool_conv(l, z, n_rows, pool_w_b, vec(pool_scale), conv_w)
        yb, w_up_b = _diff_attn(l, z, cos_t, sin_t, diff_lambda, vec(diff_subln_g), lam_init,
                                not last, ffn_w_up)
        qn, qr, kn, v, kr2 = _mla_prep(l, z, vec(mla_gq), vec(mla_gkv), w_uq_b, w_ukv_b, cos_t, sin_t)
        ym, w_down_b = _mla_attn(l, qn, qr, kn, v, kr2, not last, ffn_w_down)
        tok = _out_proj(l, ya, yb, ym, yd, w_out, toks, mods, vec(g_post_mix))
        tok = _ffn(l, tok, mods, vec(g_pre_ffn), vec(g_post_ffn), w_up_b, ffn_conv_w, w_down_b, n_rows)
        toks = (tok,)
    return tok.reshape(BATCH, SEQ, D_MODEL)
```

```python
import functools
import math

import jax
import jax.numpy as jnp
from jax import lax
from jax.experimental import pallas as pl
from jax.experimental.pallas import tpu as pltpu

F32 = jnp.float32
BF16 = jnp.bfloat16

D_MODEL = 2048
BATCH = 4
SEQ = 2048
CTX_LEN = 256
GRID_W = 64
DEPTH = 2
NORM_EPS = 1e-6
ROPE_THETA = 10000.0
GROUP_W = 512
POOL_WINDOWS = (2, 4, 8, 16)
HEADS = 4
D_FF = 5632
ROPE_DIM = 64
DIFF_HD = 64
MLA_NOPE, MLA_VD = 128, 128
MLA_Q_LORA, MLA_KV_LORA = 384, 256
REF_SEGMENTS = (("pool_qkv", 2048), ("c_q", MLA_Q_LORA), ("c_kv", MLA_KV_LORA),
                ("k_rope", ROPE_DIM), ("conv", 1536))

NX = BATCH * SEQ
NC = BATCH * CTX_LEN
NT = NX + NC

COL_A, COL_Q, COL_K, COL_V = 0, 512, 1024, 1536
COL_DB, COL_DC, COL_DH = 2048, 2560, 3072
COL_CKV, COL_CQ, COL_KR = 3584, 3840, 4224
IN_COLS_PAD = 4352

LANE = 128
SUBLANE = 8
BF16_ROWS = 2 * SUBLANE
TM = 512
TM_BIG = 1024
TS = 256
QSUB = 256
TF = 512
TN_ADA = 1024
VMEM_LIMIT = 56 * 1024 * 1024
VMEM_LIMIT_BIG = 62 * 1024 * 1024


def _params(sem, vmem_limit=VMEM_LIMIT):
    return pltpu.CompilerParams(dimension_semantics=sem, vmem_limit_bytes=vmem_limit)


def _split_rows_specs(rows, n_lat_rows):
    n_lat = n_lat_rows // rows
    return [pl.BlockSpec((rows, D_MODEL), lambda i, *_: (jnp.minimum(i, n_lat - 1), 0)),
            pl.BlockSpec((rows, D_MODEL), lambda i, *_: (jnp.maximum(i - n_lat, 0), 0),
                         pipeline_mode=pl.Buffered(1))]


def _pick_rows(i, rows, x_ref, c_ref, fn):
    if c_ref is None:
        fn(x_ref)
        return

    @pl.when(i * rows < NX)
    def _():
        fn(x_ref)

    @pl.when(i * rows >= NX)
    def _():
        fn(c_ref)


def _rms(x, g):
    ms = jnp.mean(x * x, axis=-1, keepdims=True)
    return x * lax.rsqrt(ms + NORM_EPS) * g


def _store_rinv(r_ref, x):
    r = lax.rsqrt(jnp.mean(x * x, axis=-1, keepdims=True) + NORM_EPS)
    r_ref[...] = jnp.broadcast_to(r, r_ref.shape)


def _load_rinv(r_ref, rows):
    return jnp.tile(r_ref[rows, :], (1, D_MODEL // LANE))


def _nt_dot(a, b):
    return lax.dot_general(a, b, (((1,), (1,)), ((), ())), preferred_element_type=F32)


def _rope(v, cos, sin):
    lane = lax.broadcasted_iota(jnp.int32, v.shape, 1)
    half = ROPE_DIM // 2
    sw = jnp.where((lane & half) == 0, pltpu.roll(v, LANE - half, 1), pltpu.roll(v, half, 1))
    return v * cos + sw * sin


def _seq_pos(tile, rows):
    r = tile * rows + lax.broadcasted_iota(jnp.int32, (rows, 1), 0)
    seqlen = jnp.where(r < NX, SEQ, CTX_LEN)
    return r & (seqlen - 1), seqlen


def _tile_edges(tile, rows):
    r0 = tile * rows
    seqlen = jnp.where(r0 < NX, SEQ, CTX_LEN)
    has_prev = jnp.where((r0 & (seqlen - 1)) == 0, 0.0, 1.0)
    has_next = jnp.where(((r0 + rows) & (seqlen - 1)) == 0, 0.0, 1.0)
    return has_prev, has_next


def _conv3_rows(u, prev_row, next_row, c, is_ctx):
    n = u.shape[0]
    row8 = lax.broadcasted_iota(jnp.int32, (SUBLANE, 1), 0)
    prev = pltpu.roll(u, 1, 0)
    prev = jnp.concatenate([jnp.where(row8 == 0, prev_row, prev[:SUBLANE]), prev[SUBLANE:]], axis=0)
    nxt = pltpu.roll(u, n - 1, 0)
    nxt = jnp.concatenate([nxt[:n - SUBLANE],
                           jnp.where(row8 == SUBLANE - 1, next_row, nxt[n - SUBLANE:])], axis=0)
    ctx_row = row8 + jnp.where(is_ctx, 0, 2 * SUBLANE)
    for p in range(CTX_LEN, n, CTX_LEN):
        prev = jnp.concatenate([prev[:p], jnp.where(ctx_row == 0, 0.0, prev[p:p + SUBLANE]),
                                prev[p + SUBLANE:]], axis=0)
        nxt = jnp.concatenate([nxt[:p - SUBLANE],
                               jnp.where(ctx_row == SUBLANE - 1, 0.0, nxt[p - SUBLANE:p]), nxt[p:]],
                              axis=0)
    return c[0:1, :] * prev + c[1:2, :] * u + c[2:3, :] * nxt


def _first_axis(i, *_):
    return i


def _halo_specs(rows, width, n_rows, col, tile=_first_axis):
    r8 = rows // SUBLANE
    last8 = n_rows // SUBLANE - 1
    return [
        pl.BlockSpec((rows, width), lambda *g: (tile(*g), col)),
        pl.BlockSpec((SUBLANE, width), lambda *g: (jnp.maximum(tile(*g) * r8 - 1, 0), col)),
        pl.BlockSpec((SUBLANE, width), lambda *g: (jnp.minimum((tile(*g) + 1) * r8, last8), col)),
    ]


def _layer_spec(l, shape):
    return pl.BlockSpec((None,) + shape, lambda *_: (l,) + (0,) * len(shape))


def _mod_spec(l, rows, tile=_first_axis):
    def index(*g):
        r = tile(*g) * rows
        return (l, jnp.where(r < NX, r // SEQ, BATCH), 0, 0)
    return pl.BlockSpec((None, None, 6, D_MODEL), index)


def _ada_kernel(c_ref, w_ref, b_ref, o_ref):
    c = c_ref[...]
    a = c * jax.nn.sigmoid(c)
    o_ref[...] = jnp.dot(a.astype(BF16), w_ref[...].astype(BF16),
                         preferred_element_type=F32) + b_ref[...]


def _ada(cc, ada_w, ada_b):
    n = ada_w.shape[-1]
    return pl.pallas_call(
        _ada_kernel,
        grid=(DEPTH, n // TN_ADA),
        in_specs=[
            pl.BlockSpec((8, D_MODEL), lambda l, j: (0, 0)),
            pl.BlockSpec((None, D_MODEL, TN_ADA), lambda l, j: (l, 0, j)),
            pl.BlockSpec((None, 1, TN_ADA), lambda l, j: (l, 0, j)),
        ],
        out_specs=pl.BlockSpec((None, 8, TN_ADA), lambda l, j: (l, 0, j)),
        out_shape=jax.ShapeDtypeStruct((DEPTH, 8, n), F32),
        compiler_params=_params(("arbitrary", "arbitrary")),
        name="ada",
    )(cc, ada_w, ada_b.reshape(DEPTH, 1, n))


def _in_kernel(split, x_ref, *refs):
    c_ref, (mod_ref, g_ref, w_ref, z_ref, h_ref, r_ref) = ((refs[0], refs[1:]) if split
                                                              else (None, refs))

    def put(src):
        _store_rinv(r_ref, src[...])

        shape = (BF16_ROWS, D_MODEL)
        g = jnp.broadcast_to(g_ref[...], shape)
        scale = jnp.broadcast_to(1.0 + mod_ref[1:2, :], shape)
        shift = jnp.broadcast_to(mod_ref[0:1, :], shape)

        def slab(k, carry):
            rows = pl.ds(pl.multiple_of(k * BF16_ROWS, BF16_ROWS), BF16_ROWS)
            y = src[rows, :] * _load_rinv(r_ref, rows) * g
            h_ref[rows, :] = (y * scale + shift).astype(BF16)
            return carry

        lax.fori_loop(0, TM // BF16_ROWS, slab, 0, unroll=4)

    _pick_rows(pl.program_id(0), TM, x_ref, c_ref, put)
    z_ref[...] = _nt_dot(h_ref[...], w_ref[...])


def _in_proj(l, toks, mods, g, w):
    split = len(toks) == 2
    rows = (_split_rows_specs(TM, NX) if split
            else [pl.BlockSpec((TM, D_MODEL), lambda i: (i, 0))])
    return pl.pallas_call(
        functools.partial(_in_kernel, split),
        grid=(NT // TM,),
        in_specs=rows + [
            _mod_spec(l, TM),
            _layer_spec(l, (1, D_MODEL)),
            pl.BlockSpec((None, IN_COLS_PAD, D_MODEL), lambda i: (l, 0, 0),
                         pipeline_mode=pl.Buffered(1)),
        ],
        out_specs=pl.BlockSpec((TM, IN_COLS_PAD), lambda i: (i, 0)),
        out_shape=jax.ShapeDtypeStruct((NT, IN_COLS_PAD), F32),
        scratch_shapes=[pltpu.VMEM((TM, D_MODEL), BF16), pltpu.VMEM((TM, LANE), F32)],
        compiler_params=_params(("arbitrary",), VMEM_LIMIT_BIG),
        name="in_proj",
    )(*toks, mods, g, w)


def _pool_conv_kernel(a_ref, ap_ref, an_ref, db_ref, dc_ref, dcp_ref, dcn_ref,
                      dh_ref, dhp_ref, dhn_ref, pw_ref, ps_ref, cw_ref,
                      ya_ref, yd_ref, e_ref):
    i = pl.program_id(0)
    pos, seqlen = _seq_pos(i, TS)
    pf, nf = _tile_edges(i, TS)

    halo = SUBLANE
    e_ref[0:halo, :] = ap_ref[...] * pf
    e_ref[halo:halo + TS, :] = a_ref[...]
    e_ref[halo + TS:, :] = an_ref[...] * nf
    for g, w in enumerate(POOL_WINDOWS):
        half = w // 2
        cs = slice(g * LANE, (g + 1) * LANE)
        c, s = e_ref[:, cs], 1
        while s < w:
            c = c + pltpu.roll(c, TS + 2 * halo - s, 0)
            s *= 2
        acc = c[halo - half:halo - half + TS]
        cnt = (jnp.minimum(pos + half, seqlen) - jnp.maximum(pos - half, 0)).astype(F32)
        pooled = acc * (1.0 / cnt) - a_ref[:, cs]
        y = jnp.dot(pooled.astype(BF16), pw_ref[g], preferred_element_type=F32)
        ya_ref[:, cs] = (y * ps_ref[:, cs]).astype(BF16)

    p_prev = dcp_ref[SUBLANE - 1:SUBLANE, :] * dhp_ref[SUBLANE - 1:SUBLANE, :] * pf
    p_next = dcn_ref[0:1, :] * dhn_ref[0:1, :] * nf
    conv = _conv3_rows(dc_ref[...] * dh_ref[...], p_prev, p_next, cw_ref[...], False)
    yd_ref[...] = (db_ref[...] * conv).astype(BF16)


def _pool_conv(l, z, n_rows, pool_w, pool_scale, conv_w):
    ca, cb, cc, ch = (COL_A // GROUP_W, COL_DB // GROUP_W, COL_DC // GROUP_W, COL_DH // GROUP_W)
    out = jax.ShapeDtypeStruct((n_rows, GROUP_W), BF16)
    return pl.pallas_call(
        _pool_conv_kernel,
        grid=(n_rows // TS,),
        in_specs=(_halo_specs(TS, GROUP_W, NT, ca)
                  + [pl.BlockSpec((TS, GROUP_W), lambda i: (i, cb))]
                  + _halo_specs(TS, GROUP_W, NT, cc) + _halo_specs(TS, GROUP_W, NT, ch)
                  + [_layer_spec(l, (len(POOL_WINDOWS), LANE, LANE)),
                     _layer_spec(l, (1, GROUP_W)), _layer_spec(l, (3, GROUP_W))]),
        out_specs=[pl.BlockSpec((TS, GROUP_W), lambda i: (i, 0))] * 2,
        out_shape=[out, out],
        scratch_shapes=[pltpu.VMEM((TS + 2 * SUBLANE, GROUP_W), F32)],
        compiler_params=_params(("arbitrary",)),
        name="pool_conv",
    )(z, z, z, z, z, z, z, z, z, z, pool_w, pool_scale, conv_w)


CBLK = NX // CTX_LEN


def _softmax_pv(s, v):
    e = jnp.exp(s - jnp.max(s, axis=-1, keepdims=True)).astype(BF16)
    return jnp.dot(e, v, preferred_element_type=F32)


def _with_ones(v):
    return jnp.concatenate([v, jnp.ones(v.shape, v.dtype)], axis=-1)


def _cast_rows(w_ref, wb_ref):
    wb_ref[...] = w_ref[...].astype(BF16)


def _cast_rows_pair_chunks(w_ref, wb_ref):
    for k in range(D_FF // TF):
        for part in range(2):
            src = part * D_FF + k * TF
            dst = (2 * k + part) * TF
            wb_ref[:, dst:dst + TF] = w_ref[:, src:src + TF].astype(BF16)


def _attn_calls(lat_fn, ctx_fn, name, lat_ins, lat_specs, ctx_ins, ctx_specs, scratch, with_ctx,
                l, w_f32, cast_fn):
    steps = BATCH * HEADS
    w_rows, w_cols = w_f32.shape[1] // steps, w_f32.shape[2]

    def lat_body(*refs):
        n = len(lat_ins)
        w_ref, (o_ref, wb_ref), rest = refs[n], refs[-len(scratch) - 2:-len(scratch)], refs[-len(scratch):]
        cast_fn(w_ref, wb_ref)
        lat_fn(*refs[:n], o_ref, *rest)

    y0 = [jnp.zeros((NT, GROUP_W), BF16)] if with_ctx else []
    y_shape = jax.ShapeDtypeStruct((NT if with_ctx else NX, GROUP_W), BF16)
    y, w_b = pl.pallas_call(
        lat_body,
        grid=(BATCH, HEADS),
        in_specs=(lat_specs
                  + [pl.BlockSpec((None, w_rows, w_cols), lambda b, h: (l, b * HEADS + h, 0))]
                  + [pl.BlockSpec(memory_space=pl.ANY)] * len(y0)),
        out_specs=[pl.BlockSpec((SEQ, LANE), lambda b, h: (b, h)),
                   pl.BlockSpec((w_rows, w_cols), lambda b, h: (b * HEADS + h, 0))],
        out_shape=[y_shape, jax.ShapeDtypeStruct(w_f32.shape[1:], BF16)],
        input_output_aliases={len(lat_ins) + 1: 0} if with_ctx else {},
        scratch_shapes=scratch,
        compiler_params=_params(("arbitrary", "arbitrary")),
        name=name,
    )(*lat_ins, w_f32, *y0)
    if not with_ctx:
        return y, w_b
    y = pl.pallas_call(
        lambda *refs: ctx_fn(*refs[:len(ctx_ins)], refs[-1]),
        grid=(BATCH,),
        in_specs=ctx_specs + [pl.BlockSpec(memory_space=pl.ANY)],
        out_specs=pl.BlockSpec((CTX_LEN, GROUP_W), lambda b: (CBLK + b, 0)),
        out_shape=y_shape,
        input_output_aliases={len(ctx_ins): 0},
        compiler_params=_params(("arbitrary",)),
        name=name + "_ctx",
    )(*ctx_ins, y)
    return y, w_b


def _diff_lambda(lam_ref, lam_init):
    lp = lam_ref[...]
    return (jnp.exp(jnp.sum(lp[0:1] * lp[1:2], keepdims=True))
            - jnp.exp(jnp.sum(lp[2:3] * lp[3:4], keepdims=True)) + lam_init)


def _diff_scores(q, k):
    q = q * (DIFF_HD ** -0.5)
    lane = lax.broadcasted_iota(jnp.int32, q.shape, 1)
    return (_nt_dot(jnp.where(lane < DIFF_HD, q, 0.0).astype(BF16), k[...]),
            _nt_dot(jnp.where(lane >= DIFF_HD, q, 0.0).astype(BF16), k[...]))


def _diff_out(s, v, lam, g, lam_init):
    o1 = _softmax_pv(s[0], v[...])
    o2 = _softmax_pv(s[1], v[...])
    o = o1[:, :LANE] / o1[:, LANE:] - o2[:, :LANE] * (lam / o2[:, LANE:])
    return (_rms(o, g) * (1.0 - lam_init)).astype(BF16)


def _diff_chain(q, k, v, lam, g, lam_init):
    return _diff_out(_diff_scores(q, k), v, lam, g, lam_init)


def _diff_kernel(lam_init, q_ref, kx_ref, kc_ref, vx_ref, vc_ref, cos_ref, sin_ref, lam_ref, g_ref,
                 o_ref, kb_ref, vb_ref):
    kb_ref[0:SEQ, :] = _rope(kx_ref[...], cos_ref[...], sin_ref[...]).astype(BF16)
    kb_ref[SEQ:, :] = kc_ref[...].astype(BF16)
    vb_ref[0:SEQ, 0:LANE] = vx_ref[...].astype(BF16)
    vb_ref[SEQ:, 0:LANE] = vc_ref[...].astype(BF16)
    vb_ref[:, LANE:] = jnp.ones((SEQ + CTX_LEN, LANE), BF16)
    lam = _diff_lambda(lam_ref, lam_init)
    def scores(rows):
        return _diff_scores(_rope(q_ref[rows, :], cos_ref[rows, :], sin_ref[rows, :]), kb_ref)

    chunks = [slice(r, r + QSUB) for r in range(0, SEQ, QSUB)]
    s = scores(chunks[0])
    for c, rows in enumerate(chunks):
        s_next = scores(chunks[c + 1]) if c + 1 < len(chunks) else None
        o_ref[rows, :] = _diff_out(s, vb_ref, lam, g_ref[...], lam_init)
        s = s_next


def _diff_ctx_kernel(lam_init, q_ref, k_ref, v_ref, lam_ref, g_ref, o_ref):
    lam = _diff_lambda(lam_ref, lam_init)
    for h in range(HEADS):
        cols = slice(h * LANE, (h + 1) * LANE)
        o_ref[:, cols] = _diff_chain(q_ref[:, cols], k_ref[:, cols].astype(BF16),
                                     _with_ones(v_ref[:, cols].astype(BF16)), lam, g_ref[...],
                                     lam_init)


def _diff_attn(l, z, cos_t, sin_t, lam_p, subln_g, lam_init, with_ctx, w_f32):
    cq, ck, cv = COL_Q // LANE, COL_K // LANE, COL_V // LANE

    def xs(col):
        return pl.BlockSpec((SEQ, LANE), lambda b, h: (b, col + h))

    def cs(col):
        return pl.BlockSpec((CTX_LEN, LANE), lambda b, h: (CBLK + b, col + h))

    def all_heads(col):
        return pl.BlockSpec((CTX_LEN, GROUP_W), lambda b: (CBLK + b, col // GROUP_W))

    table = pl.BlockSpec((SEQ, LANE), lambda b, h: (0, 0))
    params = [_layer_spec(l, (4, 64)), _layer_spec(l, (1, LANE))]
    return _attn_calls(
        functools.partial(_diff_kernel, lam_init), functools.partial(_diff_ctx_kernel, lam_init),
        "diff_attn",
        [z, z, z, z, z, cos_t, sin_t, lam_p, subln_g],
        [xs(cq), xs(ck), cs(ck), xs(cv), cs(cv), table, table] + params,
        [z, z, z, lam_p, subln_g],
        [all_heads(COL_Q), all_heads(COL_K), all_heads(COL_V)] + params,
        [pltpu.VMEM((SEQ + CTX_LEN, LANE), BF16), pltpu.VMEM((SEQ + CTX_LEN, 2 * LANE), BF16)],
        with_ctx, l, w_f32, _cast_rows_pair_chunks)


def _mla_prep_kernel(ckv_ref, cq_ref, kr_ref, gq_ref, gkv_ref, wq_ref, wkv_ref, cos_ref, sin_ref,
                     qn_ref, qr_ref, kn_ref, v_ref, kr2_ref):
    cos = cos_ref[...]
    sin = sin_ref[...]
    q = jnp.dot(_rms(cq_ref[...], gq_ref[...]).astype(BF16), wq_ref[...],
                preferred_element_type=F32)
    qn_ref[...] = q[:, :GROUP_W].astype(BF16)
    for c in range(0, 2 * LANE, LANE):
        qr_ref[:, c:c + LANE] = _rope(q[:, GROUP_W + c:GROUP_W + c + LANE], cos, sin).astype(BF16)
    kv = jnp.dot(_rms(ckv_ref[...], gkv_ref[...]).astype(BF16), wkv_ref[...],
                 preferred_element_type=F32)
    kn_ref[...] = kv[:, :GROUP_W].astype(BF16)
    v_ref[...] = kv[:, GROUP_W:].astype(BF16)
    kr = _rope(kr_ref[...], cos, sin)
    kr2_ref[...] = (kr + pltpu.roll(kr, ROPE_DIM, 1)).astype(BF16)


def _mla_prep(l, z, gq, gkv, wq, wkv, cos_t, sin_t):
    def out(w):
        return jax.ShapeDtypeStruct((NT, w), BF16)

    def row(w):
        return pl.BlockSpec((TM, w), lambda i: (i, 0))

    rope = pl.BlockSpec((TM, LANE), lambda i: (jnp.where(i * TM < NX, i % (SEQ // TM), SEQ // TM), 0))
    return pl.pallas_call(
        _mla_prep_kernel,
        grid=(NT // TM,),
        in_specs=[
            pl.BlockSpec((TM, MLA_KV_LORA), lambda i: (i, COL_CKV // MLA_KV_LORA)),
            pl.BlockSpec((TM, MLA_Q_LORA), lambda i: (i, COL_CQ // MLA_Q_LORA)),
            pl.BlockSpec((TM, LANE), lambda i: (i, COL_KR // LANE)),
            _layer_spec(l, (1, MLA_Q_LORA)),
            _layer_spec(l, (1, MLA_KV_LORA)),
            _layer_spec(l, (MLA_Q_LORA, HEADS * (MLA_NOPE + ROPE_DIM))),
            _layer_spec(l, (MLA_KV_LORA, HEADS * (MLA_NOPE + MLA_VD))),
            rope, rope,
        ],
        out_specs=[row(GROUP_W), row(2 * LANE), row(GROUP_W), row(GROUP_W), row(LANE)],
        out_shape=[out(GROUP_W), out(2 * LANE), out(GROUP_W), out(GROUP_W), out(LANE)],
        compiler_params=_params(("arbitrary",)),
        name="mla_prep",
    )(z, z, z, gq, gkv, wq, wkv, cos_t, sin_t)


def _mla_scores(h, qn, qr, k):
    lane = lax.broadcasted_iota(jnp.int32, qr.shape, 1)
    lo = (h % 2) * ROPE_DIM
    qr = jnp.where((lane >= lo) & (lane < lo + ROPE_DIM), qr, jnp.zeros_like(qr))
    return _nt_dot(jnp.concatenate([qn, qr], axis=-1), k[...]) * ((MLA_NOPE + ROPE_DIM) ** -0.5)


def _mla_out(s, v):
    o = _softmax_pv(s, v[...])
    return (o[:, :LANE] / o[:, LANE:]).astype(BF16)


def _mla_chain(h, qn, qr, k, v):
    return _mla_out(_mla_scores(h, qn, qr, k), v)


def _mla_kernel(qn_ref, qr_ref, knx_ref, knc_ref, krx_ref, krc_ref, vx_ref, vc_ref,
                o_ref, kb_ref, vb_ref):
    h = pl.program_id(1)
    kb_ref[0:SEQ, 0:LANE] = knx_ref[...]
    kb_ref[SEQ:, 0:LANE] = knc_ref[...]
    kb_ref[0:SEQ, LANE:] = krx_ref[...]
    kb_ref[SEQ:, LANE:] = krc_ref[...]
    vb_ref[0:SEQ, 0:LANE] = vx_ref[...]
    vb_ref[SEQ:, 0:LANE] = vc_ref[...]
    vb_ref[:, LANE:] = jnp.ones((SEQ + CTX_LEN, LANE), BF16)
    def scores(rows):
        return _mla_scores(h, qn_ref[rows, :], qr_ref[rows, :], kb_ref)

    chunks = [slice(r, r + QSUB) for r in range(0, SEQ, QSUB)]
    s = scores(chunks[0])
    for c, rows in enumerate(chunks):
        s_next = scores(chunks[c + 1]) if c + 1 < len(chunks) else None
        o_ref[rows, :] = _mla_out(s, vb_ref)
        s = s_next


def _mla_ctx_kernel(qn_ref, qr_ref, kn_ref, kr_ref, v_ref, o_ref):
    for h in range(HEADS):
        cols = slice(h * LANE, (h + 1) * LANE)
        pair = slice(h // 2 * LANE, (h // 2 + 1) * LANE)
        k = jnp.concatenate([kn_ref[:, cols], kr_ref[...]], axis=-1)
        o_ref[:, cols] = _mla_chain(h, qn_ref[:, cols], qr_ref[:, pair], k,
                                    _with_ones(v_ref[:, cols]))


def _mla_attn(l, qn, qr, kn, v, kr2, with_ctx, w_f32):
    def xs(col):
        return pl.BlockSpec((SEQ, LANE), lambda b, h: (b, col(h)))

    def cs(col):
        return pl.BlockSpec((CTX_LEN, LANE), lambda b, h: (CBLK + b, col(h)))

    def whole(a):
        return pl.BlockSpec((CTX_LEN, a.shape[1]), lambda b: (CBLK + b, 0))

    head, pair, first = (lambda h: h), (lambda h: h // 2), (lambda h: 0)
    return _attn_calls(
        _mla_kernel, _mla_ctx_kernel, "mla_attn",
        [qn, qr, kn, kn, kr2, kr2, v, v],
        [xs(head), xs(pair), xs(head), cs(head), xs(first), cs(first), xs(head), cs(head)],
        [qn, qr, kn, kr2, v],
        [whole(qn), whole(qr), whole(kn), whole(kr2), whole(v)],
        [pltpu.VMEM((SEQ + CTX_LEN, 2 * LANE), BF16), pltpu.VMEM((SEQ + CTX_LEN, 2 * LANE), BF16)],
        with_ctx, l, w_f32, _cast_rows)


def _out_kernel(split, ya_ref, yb_ref, ym_ref, yd_ref, w_ref, x_ref, *refs):
    c_ref, (mod_ref, gpost_ref, xo_ref, wb_ref) = (refs[0], refs[1:]) if split else (None, refs)

    @pl.when(pl.program_id(0) == 0)
    def _():
        wb_ref[...] = w_ref[...].astype(BF16)

    y = jnp.dot(ya_ref[...], wb_ref[0:GROUP_W, :], preferred_element_type=F32)
    y += jnp.dot(yb_ref[...], wb_ref[GROUP_W:2 * GROUP_W, :], preferred_element_type=F32)
    y += jnp.dot(ym_ref[...], wb_ref[2 * GROUP_W:3 * GROUP_W, :], preferred_element_type=F32)
    y += jnp.dot(yd_ref[...], wb_ref[3 * GROUP_W:, :], preferred_element_type=F32)
    delta = mod_ref[2:3, :] * _rms(y, gpost_ref[...])

    def add_residual(src):
        xo_ref[...] = src[...] + delta

    _pick_rows(pl.program_id(0), TM, x_ref, c_ref, add_residual)


def _out_proj(l, ya, yb, ym, yd, w, toks, mods, g_post):
    n = ya.shape[0]
    split = len(toks) == 2
    y_spec = pl.BlockSpec((TM, GROUP_W), lambda i: (i, 0))
    row = pl.BlockSpec((TM, D_MODEL), lambda i: (i, 0))
    return pl.pallas_call(
        functools.partial(_out_kernel, split),
        grid=(n // TM,),
        in_specs=[
            y_spec, y_spec, y_spec, y_spec,
            pl.BlockSpec((None, D_MODEL, D_MODEL), lambda i: (l, 0, 0),
                         pipeline_mode=pl.Buffered(1)),
        ] + (_split_rows_specs(TM, NX) if split else [row]) + [
            _mod_spec(l, TM),
            _layer_spec(l, (1, D_MODEL)),
        ],
        out_specs=row,
        out_shape=jax.ShapeDtypeStruct((n, D_MODEL), F32),
        scratch_shapes=[pltpu.VMEM((D_MODEL, D_MODEL), BF16)],
        compiler_params=_params(("arbitrary",)),
        name="out_proj",
    )(ya, yb, ym, yd, w, *toks, mods, g_post)


def _ffn_kernel(x_ref, xp_ref, xn_ref, mod_ref, gpre_ref, gpost_ref, wgv_ref, cg_ref, cv_ref,
                wd_ref, xo_ref, hx_ref, r_ref):
    i = pl.program_id(0)
    j = pl.program_id(1)
    n = TM_BIG

    @pl.when(j == 0)
    def _():
        slab_shape = (BF16_ROWS, D_MODEL)
        g = jnp.broadcast_to(gpre_ref[...], slab_shape)
        scale = jnp.broadcast_to(1.0 + mod_ref[4:5, :], slab_shape)
        shift = jnp.broadcast_to(mod_ref[3:4, :], slab_shape)

        def hmod(x, r):
            return x * r * g * scale + shift

        def rinv(x):
            return lax.rsqrt(jnp.mean(x * x, axis=-1, keepdims=True) + NORM_EPS)

        _store_rinv(r_ref, x_ref[...])

        def slab(k, carry):
            rows = pl.ds(pl.multiple_of(k * BF16_ROWS, BF16_ROWS), BF16_ROWS)
            dst = pl.ds(pl.multiple_of((k + 1) * BF16_ROWS, BF16_ROWS), BF16_ROWS)
            hx_ref[dst, :] = hmod(x_ref[rows, :], _load_rinv(r_ref, rows)).astype(BF16)
            return carry

        lax.fori_loop(0, n // BF16_ROWS, slab, 0, unroll=4)
        has_prev, has_next = _tile_edges(i, n)
        halo = jnp.concatenate([xp_ref[...], xn_ref[...]], axis=0)
        keep = jnp.where(lax.broadcasted_iota(jnp.int32, (BF16_ROWS, 1), 0) < SUBLANE,
                         has_prev, has_next)
        halo = hmod(halo, rinv(halo)) * keep
        zeros = jnp.zeros((SUBLANE, D_MODEL), F32)
        hx_ref[:BF16_ROWS, :] = jnp.concatenate([zeros, halo[:SUBLANE]], axis=0).astype(BF16)
        hx_ref[BF16_ROWS + n:, :] = jnp.concatenate([halo[SUBLANE:], zeros], axis=0).astype(BF16)
        xo_ref[...] = jnp.zeros_like(xo_ref)

    is_ctx = i * n >= NX
    same_seq = jnp.where(is_ctx, 0.0, 1.0)
    half = n // 2
    us = [jnp.dot(hx_ref[r0:r0 + half + 2 * BF16_ROWS, :], wgv_ref[...],
                  preferred_element_type=F32) for r0 in (0, half)]
    for part, u in enumerate(us):
        r0 = part * half
        prev_row = u[BF16_ROWS - 1:BF16_ROWS]
        next_row = u[BF16_ROWS + half:BF16_ROWS + half + 1]
        if part == 0:
            next_row = next_row * same_seq
        else:
            prev_row = prev_row * same_seq
        main = u[BF16_ROWS:BF16_ROWS + half]

        def conv(cols, c_ref):
            return _conv3_rows(main[:, cols], prev_row[:, cols], next_row[:, cols], c_ref[...],
                               is_ctx)

        gate = conv(slice(0, TF), cg_ref)
        val = conv(slice(TF, 2 * TF), cv_ref)
        act = (gate * jax.nn.sigmoid(gate) * val).astype(BF16)
        xo_ref[r0:r0 + half, :] += jnp.dot(act, wd_ref[...], preferred_element_type=F32)

    @pl.when(j == pl.num_programs(1) - 1)
    def _():
        _store_rinv(r_ref, xo_ref[...])
        slab_shape = (BF16_ROWS, D_MODEL)
        g = jnp.broadcast_to(gpost_ref[...], slab_shape)
        res_gate = jnp.broadcast_to(mod_ref[5:6, :], slab_shape)

        def slab(k, carry):
            rows = pl.ds(pl.multiple_of(k * BF16_ROWS, BF16_ROWS), BF16_ROWS)
            normed = xo_ref[rows, :] * _load_rinv(r_ref, rows) * g
            xo_ref[rows, :] = x_ref[rows, :] + res_gate * normed
            return carry

        lax.fori_loop(0, n // BF16_ROWS, slab, 0, unroll=4)


def _ffn(l, tok, mods, g_pre, g_post, w_up, conv_w, w_down, n_rows):
    nj = D_FF // TF
    return pl.pallas_call(
        _ffn_kernel,
        grid=(n_rows // TM_BIG, nj),
        in_specs=(_halo_specs(TM_BIG, D_MODEL, n_rows, 0) + [
            _mod_spec(l, TM_BIG),
            _layer_spec(l, (1, D_MODEL)),
            _layer_spec(l, (1, D_MODEL)),
            pl.BlockSpec((D_MODEL, 2 * TF), lambda i, j: (0, j)),
            pl.BlockSpec((None, 3, TF), lambda i, j: (l, 0, j)),
            pl.BlockSpec((None, 3, TF), lambda i, j: (l, 0, j + nj)),
            pl.BlockSpec((TF, D_MODEL), lambda i, j: (j, 0)),
        ]),
        out_specs=pl.BlockSpec((TM_BIG, D_MODEL), lambda i, j: (i, 0)),
        out_shape=jax.ShapeDtypeStruct((n_rows, D_MODEL), F32),
        scratch_shapes=[pltpu.VMEM((TM_BIG + 2 * BF16_ROWS, D_MODEL), BF16),
                        pltpu.VMEM((TM_BIG, LANE), F32)],
        compiler_params=_params(("arbitrary", "arbitrary"), VMEM_LIMIT_BIG),
        name="ffn",
    )(tok, tok, tok, mods, g_pre, g_post, w_up, conv_w, conv_w, w_down)


def _rope_tables():
    rows = SEQ // GRID_W
    row = jnp.repeat(jnp.arange(rows), GRID_W).astype(F32)
    col = jnp.tile(jnp.arange(GRID_W), rows).astype(F32)
    n_freq = ROPE_DIM // 4
    inv = ROPE_THETA ** (-jnp.arange(n_freq, dtype=F32) / n_freq)
    ang = jnp.concatenate([row[:, None] * inv, col[:, None] * inv], axis=-1)
    cos, sin = jnp.cos(ang), jnp.sin(ang)
    cos_t = jnp.concatenate([cos, cos, cos, cos], axis=-1)
    sin_t = jnp.concatenate([-sin, sin, -sin, sin], axis=-1)
    cos_t = jnp.concatenate([cos_t, jnp.ones((TM, LANE), F32)], axis=0)
    sin_t = jnp.concatenate([sin_t, jnp.zeros((TM, LANE), F32)], axis=0)
    return cos_t, sin_t


W_IN_KB = 256


def _w_in_kernel(w_ref, o_ref):
    dst_of = {"pool_qkv": COL_A, "conv": COL_DB, "c_kv": COL_CKV, "c_q": COL_CQ, "k_rope": COL_KR}
    src = 0
    for name, n in REF_SEGMENTS:
        dst = dst_of[name]
        o_ref[dst:dst + n, :] = w_ref[src:src + n, :].astype(BF16)
        src += n
    end = COL_KR + ROPE_DIM
    o_ref[end:, :] = jnp.zeros((IN_COLS_PAD - end, W_IN_KB), BF16)


def _layout_w_in(w):
    wt = jnp.swapaxes(w, 1, 2)
    n, k = wt.shape[1:]
    return pl.pallas_call(
        _w_in_kernel,
        grid=(DEPTH, k // W_IN_KB),
        in_specs=[pl.BlockSpec((None, n, W_IN_KB), lambda l, i: (l, 0, i))],
        out_specs=pl.BlockSpec((None, IN_COLS_PAD, W_IN_KB), lambda l, i: (l, 0, i)),
        out_shape=jax.ShapeDtypeStruct((DEPTH, IN_COLS_PAD, k), BF16),
        compiler_params=_params(("arbitrary", "arbitrary")),
        name="w_in_layout",
    )(wt)


def _layout_w_uq(w):
    w = w.reshape(DEPTH, MLA_Q_LORA, HEADS, MLA_NOPE + ROPE_DIM)
    return jnp.concatenate([w[..., :MLA_NOPE].reshape(DEPTH, MLA_Q_LORA, HEADS * MLA_NOPE),
                            w[..., MLA_NOPE:].reshape(DEPTH, MLA_Q_LORA, HEADS * ROPE_DIM)],
                           axis=-1).astype(BF16)


def _layout_w_ukv(w):
    w = w.reshape(DEPTH, MLA_KV_LORA, HEADS, MLA_NOPE + MLA_VD)
    return jnp.concatenate([w[..., :MLA_NOPE].reshape(DEPTH, MLA_KV_LORA, HEADS * MLA_NOPE),
                            w[..., MLA_NOPE:].reshape(DEPTH, MLA_KV_LORA, HEADS * MLA_VD)],
                           axis=-1).astype(BF16)


def kernel(x, c, ctx, c_ctx, ada_w, ada_b, g_pre_mix, g_post_mix, g_pre_ffn, g_post_ffn, w_in,
           pool_w, pool_scale, diff_lambda, diff_subln_g, mla_gq, mla_w_uq, mla_gkv, mla_w_ukv,
           conv_w, w_out, ffn_w_up, ffn_conv_w, ffn_w_down):
    toks = (x.reshape(NX, D_MODEL), ctx.reshape(NC, D_MODEL))
    cc = jnp.concatenate([c, c_ctx[None, :], jnp.zeros((3, D_MODEL), F32)], axis=0)
    mods = _ada(cc, ada_w, ada_b).reshape(DEPTH, 8, 6, D_MODEL)
    cos_t, sin_t = _rope_tables()

    def vec(p):
        return p.reshape(DEPTH, 1, p.shape[-1])

    w_in_b = _layout_w_in(w_in)
    w_uq_b, w_ukv_b = _layout_w_uq(mla_w_uq), _layout_w_ukv(mla_w_ukv)
    pool_w_b = pool_w.astype(BF16)

    for l in range(DEPTH):
        last = l == DEPTH - 1
        n_rows = NX if last else NT
        lam_init = 0.8 - 0.6 * math.exp(-0.3 * l)

        z = _in_proj(l, toks, mods, vec(g_pre_mix), w_in_b)
        ya, yd = _pool_conv(l, z, n_rows, pool_w_b, vec(pool_scale), conv_w)
        yb, w_up_b = _diff_attn(l, z, cos_t, sin_t, diff_lambda, vec(diff_subln_g), lam_init,
                                not last, ffn_w_up)
        qn, qr, kn, v, kr2 = _mla_prep(l, z, vec(mla_gq), vec(mla_gkv), w_uq_b, w_ukv_b, cos_t, sin_t)
        ym, w_down_b = _mla_attn(l, qn, qr, kn, v, kr2, not last, ffn_w_down)
        tok = _out_proj(l, ya, yb, ym, yd, w_out, toks, mods, vec(g_post_mix))
        tok = _ffn(l, tok, mods, vec(g_pre_ffn), vec(g_post_ffn), w_up_b, ffn_conv_w, w_down_b, n_rows)
        toks = (tok,)
    return tok.reshape(BATCH, SEQ, D_MODEL)
```

```python
import functools
import math

import jax
import jax.numpy as jnp
from jax import lax
from jax.experimental import pallas as pl
from jax.experimental.pallas import tpu as pltpu

F32 = jnp.float32
BF16 = jnp.bfloat16

D_MODEL = 2048
BATCH = 4
SEQ = 2048
CTX_LEN = 256
GRID_W = 64
DEPTH = 2
NORM_EPS = 1e-6
ROPE_THETA = 10000.0
GROUP_W = 512
POOL_WINDOWS = (2, 4, 8, 16)
HEADS = 4
D_FF = 5632
ROPE_DIM = 64
DIFF_HD = 64
MLA_NOPE, MLA_VD = 128, 128
MLA_Q_LORA, MLA_KV_LORA = 384, 256
REF_SEGMENTS = (("pool_qkv", 2048), ("c_q", MLA_Q_LORA), ("c_kv", MLA_KV_LORA),
                ("k_rope", ROPE_DIM), ("conv", 1536))

NX = BATCH * SEQ
NC = BATCH * CTX_LEN
NT = NX + NC

COL_A, COL_Q, COL_K, COL_V = 0, 512, 1024, 1536
COL_DB, COL_DC, COL_DH = 2048, 2560, 3072
COL_CKV, COL_CQ, COL_KR = 3584, 3840, 4224
IN_COLS_PAD = 4352

LANE = 128
SUBLANE = 8
BF16_ROWS = 2 * SUBLANE
TM = 512
TM_BIG = 1024
TS = 256
QSUB = 256
TF = 512
TN_ADA = 1024
ADA_ROWS = SUBLANE
VMEM_LIMIT = 56 * 1024 * 1024
VMEM_LIMIT_BIG = 62 * 1024 * 1024


def _params(sem, vmem_limit=VMEM_LIMIT):
    return pltpu.CompilerParams(dimension_semantics=sem, vmem_limit_bytes=vmem_limit)


def _split_rows_specs(rows, n_lat_rows):
    n_lat = n_lat_rows // rows
    return [pl.BlockSpec((rows, D_MODEL), lambda i, *_: (jnp.minimum(i, n_lat - 1), 0)),
            pl.BlockSpec((rows, D_MODEL), lambda i, *_: (jnp.maximum(i - n_lat, 0), 0),
                         pipeline_mode=pl.Buffered(1))]


def _pick_rows(i, rows, x_ref, c_ref, fn):
    if c_ref is None:
        fn(x_ref)
        return

    @pl.when(i * rows < NX)
    def _():
        fn(x_ref)

    @pl.when(i * rows >= NX)
    def _():
        fn(c_ref)


def _rms(x, g):
    ms = jnp.mean(x * x, axis=-1, keepdims=True)
    return x * lax.rsqrt(ms + NORM_EPS) * g


def _store_rinv(r_ref, x):
    r = lax.rsqrt(jnp.mean(x * x, axis=-1, keepdims=True) + NORM_EPS)
    r_ref[...] = jnp.broadcast_to(r, r_ref.shape)


def _load_rinv(r_ref, rows):
    return jnp.tile(r_ref[rows, :], (1, D_MODEL // LANE))


def _nt_dot(a, b):
    return lax.dot_general(a, b, (((1,), (1,)), ((), ())), preferred_element_type=F32)


def _rope(v, cos, sin):
    lane = lax.broadcasted_iota(jnp.int32, v.shape, 1)
    half = ROPE_DIM // 2
    sw = jnp.where((lane & half) == 0, pltpu.roll(v, LANE - half, 1), pltpu.roll(v, half, 1))
    return v * cos + sw * sin


def _seq_pos(tile, rows):
    r = tile * rows + lax.broadcasted_iota(jnp.int32, (rows, 1), 0)
    seqlen = jnp.where(r < NX, SEQ, CTX_LEN)
    return r & (seqlen - 1), seqlen


def _tile_edges(tile, rows):
    r0 = tile * rows
    seqlen = jnp.where(r0 < NX, SEQ, CTX_LEN)
    has_prev = jnp.where((r0 & (seqlen - 1)) == 0, 0.0, 1.0)
    has_next = jnp.where(((r0 + rows) & (seqlen - 1)) == 0, 0.0, 1.0)
    return has_prev, has_next


def _conv3_rows(u, prev_row, next_row, c, is_ctx):
    n = u.shape[0]
    row8 = lax.broadcasted_iota(jnp.int32, (SUBLANE, 1), 0)
    prev = pltpu.roll(u, 1, 0)
    prev = jnp.concatenate([jnp.where(row8 == 0, prev_row, prev[:SUBLANE]), prev[SUBLANE:]], axis=0)
    nxt = pltpu.roll(u, n - 1, 0)
    nxt = jnp.concatenate([nxt[:n - SUBLANE],
                           jnp.where(row8 == SUBLANE - 1, next_row, nxt[n - SUBLANE:])], axis=0)
    ctx_row = row8 + jnp.where(is_ctx, 0, 2 * SUBLANE)
    for p in range(CTX_LEN, n, CTX_LEN):
        prev = jnp.concatenate([prev[:p], jnp.where(ctx_row == 0, 0.0, prev[p:p + SUBLANE]),
                                prev[p + SUBLANE:]], axis=0)
        nxt = jnp.concatenate([nxt[:p - SUBLANE],
                               jnp.where(ctx_row == SUBLANE - 1, 0.0, nxt[p - SUBLANE:p]), nxt[p:]],
                              axis=0)
    return c[0:1, :] * prev + c[1:2, :] * u + c[2:3, :] * nxt


def _halo_specs(rows, width, n_rows, col):
    r8 = rows // SUBLANE
    last8 = n_rows // SUBLANE - 1
    return [
        pl.BlockSpec((rows, width), lambda i, *_: (i, col)),
        pl.BlockSpec((SUBLANE, width), lambda i, *_: (jnp.maximum(i * r8 - 1, 0), col)),
        pl.BlockSpec((SUBLANE, width), lambda i, *_: (jnp.minimum((i + 1) * r8, last8), col)),
    ]


def _layer_spec(l, shape):
    return pl.BlockSpec((None,) + shape, lambda *_: (l,) + (0,) * len(shape))


def _mod_spec(l, rows):
    def index(i, *_):
        r = i * rows
        return (l, jnp.where(r < NX, r // SEQ, BATCH), 0, 0)
    return pl.BlockSpec((None, None, 6, D_MODEL), index)


def _ada_kernel(c_ref, w_ref, b_ref, o_ref):
    c = c_ref[...]
    a = c * jax.nn.sigmoid(c)
    o_ref[...] = jnp.dot(a.astype(BF16), w_ref[...].astype(BF16),
                         preferred_element_type=F32) + b_ref[...]


def _ada(cc, ada_w, ada_b):
    n = ada_w.shape[-1]
    return pl.pallas_call(
        _ada_kernel,
        grid=(DEPTH, n // TN_ADA),
        in_specs=[
            pl.BlockSpec((ADA_ROWS, D_MODEL), lambda l, j: (0, 0)),
            pl.BlockSpec((None, D_MODEL, TN_ADA), lambda l, j: (l, 0, j)),
            pl.BlockSpec((None, 1, TN_ADA), lambda l, j: (l, 0, j)),
        ],
        out_specs=pl.BlockSpec((None, ADA_ROWS, TN_ADA), lambda l, j: (l, 0, j)),
        out_shape=jax.ShapeDtypeStruct((DEPTH, ADA_ROWS, n), F32),
        compiler_params=_params(("arbitrary", "arbitrary")),
        name="ada",
    )(cc, ada_w, ada_b.reshape(DEPTH, 1, n))


def _in_kernel(split, x_ref, *refs):
    c_ref, (mod_ref, g_ref, w_ref, z_ref, h_ref, r_ref) = ((refs[0], refs[1:]) if split
                                                              else (None, refs))

    def put(src):
        _store_rinv(r_ref, src[...])

        shape = (BF16_ROWS, D_MODEL)
        g = jnp.broadcast_to(g_ref[...], shape)
        scale = jnp.broadcast_to(1.0 + mod_ref[1:2, :], shape)
        shift = jnp.broadcast_to(mod_ref[0:1, :], shape)

        def slab(k, carry):
            rows = pl.ds(pl.multiple_of(k * BF16_ROWS, BF16_ROWS), BF16_ROWS)
            y = src[rows, :] * _load_rinv(r_ref, rows) * g
            h_ref[rows, :] = (y * scale + shift).astype(BF16)
            return carry

        lax.fori_loop(0, TM // BF16_ROWS, slab, 0, unroll=4)

    _pick_rows(pl.program_id(0), TM, x_ref, c_ref, put)
    z_ref[...] = _nt_dot(h_ref[...], w_ref[...])


def _in_proj(l, toks, mods, g, w):
    split = len(toks) == 2
    rows = (_split_rows_specs(TM, NX) if split
            else [pl.BlockSpec((TM, D_MODEL), lambda i: (i, 0))])
    return pl.pallas_call(
        functools.partial(_in_kernel, split),
        grid=(NT // TM,),
        in_specs=rows + [
            _mod_spec(l, TM),
            _layer_spec(l, (1, D_MODEL)),
            pl.BlockSpec((None, IN_COLS_PAD, D_MODEL), lambda i: (l, 0, 0),
                         pipeline_mode=pl.Buffered(1)),
        ],
        out_specs=pl.BlockSpec((TM, IN_COLS_PAD), lambda i: (i, 0)),
        out_shape=jax.ShapeDtypeStruct((NT, IN_COLS_PAD), F32),
        scratch_shapes=[pltpu.VMEM((TM, D_MODEL), BF16), pltpu.VMEM((TM, LANE), F32)],
        compiler_params=_params(("arbitrary",), VMEM_LIMIT_BIG),
        name="in_proj",
    )(*toks, mods, g, w)


def _pool_conv_kernel(a_ref, ap_ref, an_ref, db_ref, dc_ref, dcp_ref, dcn_ref,
                      dh_ref, dhp_ref, dhn_ref, pw_ref, ps_ref, cw_ref,
                      ya_ref, yd_ref, e_ref):
    i = pl.program_id(0)
    pos, seqlen = _seq_pos(i, TS)
    pf, nf = _tile_edges(i, TS)

    halo = SUBLANE
    e_ref[0:halo, :] = ap_ref[...] * pf
    e_ref[halo:halo + TS, :] = a_ref[...]
    e_ref[halo + TS:, :] = an_ref[...] * nf
    for g, w in enumerate(POOL_WINDOWS):
        half = w // 2
        cs = slice(g * LANE, (g + 1) * LANE)
        c, s = e_ref[:, cs], 1
        while s < w:
            c = c + pltpu.roll(c, TS + 2 * halo - s, 0)
            s *= 2
        acc = c[halo - half:halo - half + TS]
        cnt = (jnp.minimum(pos + half, seqlen) - jnp.maximum(pos - half, 0)).astype(F32)
        pooled = acc * (1.0 / cnt) - a_ref[:, cs]
        y = jnp.dot(pooled.astype(BF16), pw_ref[g], preferred_element_type=F32)
        ya_ref[:, cs] = (y * ps_ref[:, cs]).astype(BF16)

    p_prev = dcp_ref[SUBLANE - 1:SUBLANE, :] * dhp_ref[SUBLANE - 1:SUBLANE, :] * pf
    p_next = dcn_ref[0:1, :] * dhn_ref[0:1, :] * nf
    conv = _conv3_rows(dc_ref[...] * dh_ref[...], p_prev, p_next, cw_ref[...], False)
    yd_ref[...] = (db_ref[...] * conv).astype(BF16)


def _pool_conv(l, z, n_rows, pool_w, pool_scale, conv_w):
    ca, cb, cc, ch = (COL_A // GROUP_W, COL_DB // GROUP_W, COL_DC // GROUP_W, COL_DH // GROUP_W)
    out = jax.ShapeDtypeStruct((n_rows, GROUP_W), BF16)
    return pl.pallas_call(
        _pool_conv_kernel,
        grid=(n_rows // TS,),
        in_specs=(_halo_specs(TS, GROUP_W, NT, ca)
                  + [pl.BlockSpec((TS, GROUP_W), lambda i: (i, cb))]
                  + _halo_specs(TS, GROUP_W, NT, cc) + _halo_specs(TS, GROUP_W, NT, ch)
                  + [_layer_spec(l, (len(POOL_WINDOWS), LANE, LANE)),
                     _layer_spec(l, (1, GROUP_W)), _layer_spec(l, (3, GROUP_W))]),
        out_specs=[pl.BlockSpec((TS, GROUP_W), lambda i: (i, 0))] * 2,
        out_shape=[out, out],
        scratch_shapes=[pltpu.VMEM((TS + 2 * SUBLANE, GROUP_W), F32)],
        compiler_params=_params(("arbitrary",)),
        name="pool_conv",
    )(z, z, z, z, z, z, z, z, z, z, pool_w, pool_scale, conv_w)


CBLK = NX // CTX_LEN


def _softmax_pv(s, v):
    e = jnp.exp(s - jnp.max(s, axis=-1, keepdims=True)).astype(BF16)
    return jnp.dot(e, v, preferred_element_type=F32)


def _with_ones(v):
    return jnp.concatenate([v, jnp.ones(v.shape, v.dtype)], axis=-1)


def _cast_rows(w_ref, wb_ref):
    wb_ref[...] = w_ref[...].astype(BF16)


def _cast_rows_pair_chunks(w_ref, wb_ref):
    for k in range(D_FF // TF):
        for part in range(2):
            src = part * D_FF + k * TF
            dst = (2 * k + part) * TF
            wb_ref[:, dst:dst + TF] = w_ref[:, src:src + TF].astype(BF16)


def _attn_calls(lat_fn, ctx_fn, name, lat_ins, lat_specs, ctx_ins, ctx_specs, scratch, with_ctx,
                l, w_f32, cast_fn):
    steps = BATCH * HEADS
    w_rows, w_cols = w_f32.shape[1] // steps, w_f32.shape[2]

    def lat_body(*refs):
        n = len(lat_ins)
        w_ref, (o_ref, wb_ref), rest = refs[n], refs[-len(scratch) - 2:-len(scratch)], refs[-len(scratch):]
        cast_fn(w_ref, wb_ref)
        lat_fn(*refs[:n], o_ref, *rest)

    y0 = [jnp.zeros((NT, GROUP_W), BF16)] if with_ctx else []
    y_shape = jax.ShapeDtypeStruct((NT if with_ctx else NX, GROUP_W), BF16)
    y, w_b = pl.pallas_call(
        lat_body,
        grid=(BATCH, HEADS),
        in_specs=(lat_specs
                  + [pl.BlockSpec((None, w_rows, w_cols), lambda b, h: (l, b * HEADS + h, 0))]
                  + [pl.BlockSpec(memory_space=pl.ANY)] * len(y0)),
        out_specs=[pl.BlockSpec((SEQ, LANE), lambda b, h: (b, h)),
                   pl.BlockSpec((w_rows, w_cols), lambda b, h: (b * HEADS + h, 0))],
        out_shape=[y_shape, jax.ShapeDtypeStruct(w_f32.shape[1:], BF16)],
        input_output_aliases={len(lat_ins) + 1: 0} if with_ctx else {},
        scratch_shapes=scratch,
        compiler_params=_params(("arbitrary", "arbitrary")),
        name=name,
    )(*lat_ins, w_f32, *y0)
    if not with_ctx:
        return y, w_b
    y = pl.pallas_call(
        lambda *refs: ctx_fn(*refs[:len(ctx_ins)], refs[-1]),
        grid=(BATCH,),
        in_specs=ctx_specs + [pl.BlockSpec(memory_space=pl.ANY)],
        out_specs=pl.BlockSpec((CTX_LEN, GROUP_W), lambda b: (CBLK + b, 0)),
        out_shape=y_shape,
        input_output_aliases={len(ctx_ins): 0},
        compiler_params=_params(("arbitrary",)),
        name=name + "_ctx",
    )(*ctx_ins, y)
    return y, w_b


def _diff_lambda(lam_ref, lam_init):
    lp = lam_ref[...]
    return (jnp.exp(jnp.sum(lp[0:1] * lp[1:2], keepdims=True))
            - jnp.exp(jnp.sum(lp[2:3] * lp[3:4], keepdims=True)) + lam_init)


def _diff_scores(q, k):
    q = q * (DIFF_HD ** -0.5)
    lane = lax.broadcasted_iota(jnp.int32, q.shape, 1)
    return (_nt_dot(jnp.where(lane < DIFF_HD, q, 0.0).astype(BF16), k[...]),
            _nt_dot(jnp.where(lane >= DIFF_HD, q, 0.0).astype(BF16), k[...]))


def _diff_out(s, v, lam, g, lam_init):
    o1 = _softmax_pv(s[0], v[...])
    o2 = _softmax_pv(s[1], v[...])
    o = o1[:, :LANE] / o1[:, LANE:] - o2[:, :LANE] * (lam / o2[:, LANE:])
    return (_rms(o, g) * (1.0 - lam_init)).astype(BF16)


def _diff_chain(q, k, v, lam, g, lam_init):
    return _diff_out(_diff_scores(q, k), v, lam, g, lam_init)


def _diff_kernel(lam_init, q_ref, kx_ref, kc_ref, vx_ref, vc_ref, cos_ref, sin_ref, lam_ref, g_ref,
                 o_ref, kb_ref, vb_ref):
    kb_ref[0:SEQ, :] = _rope(kx_ref[...], cos_ref[...], sin_ref[...]).astype(BF16)
    kb_ref[SEQ:, :] = kc_ref[...].astype(BF16)
    vb_ref[0:SEQ, 0:LANE] = vx_ref[...].astype(BF16)
    vb_ref[SEQ:, 0:LANE] = vc_ref[...].astype(BF16)
    vb_ref[:, LANE:] = jnp.ones((SEQ + CTX_LEN, LANE), BF16)
    lam = _diff_lambda(lam_ref, lam_init)
    def scores(rows):
        return _diff_scores(_rope(q_ref[rows, :], cos_ref[rows, :], sin_ref[rows, :]), kb_ref)

    chunks = [slice(r, r + QSUB) for r in range(0, SEQ, QSUB)]
    s = scores(chunks[0])
    for c, rows in enumerate(chunks):
        s_next = scores(chunks[c + 1]) if c + 1 < len(chunks) else None
        o_ref[rows, :] = _diff_out(s, vb_ref, lam, g_ref[...], lam_init)
        s = s_next


def _diff_ctx_kernel(lam_init, q_ref, k_ref, v_ref, lam_ref, g_ref, o_ref):
    lam = _diff_lambda(lam_ref, lam_init)
    for h in range(HEADS):
        cols = slice(h * LANE, (h + 1) * LANE)
        o_ref[:, cols] = _diff_chain(q_ref[:, cols], k_ref[:, cols].astype(BF16),
                                     _with_ones(v_ref[:, cols].astype(BF16)), lam, g_ref[...],
                                     lam_init)


def _diff_attn(l, z, cos_t, sin_t, lam_p, subln_g, lam_init, with_ctx, w_f32):
    cq, ck, cv = COL_Q // LANE, COL_K // LANE, COL_V // LANE

    def xs(col):
        return pl.BlockSpec((SEQ, LANE), lambda b, h: (b, col + h))

    def cs(col):
        return pl.BlockSpec((CTX_LEN, LANE), lambda b, h: (CBLK + b, col + h))

    def all_heads(col):
        return pl.BlockSpec((CTX_LEN, GROUP_W), lambda b: (CBLK + b, col // GROUP_W))

    table = pl.BlockSpec((SEQ, LANE), lambda b, h: (0, 0))
    params = [_layer_spec(l, (4, DIFF_HD)), _layer_spec(l, (1, LANE))]
    return _attn_calls(
        functools.partial(_diff_kernel, lam_init), functools.partial(_diff_ctx_kernel, lam_init),
        "diff_attn",
        [z, z, z, z, z, cos_t, sin_t, lam_p, subln_g],
        [xs(cq), xs(ck), cs(ck), xs(cv), cs(cv), table, table] + params,
        [z, z, z, lam_p, subln_g],
        [all_heads(COL_Q), all_heads(COL_K), all_heads(COL_V)] + params,
        [pltpu.VMEM((SEQ + CTX_LEN, LANE), BF16), pltpu.VMEM((SEQ + CTX_LEN, 2 * LANE), BF16)],
        with_ctx, l, w_f32, _cast_rows_pair_chunks)


def _mla_prep_kernel(ckv_ref, cq_ref, kr_ref, gq_ref, gkv_ref, wq_ref, wkv_ref, cos_ref, sin_ref,
                     qn_ref, qr_ref, kn_ref, v_ref, kr2_ref):
    cos = cos_ref[...]
    sin = sin_ref[...]
    q = jnp.dot(_rms(cq_ref[...], gq_ref[...]).astype(BF16), wq_ref[...],
                preferred_element_type=F32)
    qn_ref[...] = q[:, :GROUP_W].astype(BF16)
    for c in range(0, 2 * LANE, LANE):
        qr_ref[:, c:c + LANE] = _rope(q[:, GROUP_W + c:GROUP_W + c + LANE], cos, sin).astype(BF16)
    kv = jnp.dot(_rms(ckv_ref[...], gkv_ref[...]).astype(BF16), wkv_ref[...],
                 preferred_element_type=F32)
    kn_ref[...] = kv[:, :GROUP_W].astype(BF16)
    v_ref[...] = kv[:, GROUP_W:].astype(BF16)
    kr = _rope(kr_ref[...], cos, sin)
    kr2_ref[...] = (kr + pltpu.roll(kr, ROPE_DIM, 1)).astype(BF16)


def _mla_prep(l, z, gq, gkv, wq, wkv, cos_t, sin_t):
    def out(w):
        return jax.ShapeDtypeStruct((NT, w), BF16)

    def row(w):
        return pl.BlockSpec((TM, w), lambda i: (i, 0))

    rope = pl.BlockSpec((TM, LANE), lambda i: (jnp.where(i * TM < NX, i % (SEQ // TM), SEQ // TM), 0))
    return pl.pallas_call(
        _mla_prep_kernel,
        grid=(NT // TM,),
        in_specs=[
            pl.BlockSpec((TM, MLA_KV_LORA), lambda i: (i, COL_CKV // MLA_KV_LORA)),
            pl.BlockSpec((TM, MLA_Q_LORA), lambda i: (i, COL_CQ // MLA_Q_LORA)),
            pl.BlockSpec((TM, LANE), lambda i: (i, COL_KR // LANE)),
            _layer_spec(l, (1, MLA_Q_LORA)),
            _layer_spec(l, (1, MLA_KV_LORA)),
            _layer_spec(l, (MLA_Q_LORA, HEADS * (MLA_NOPE + ROPE_DIM))),
            _layer_spec(l, (MLA_KV_LORA, HEADS * (MLA_NOPE + MLA_VD))),
            rope, rope,
        ],
        out_specs=[row(GROUP_W), row(2 * LANE), row(GROUP_W), row(GROUP_W), row(LANE)],
        out_shape=[out(GROUP_W), out(2 * LANE), out(GROUP_W), out(GROUP_W), out(LANE)],
        compiler_params=_params(("arbitrary",)),
        name="mla_prep",
    )(z, z, z, gq, gkv, wq, wkv, cos_t, sin_t)


def _mla_scores(h, qn, qr, k):
    lane = lax.broadcasted_iota(jnp.int32, qr.shape, 1)
    lo = (h % 2) * ROPE_DIM
    qr = jnp.where((lane >= lo) & (lane < lo + ROPE_DIM), qr, jnp.zeros_like(qr))
    return _nt_dot(jnp.concatenate([qn, qr], axis=-1), k[...]) * ((MLA_NOPE + ROPE_DIM) ** -0.5)


def _mla_out(s, v):
    o = _softmax_pv(s, v[...])
    return (o[:, :LANE] / o[:, LANE:]).astype(BF16)


def _mla_chain(h, qn, qr, k, v):
    return _mla_out(_mla_scores(h, qn, qr, k), v)


def _mla_kernel(qn_ref, qr_ref, knx_ref, knc_ref, krx_ref, krc_ref, vx_ref, vc_ref,
                o_ref, kb_ref, vb_ref):
    h = pl.program_id(1)
    kb_ref[0:SEQ, 0:LANE] = knx_ref[...]
    kb_ref[SEQ:, 0:LANE] = knc_ref[...]
    kb_ref[0:SEQ, LANE:] = krx_ref[...]
    kb_ref[SEQ:, LANE:] = krc_ref[...]
    vb_ref[0:SEQ, 0:LANE] = vx_ref[...]
    vb_ref[SEQ:, 0:LANE] = vc_ref[...]
    vb_ref[:, LANE:] = jnp.ones((SEQ + CTX_LEN, LANE), BF16)
    def scores(rows):
        return _mla_scores(h, qn_ref[rows, :], qr_ref[rows, :], kb_ref)

    chunks = [slice(r, r + QSUB) for r in range(0, SEQ, QSUB)]
    s = scores(chunks[0])
    for c, rows in enumerate(chunks):
        s_next = scores(chunks[c + 1]) if c + 1 < len(chunks) else None
        o_ref[rows, :] = _mla_out(s, vb_ref)
        s = s_next


def _mla_ctx_kernel(qn_ref, qr_ref, kn_ref, kr_ref, v_ref, o_ref):
    for h in range(HEADS):
        cols = slice(h * LANE, (h + 1) * LANE)
        pair = slice(h // 2 * LANE, (h // 2 + 1) * LANE)
        k = jnp.concatenate([kn_ref[:, cols], kr_ref[...]], axis=-1)
        o_ref[:, cols] = _mla_chain(h, qn_ref[:, cols], qr_ref[:, pair], k,
                                    _with_ones(v_ref[:, cols]))


def _mla_attn(l, qn, qr, kn, v, kr2, with_ctx, w_f32):
    def xs(col):
        return pl.BlockSpec((SEQ, LANE), lambda b, h: (b, col(h)))

    def cs(col):
        return pl.BlockSpec((CTX_LEN, LANE), lambda b, h: (CBLK + b, col(h)))

    def whole(a):
        return pl.BlockSpec((CTX_LEN, a.shape[1]), lambda b: (CBLK + b, 0))

    head, pair, first = (lambda h: h), (lambda h: h // 2), (lambda h: 0)
    return _attn_calls(
        _mla_kernel, _mla_ctx_kernel, "mla_attn",
        [qn, qr, kn, kn, kr2, kr2, v, v],
        [xs(head), xs(pair), xs(head), cs(head), xs(first), cs(first), xs(head), cs(head)],
        [qn, qr, kn, kr2, v],
        [whole(qn), whole(qr), whole(kn), whole(kr2), whole(v)],
        [pltpu.VMEM((SEQ + CTX_LEN, 2 * LANE), BF16), pltpu.VMEM((SEQ + CTX_LEN, 2 * LANE), BF16)],
        with_ctx, l, w_f32, _cast_rows)


def _out_kernel(split, ya_ref, yb_ref, ym_ref, yd_ref, w_ref, x_ref, *refs):
    c_ref, (mod_ref, gpost_ref, xo_ref, wb_ref) = (refs[0], refs[1:]) if split else (None, refs)

    @pl.when(pl.program_id(0) == 0)
    def _():
        wb_ref[...] = w_ref[...].astype(BF16)

    y = jnp.dot(ya_ref[...], wb_ref[0:GROUP_W, :], preferred_element_type=F32)
    y += jnp.dot(yb_ref[...], wb_ref[GROUP_W:2 * GROUP_W, :], preferred_element_type=F32)
    y += jnp.dot(ym_ref[...], wb_ref[2 * GROUP_W:3 * GROUP_W, :], preferred_element_type=F32)
    y += jnp.dot(yd_ref[...], wb_ref[3 * GROUP_W:, :], preferred_element_type=F32)
    delta = mod_ref[2:3, :] * _rms(y, gpost_ref[...])

    def add_residual(src):
        xo_ref[...] = src[...] + delta

    _pick_rows(pl.program_id(0), TM, x_ref, c_ref, add_residual)


def _out_proj(l, ya, yb, ym, yd, w, toks, mods, g_post):
    n = ya.shape[0]
    split = len(toks) == 2
    y_spec = pl.BlockSpec((TM, GROUP_W), lambda i: (i, 0))
    row = pl.BlockSpec((TM, D_MODEL), lambda i: (i, 0))
    return pl.pallas_call(
        functools.partial(_out_kernel, split),
        grid=(n // TM,),
        in_specs=[
            y_spec, y_spec, y_spec, y_spec,
            pl.BlockSpec((None, D_MODEL, D_MODEL), lambda i: (l, 0, 0),
                         pipeline_mode=pl.Buffered(1)),
        ] + (_split_rows_specs(TM, NX) if split else [row]) + [
            _mod_spec(l, TM),
            _layer_spec(l, (1, D_MODEL)),
        ],
        out_specs=row,
        out_shape=jax.ShapeDtypeStruct((n, D_MODEL), F32),
        scratch_shapes=[pltpu.VMEM((D_MODEL, D_MODEL), BF16)],
        compiler_params=_params(("arbitrary",)),
        name="out_proj",
    )(ya, yb, ym, yd, w, *toks, mods, g_post)


def _ffn_kernel(x_ref, xp_ref, xn_ref, mod_ref, gpre_ref, gpost_ref, wgv_ref, cg_ref, cv_ref,
                wd_ref, xo_ref, hx_ref, r_ref):
    i = pl.program_id(0)
    j = pl.program_id(1)
    n = TM_BIG

    @pl.when(j == 0)
    def _():
        slab_shape = (BF16_ROWS, D_MODEL)
        g = jnp.broadcast_to(gpre_ref[...], slab_shape)
        scale = jnp.broadcast_to(1.0 + mod_ref[4:5, :], slab_shape)
        shift = jnp.broadcast_to(mod_ref[3:4, :], slab_shape)

        def hmod(x, r):
            return x * r * g * scale + shift

        def rinv(x):
            return lax.rsqrt(jnp.mean(x * x, axis=-1, keepdims=True) + NORM_EPS)

        _store_rinv(r_ref, x_ref[...])

        def slab(k, carry):
            rows = pl.ds(pl.multiple_of(k * BF16_ROWS, BF16_ROWS), BF16_ROWS)
            dst = pl.ds(pl.multiple_of((k + 1) * BF16_ROWS, BF16_ROWS), BF16_ROWS)
            hx_ref[dst, :] = hmod(x_ref[rows, :], _load_rinv(r_ref, rows)).astype(BF16)
            return carry

        lax.fori_loop(0, n // BF16_ROWS, slab, 0, unroll=4)
        has_prev, has_next = _tile_edges(i, n)
        halo = jnp.concatenate([xp_ref[...], xn_ref[...]], axis=0)
        keep = jnp.where(lax.broadcasted_iota(jnp.int32, (BF16_ROWS, 1), 0) < SUBLANE,
                         has_prev, has_next)
        halo = hmod(halo, rinv(halo)) * keep
        zeros = jnp.zeros((SUBLANE, D_MODEL), F32)
        hx_ref[:BF16_ROWS, :] = jnp.concatenate([zeros, halo[:SUBLANE]], axis=0).astype(BF16)
        hx_ref[BF16_ROWS + n:, :] = jnp.concatenate([halo[SUBLANE:], zeros], axis=0).astype(BF16)
        xo_ref[...] = jnp.zeros_like(xo_ref)

    is_ctx = i * n >= NX
    same_seq = jnp.where(is_ctx, 0.0, 1.0)
    half = n // 2
    us = [jnp.dot(hx_ref[r0:r0 + half + 2 * BF16_ROWS, :], wgv_ref[...],
                  preferred_element_type=F32) for r0 in (0, half)]
    for part, u in enumerate(us):
        r0 = part * half
        prev_row = u[BF16_ROWS - 1:BF16_ROWS]
        next_row = u[BF16_ROWS + half:BF16_ROWS + half + 1]
        if part == 0:
            next_row = next_row * same_seq
        else:
            prev_row = prev_row * same_seq
        main = u[BF16_ROWS:BF16_ROWS + half]

        def conv(cols, c_ref):
            return _conv3_rows(main[:, cols], prev_row[:, cols], next_row[:, cols], c_ref[...],
                               is_ctx)

        gate = conv(slice(0, TF), cg_ref)
        val = conv(slice(TF, 2 * TF), cv_ref)
        act = (gate * jax.nn.sigmoid(gate) * val).astype(BF16)
        xo_ref[r0:r0 + half, :] += jnp.dot(act, wd_ref[...], preferred_element_type=F32)

    @pl.when(j == pl.num_programs(1) - 1)
    def _():
        _store_rinv(r_ref, xo_ref[...])
        slab_shape = (BF16_ROWS, D_MODEL)
        g = jnp.broadcast_to(gpost_ref[...], slab_shape)
        res_gate = jnp.broadcast_to(mod_ref[5:6, :], slab_shape)

        def slab(k, carry):
            rows = pl.ds(pl.multiple_of(k * BF16_ROWS, BF16_ROWS), BF16_ROWS)
            normed = xo_ref[rows, :] * _load_rinv(r_ref, rows) * g
            xo_ref[rows, :] = x_ref[rows, :] + res_gate * normed
            return carry

        lax.fori_loop(0, n // BF16_ROWS, slab, 0, unroll=4)


def _ffn(l, tok, mods, g_pre, g_post, w_up, conv_w, w_down, n_rows):
    nj = D_FF // TF
    return pl.pallas_call(
        _ffn_kernel,
        grid=(n_rows // TM_BIG, nj),
        in_specs=(_halo_specs(TM_BIG, D_MODEL, n_rows, 0) + [
            _mod_spec(l, TM_BIG),
            _layer_spec(l, (1, D_MODEL)),
            _layer_spec(l, (1, D_MODEL)),
            pl.BlockSpec((D_MODEL, 2 * TF), lambda i, j: (0, j)),
            pl.BlockSpec((None, 3, TF), lambda i, j: (l, 0, j)),
            pl.BlockSpec((None, 3, TF), lambda i, j: (l, 0, j + nj)),
            pl.BlockSpec((TF, D_MODEL), lambda i, j: (j, 0)),
        ]),
        out_specs=pl.BlockSpec((TM_BIG, D_MODEL), lambda i, j: (i, 0)),
        out_shape=jax.ShapeDtypeStruct((n_rows, D_MODEL), F32),
        scratch_shapes=[pltpu.VMEM((TM_BIG + 2 * BF16_ROWS, D_MODEL), BF16),
                        pltpu.VMEM((TM_BIG, LANE), F32)],
        compiler_params=_params(("arbitrary", "arbitrary"), VMEM_LIMIT_BIG),
        name="ffn",
    )(tok, tok, tok, mods, g_pre, g_post, w_up, conv_w, conv_w, w_down)


def _rope_tables():
    rows = SEQ // GRID_W
    row = jnp.repeat(jnp.arange(rows), GRID_W).astype(F32)
    col = jnp.tile(jnp.arange(GRID_W), rows).astype(F32)
    n_freq = ROPE_DIM // 4
    inv = ROPE_THETA ** (-jnp.arange(n_freq, dtype=F32) / n_freq)
    ang = jnp.concatenate([row[:, None] * inv, col[:, None] * inv], axis=-1)
    cos, sin = jnp.cos(ang), jnp.sin(ang)
    cos_t = jnp.concatenate([cos, cos, cos, cos], axis=-1)
    sin_t = jnp.concatenate([-sin, sin, -sin, sin], axis=-1)
    cos_t = jnp.concatenate([cos_t, jnp.ones((TM, LANE), F32)], axis=0)
    sin_t = jnp.concatenate([sin_t, jnp.zeros((TM, LANE), F32)], axis=0)
    return cos_t, sin_t


W_IN_KB = 256


def _w_in_kernel(w_ref, o_ref):
    dst_of = {"pool_qkv": COL_A, "conv": COL_DB, "c_kv": COL_CKV, "c_q": COL_CQ, "k_rope": COL_KR}
    src = 0
    for name, n in REF_SEGMENTS:
        dst = dst_of[name]
        o_ref[dst:dst + n, :] = w_ref[src:src + n, :].astype(BF16)
        src += n
    end = COL_KR + ROPE_DIM
    o_ref[end:, :] = jnp.zeros((IN_COLS_PAD - end, W_IN_KB), BF16)


def _layout_w_in(w):
    wt = jnp.swapaxes(w, 1, 2)
    n, k = wt.shape[1:]
    return pl.pallas_call(
        _w_in_kernel,
        grid=(DEPTH, k // W_IN_KB),
        in_specs=[pl.BlockSpec((None, n, W_IN_KB), lambda l, i: (l, 0, i))],
        out_specs=pl.BlockSpec((None, IN_COLS_PAD, W_IN_KB), lambda l, i: (l, 0, i)),
        out_shape=jax.ShapeDtypeStruct((DEPTH, IN_COLS_PAD, k), BF16),
        compiler_params=_params(("arbitrary", "arbitrary")),
        name="w_in_layout",
    )(wt)


def _layout_w_uq(w):
    w = w.reshape(DEPTH, MLA_Q_LORA, HEADS, MLA_NOPE + ROPE_DIM)
    return jnp.concatenate([w[..., :MLA_NOPE].reshape(DEPTH, MLA_Q_LORA, HEADS * MLA_NOPE),
                            w[..., MLA_NOPE:].reshape(DEPTH, MLA_Q_LORA, HEADS * ROPE_DIM)],
                           axis=-1).astype(BF16)


def _layout_w_ukv(w):
    w = w.reshape(DEPTH, MLA_KV_LORA, HEADS, MLA_NOPE + MLA_VD)
    return jnp.concatenate([w[..., :MLA_NOPE].reshape(DEPTH, MLA_KV_LORA, HEADS * MLA_NOPE),
                            w[..., MLA_NOPE:].reshape(DEPTH, MLA_KV_LORA, HEADS * MLA_VD)],
                           axis=-1).astype(BF16)


def kernel(x, c, ctx, c_ctx, ada_w, ada_b, g_pre_mix, g_post_mix, g_pre_ffn, g_post_ffn, w_in,
           pool_w, pool_scale, diff_lambda, diff_subln_g, mla_gq, mla_w_uq, mla_gkv, mla_w_ukv,
           conv_w, w_out, ffn_w_up, ffn_conv_w, ffn_w_down):
    toks = (x.reshape(NX, D_MODEL), ctx.reshape(NC, D_MODEL))
    cc = jnp.concatenate([c, c_ctx[None, :], jnp.zeros((ADA_ROWS - BATCH - 1, D_MODEL), F32)],
                         axis=0)
    mods = _ada(cc, ada_w, ada_b).reshape(DEPTH, ADA_ROWS, 6, D_MODEL)
    cos_t, sin_t = _rope_tables()

    def vec(p):
        return p.reshape(DEPTH, 1, p.shape[-1])

    w_in_b = _layout_w_in(w_in)
    w_uq_b, w_ukv_b = _layout_w_uq(mla_w_uq), _layout_w_ukv(mla_w_ukv)
    pool_w_b = pool_w.astype(BF16)

    for l in range(DEPTH):
        last = l == DEPTH - 1
        n_rows = NX if last else NT
        lam_init = 0.8 - 0.6 * math.exp(-0.3 * l)

        z = _in_proj(l, toks, mods, vec(g_pre_mix), w_in_b)
        ya, yd = _pool_conv(l, z, n_rows, pool_w_b, vec(pool_scale), conv_w)
        yb, w_up_b = _diff_attn(l, z, cos_t, sin_t, diff_lambda, vec(diff_subln_g), lam_init,
                                not last, ffn_w_up)
        qn, qr, kn, v, kr2 = _mla_prep(l, z, vec(mla_gq), vec(mla_gkv), w_uq_b, w_ukv_b, cos_t, sin_t)
        ym, w_down_b = _mla_attn(l, qn, qr, kn, v, kr2, not last, ffn_w_down)
        tok = _out_proj(l, ya, yb, ym, yd, w_out, toks, mods, vec(g_post_mix))
        tok = _ffn(l, tok, mods, vec(g_pre_ffn), vec(g_post_ffn), w_up_b, ffn_conv_w, w_down_b, n_rows)
        toks = (tok,)
    return tok.reshape(BATCH, SEQ, D_MODEL)
```

```python
import functools
import math

import jax
import jax.numpy as jnp
from jax import lax
from jax.experimental import pallas as pl
from jax.experimental.pallas import tpu as pltpu

F32 = jnp.float32
BF16 = jnp.bfloat16

D_MODEL = 2048
BATCH = 4
SEQ = 2048
CTX_LEN = 256
GRID_W = 64
DEPTH = 2
NORM_EPS = 1e-6
ROPE_THETA = 10000.0
GROUP_W = 512
POOL_WINDOWS = (2, 4, 8, 16)
HEADS = 4
D_FF = 5632
ROPE_DIM = 64
DIFF_HD = 64
MLA_NOPE, MLA_VD = 128, 128
MLA_Q_LORA, MLA_KV_LORA = 384, 256
REF_SEGMENTS = (("pool_qkv", 2048), ("c_q", MLA_Q_LORA), ("c_kv", MLA_KV_LORA),
                ("k_rope", ROPE_DIM), ("conv", 1536))

NX = BATCH * SEQ
NC = BATCH * CTX_LEN
NT = NX + NC

COL_A, COL_Q, COL_K, COL_V = 0, 512, 1024, 1536
COL_DB, COL_DC, COL_DH = 2048, 2560, 3072
COL_CKV, COL_CQ, COL_KR = 3584, 3840, 4224
IN_COLS_PAD = 4352

LANE = 128
SUBLANE = 8
BF16_ROWS = 2 * SUBLANE
TM = 512
TM_BIG = 1024
TS = 256
QSUB = 256
TF = 512
TN_ADA = 2048
VMEM_LIMIT = 56 * 1024 * 1024
VMEM_LIMIT_BIG = 62 * 1024 * 1024


def _params(sem, vmem_limit=VMEM_LIMIT):
    return pltpu.CompilerParams(dimension_semantics=sem, vmem_limit_bytes=vmem_limit)


def _split_rows_specs(rows, n_lat_rows):
    n_lat = n_lat_rows // rows
    return [pl.BlockSpec((rows, D_MODEL), lambda i, *_: (jnp.minimum(i, n_lat - 1), 0)),
            pl.BlockSpec((rows, D_MODEL), lambda i, *_: (jnp.maximum(i - n_lat, 0), 0),
                         pipeline_mode=pl.Buffered(1))]


def _pick_rows(i, rows, x_ref, c_ref, fn):
    if c_ref is None:
        fn(x_ref)
        return

    @pl.when(i * rows < NX)
    def _():
        fn(x_ref)

    @pl.when(i * rows >= NX)
    def _():
        fn(c_ref)


def _rms(x, g):
    ms = jnp.mean(x * x, axis=-1, keepdims=True)
    return x * lax.rsqrt(ms + NORM_EPS) * g


def _store_rinv(r_ref, x):
    r = lax.rsqrt(jnp.mean(x * x, axis=-1, keepdims=True) + NORM_EPS)
    r_ref[...] = jnp.broadcast_to(r, r_ref.shape)


def _load_rinv(r_ref, rows):
    return jnp.tile(r_ref[rows, :], (1, D_MODEL // LANE))


def _nt_dot(a, b):
    return lax.dot_general(a, b, (((1,), (1,)), ((), ())), preferred_element_type=F32)


def _rope(v, cos, sin):
    lane = lax.broadcasted_iota(jnp.int32, v.shape, 1)
    half = ROPE_DIM // 2
    sw = jnp.where((lane & half) == 0, pltpu.roll(v, LANE - half, 1), pltpu.roll(v, half, 1))
    return v * cos + sw * sin


def _seq_pos(tile, rows):
    r = tile * rows + lax.broadcasted_iota(jnp.int32, (rows, 1), 0)
    seqlen = jnp.where(r < NX, SEQ, CTX_LEN)
    return r & (seqlen - 1), seqlen


def _tile_edges(tile, rows):
    r0 = tile * rows
    seqlen = jnp.where(r0 < NX, SEQ, CTX_LEN)
    has_prev = jnp.where((r0 & (seqlen - 1)) == 0, 0.0, 1.0)
    has_next = jnp.where(((r0 + rows) & (seqlen - 1)) == 0, 0.0, 1.0)
    return has_prev, has_next


def _conv3_rows(u, prev_row, next_row, c, is_ctx):
    n = u.shape[0]
    row8 = lax.broadcasted_iota(jnp.int32, (SUBLANE, 1), 0)
    prev = pltpu.roll(u, 1, 0)
    prev = jnp.concatenate([jnp.where(row8 == 0, prev_row, prev[:SUBLANE]), prev[SUBLANE:]], axis=0)
    nxt = pltpu.roll(u, n - 1, 0)
    nxt = jnp.concatenate([nxt[:n - SUBLANE],
                           jnp.where(row8 == SUBLANE - 1, next_row, nxt[n - SUBLANE:])], axis=0)
    ctx_row = row8 + jnp.where(is_ctx, 0, 2 * SUBLANE)
    for p in range(CTX_LEN, n, CTX_LEN):
        prev = jnp.concatenate([prev[:p], jnp.where(ctx_row == 0, 0.0, prev[p:p + SUBLANE]),
                                prev[p + SUBLANE:]], axis=0)
        nxt = jnp.concatenate([nxt[:p - SUBLANE],
                               jnp.where(ctx_row == SUBLANE - 1, 0.0, nxt[p - SUBLANE:p]), nxt[p:]],
                              axis=0)
    return c[0:1, :] * prev + c[1:2, :] * u + c[2:3, :] * nxt


def _first_axis(i, *_):
    return i


def _halo_specs(rows, width, n_rows, col, tile=_first_axis):
    r8 = rows // SUBLANE
    last8 = n_rows // SUBLANE - 1
    return [
        pl.BlockSpec((rows, width), lambda *g: (tile(*g), col)),
        pl.BlockSpec((SUBLANE, width), lambda *g: (jnp.maximum(tile(*g) * r8 - 1, 0), col)),
        pl.BlockSpec((SUBLANE, width), lambda *g: (jnp.minimum((tile(*g) + 1) * r8, last8), col)),
    ]


def _layer_spec(l, shape):
    return pl.BlockSpec((None,) + shape, lambda *_: (l,) + (0,) * len(shape))


def _mod_spec(l, rows, tile=_first_axis):
    def index(*g):
        r = tile(*g) * rows
        return (l, jnp.where(r < NX, r // SEQ, BATCH), 0, 0)
    return pl.BlockSpec((None, None, 6, D_MODEL), index)


def _ada_kernel(c_ref, w_ref, b_ref, o_ref):
    c = c_ref[...]
    a = c * jax.nn.sigmoid(c)
    o_ref[...] = jnp.dot(a.astype(BF16), w_ref[...].astype(BF16),
                         preferred_element_type=F32) + b_ref[...]


def _ada(cc, ada_w, ada_b):
    n = ada_w.shape[-1]
    return pl.pallas_call(
        _ada_kernel,
        grid=(DEPTH, n // TN_ADA),
        in_specs=[
            pl.BlockSpec((8, D_MODEL), lambda l, j: (0, 0)),
            pl.BlockSpec((None, D_MODEL, TN_ADA), lambda l, j: (l, 0, j)),
            pl.BlockSpec((None, 1, TN_ADA), lambda l, j: (l, 0, j)),
        ],
        out_specs=pl.BlockSpec((None, 8, TN_ADA), lambda l, j: (l, 0, j)),
        out_shape=jax.ShapeDtypeStruct((DEPTH, 8, n), F32),
        compiler_params=_params(("arbitrary", "arbitrary")),
        name="ada",
    )(cc, ada_w, ada_b.reshape(DEPTH, 1, n))


def _in_kernel(split, x_ref, *refs):
    c_ref, (mod_ref, g_ref, w_ref, z_ref, h_ref, r_ref) = ((refs[0], refs[1:]) if split
                                                              else (None, refs))

    def put(src):
        _store_rinv(r_ref, src[...])

        shape = (BF16_ROWS, D_MODEL)
        gain = jnp.broadcast_to(g_ref[...] * (1.0 + mod_ref[1:2, :]), shape)
        shift = jnp.broadcast_to(mod_ref[0:1, :], shape)

        def slab(k, carry):
            rows = pl.ds(pl.multiple_of(k * BF16_ROWS, BF16_ROWS), BF16_ROWS)
            y = src[rows, :] * _load_rinv(r_ref, rows) * gain + shift
            h_ref[rows, :] = y.astype(BF16)
            return carry

        lax.fori_loop(0, TM // BF16_ROWS, slab, 0, unroll=4)

    _pick_rows(pl.program_id(0), TM, x_ref, c_ref, put)
    z_ref[...] = _nt_dot(h_ref[...], w_ref[...])


def _in_proj(l, toks, mods, g, w):
    split = len(toks) == 2
    rows = (_split_rows_specs(TM, NX) if split
            else [pl.BlockSpec((TM, D_MODEL), lambda i: (i, 0))])
    return pl.pallas_call(
        functools.partial(_in_kernel, split),
        grid=(NT // TM,),
        in_specs=rows + [
            _mod_spec(l, TM),
            _layer_spec(l, (1, D_MODEL)),
            pl.BlockSpec((None, IN_COLS_PAD, D_MODEL), lambda i: (l, 0, 0),
                         pipeline_mode=pl.Buffered(1)),
        ],
        out_specs=pl.BlockSpec((TM, IN_COLS_PAD), lambda i: (i, 0)),
        out_shape=jax.ShapeDtypeStruct((NT, IN_COLS_PAD), F32),
        scratch_shapes=[pltpu.VMEM((TM, D_MODEL), BF16), pltpu.VMEM((TM, LANE), F32)],
        compiler_params=_params(("arbitrary",), VMEM_LIMIT_BIG),
        name="in_proj",
    )(*toks, mods, g, w)


def _pool_conv_kernel(a_ref, ap_ref, an_ref, db_ref, dc_ref, dcp_ref, dcn_ref,
                      dh_ref, dhp_ref, dhn_ref, pw_ref, ps_ref, cw_ref,
                      ya_ref, yd_ref, e_ref):
    i = pl.program_id(0)
    pos, seqlen = _seq_pos(i, TS)
    pf, nf = _tile_edges(i, TS)

    halo = SUBLANE
    e_ref[0:halo, :] = ap_ref[...] * pf
    e_ref[halo:halo + TS, :] = a_ref[...]
    e_ref[halo + TS:, :] = an_ref[...] * nf
    for g, w in enumerate(POOL_WINDOWS):
        half = w // 2
        cs = slice(g * LANE, (g + 1) * LANE)
        c, s = e_ref[:, cs], 1
        while s < w:
            c = c + pltpu.roll(c, TS + 2 * halo - s, 0)
            s *= 2
        acc = c[halo - half:halo - half + TS]
        cnt = (jnp.minimum(pos + half, seqlen) - jnp.maximum(pos - half, 0)).astype(F32)
        pooled = acc * (1.0 / cnt) - a_ref[:, cs]
        y = jnp.dot(pooled.astype(BF16), pw_ref[g], preferred_element_type=F32)
        ya_ref[:, cs] = (y * ps_ref[:, cs]).astype(BF16)

    p_prev = dcp_ref[SUBLANE - 1:SUBLANE, :] * dhp_ref[SUBLANE - 1:SUBLANE, :] * pf
    p_next = dcn_ref[0:1, :] * dhn_ref[0:1, :] * nf
    conv = _conv3_rows(dc_ref[...] * dh_ref[...], p_prev, p_next, cw_ref[...], False)
    yd_ref[...] = (db_ref[...] * conv).astype(BF16)


def _pool_conv(l, z, n_rows, pool_w, pool_scale, conv_w):
    ca, cb, cc, ch = (COL_A // GROUP_W, COL_DB // GROUP_W, COL_DC // GROUP_W, COL_DH // GROUP_W)
    out = jax.ShapeDtypeStruct((n_rows, GROUP_W), BF16)
    return pl.pallas_call(
        _pool_conv_kernel,
        grid=(n_rows // TS,),
        in_specs=(_halo_specs(TS, GROUP_W, NT, ca)
                  + [pl.BlockSpec((TS, GROUP_W), lambda i: (i, cb))]
                  + _halo_specs(TS, GROUP_W, NT, cc) + _halo_specs(TS, GROUP_W, NT, ch)
                  + [_layer_spec(l, (len(POOL_WINDOWS), LANE, LANE)),
                     _layer_spec(l, (1, GROUP_W)), _layer_spec(l, (3, GROUP_W))]),
        out_specs=[pl.BlockSpec((TS, GROUP_W), lambda i: (i, 0))] * 2,
        out_shape=[out, out],
        scratch_shapes=[pltpu.VMEM((TS + 2 * SUBLANE, GROUP_W), F32)],
        compiler_params=_params(("arbitrary",)),
        name="pool_conv",
    )(z, z, z, z, z, z, z, z, z, z, pool_w, pool_scale, conv_w)


CBLK = NX // CTX_LEN


def _softmax_pv(s, v):
    e = jnp.exp(s - jnp.max(s, axis=-1, keepdims=True)).astype(BF16)
    return jnp.dot(e, v, preferred_element_type=F32)


def _with_ones(v):
    return jnp.concatenate([v, jnp.ones(v.shape, v.dtype)], axis=-1)


def _cast_rows(w_ref, wb_ref):
    wb_ref[...] = w_ref[...].astype(BF16)


def _cast_rows_pair_chunks(w_ref, wb_ref):
    for k in range(D_FF // TF):
        for part in range(2):
            src = part * D_FF + k * TF
            dst = (2 * k + part) * TF
            wb_ref[:, dst:dst + TF] = w_ref[:, src:src + TF].astype(BF16)


def _attn_calls(lat_fn, ctx_fn, name, lat_ins, lat_specs, ctx_ins, ctx_specs, scratch, with_ctx,
                l, w_f32, cast_fn):
    steps = BATCH * HEADS
    w_rows, w_cols = w_f32.shape[1] // steps, w_f32.shape[2]

    def lat_body(*refs):
        n = len(lat_ins)
        w_ref, (o_ref, wb_ref), rest = refs[n], refs[-len(scratch) - 2:-len(scratch)], refs[-len(scratch):]
        cast_fn(w_ref, wb_ref)
        lat_fn(*refs[:n], o_ref, *rest)

    y0 = [jnp.zeros((NT, GROUP_W), BF16)] if with_ctx else []
    y_shape = jax.ShapeDtypeStruct((NT if with_ctx else NX, GROUP_W), BF16)
    y, w_b = pl.pallas_call(
        lat_body,
        grid=(BATCH, HEADS),
        in_specs=(lat_specs
                  + [pl.BlockSpec((None, w_rows, w_cols), lambda b, h: (l, b * HEADS + h, 0))]
                  + [pl.BlockSpec(memory_space=pl.ANY)] * len(y0)),
        out_specs=[pl.BlockSpec((SEQ, LANE), lambda b, h: (b, h)),
                   pl.BlockSpec((w_rows, w_cols), lambda b, h: (b * HEADS + h, 0))],
        out_shape=[y_shape, jax.ShapeDtypeStruct(w_f32.shape[1:], BF16)],
        input_output_aliases={len(lat_ins) + 1: 0} if with_ctx else {},
        scratch_shapes=scratch,
        compiler_params=_params(("arbitrary", "arbitrary")),
        name=name,
    )(*lat_ins, w_f32, *y0)
    if not with_ctx:
        return y, w_b
    y = pl.pallas_call(
        lambda *refs: ctx_fn(*refs[:len(ctx_ins)], refs[-1]),
        grid=(BATCH,),
        in_specs=ctx_specs + [pl.BlockSpec(memory_space=pl.ANY)],
        out_specs=pl.BlockSpec((CTX_LEN, GROUP_W), lambda b: (CBLK + b, 0)),
        out_shape=y_shape,
        input_output_aliases={len(ctx_ins): 0},
        compiler_params=_params(("arbitrary",)),
        name=name + "_ctx",
    )(*ctx_ins, y)
    return y, w_b


def _diff_lambda(lam_ref, lam_init):
    lp = lam_ref[...]
    return (jnp.exp(jnp.sum(lp[0:1] * lp[1:2], keepdims=True))
            - jnp.exp(jnp.sum(lp[2:3] * lp[3:4], keepdims=True)) + lam_init)


def _diff_scores(q, k):
    q = q * (DIFF_HD ** -0.5)
    lane = lax.broadcasted_iota(jnp.int32, q.shape, 1)
    return (_nt_dot(jnp.where(lane < DIFF_HD, q, 0.0).astype(BF16), k[...]),
            _nt_dot(jnp.where(lane >= DIFF_HD, q, 0.0).astype(BF16), k[...]))


def _diff_out(s, v, lam, g, lam_init):
    o1 = _softmax_pv(s[0], v[...])
    o2 = _softmax_pv(s[1], v[...])
    o = o1[:, :LANE] / o1[:, LANE:] - o2[:, :LANE] * (lam / o2[:, LANE:])
    return (_rms(o, g) * (1.0 - lam_init)).astype(BF16)


def _diff_chain(q, k, v, lam, g, lam_init):
    return _diff_out(_diff_scores(q, k), v, lam, g, lam_init)


def _diff_kernel(lam_init, q_ref, kx_ref, kc_ref, vx_ref, vc_ref, cos_ref, sin_ref, lam_ref, g_ref,
                 o_ref, kb_ref, vb_ref):
    kb_ref[0:SEQ, :] = _rope(kx_ref[...], cos_ref[...], sin_ref[...]).astype(BF16)
    kb_ref[SEQ:, :] = kc_ref[...].astype(BF16)
    vb_ref[0:SEQ, 0:LANE] = vx_ref[...].astype(BF16)
    vb_ref[SEQ:, 0:LANE] = vc_ref[...].astype(BF16)
    vb_ref[:, LANE:] = jnp.ones((SEQ + CTX_LEN, LANE), BF16)
    lam = _diff_lambda(lam_ref, lam_init)
    def scores(rows):
        return _diff_scores(_rope(q_ref[rows, :], cos_ref[rows, :], sin_ref[rows, :]), kb_ref)

    chunks = [slice(r, r + QSUB) for r in range(0, SEQ, QSUB)]
    s = scores(chunks[0])
    for c, rows in enumerate(chunks):
        s_next = scores(chunks[c + 1]) if c + 1 < len(chunks) else None
        o_ref[rows, :] = _diff_out(s, vb_ref, lam, g_ref[...], lam_init)
        s = s_next


def _diff_ctx_kernel(lam_init, q_ref, k_ref, v_ref, lam_ref, g_ref, o_ref):
    lam = _diff_lambda(lam_ref, lam_init)
    for h in range(HEADS):
        cols = slice(h * LANE, (h + 1) * LANE)
        o_ref[:, cols] = _diff_chain(q_ref[:, cols], k_ref[:, cols].astype(BF16),
                                     _with_ones(v_ref[:, cols].astype(BF16)), lam, g_ref[...],
                                     lam_init)


def _diff_attn(l, z, cos_t, sin_t, lam_p, subln_g, lam_init, with_ctx, w_f32):
    cq, ck, cv = COL_Q // LANE, COL_K // LANE, COL_V // LANE

    def xs(col):
        return pl.BlockSpec((SEQ, LANE), lambda b, h: (b, col + h))

    def cs(col):
        return pl.BlockSpec((CTX_LEN, LANE), lambda b, h: (CBLK + b, col + h))

    def all_heads(col):
        return pl.BlockSpec((CTX_LEN, GROUP_W), lambda b: (CBLK + b, col // GROUP_W))

    table = pl.BlockSpec((SEQ, LANE), lambda b, h: (0, 0))
    params = [_layer_spec(l, (4, 64)), _layer_spec(l, (1, LANE))]
    return _attn_calls(
        functools.partial(_diff_kernel, lam_init), functools.partial(_diff_ctx_kernel, lam_init),
        "diff_attn",
        [z, z, z, z, z, cos_t, sin_t, lam_p, subln_g],
        [xs(cq), xs(ck), cs(ck), xs(cv), cs(cv), table, table] + params,
        [z, z, z, lam_p, subln_g],
        [all_heads(COL_Q), all_heads(COL_K), all_heads(COL_V)] + params,
        [pltpu.VMEM((SEQ + CTX_LEN, LANE), BF16), pltpu.VMEM((SEQ + CTX_LEN, 2 * LANE), BF16)],
        with_ctx, l, w_f32, _cast_rows_pair_chunks)


def _mla_prep_kernel(ckv_ref, cq_ref, kr_ref, gq_ref, gkv_ref, wq_ref, wkv_ref, cos_ref, sin_ref,
                     qn_ref, qr_ref, kn_ref, v_ref, kr2_ref):
    cos = cos_ref[...]
    sin = sin_ref[...]
    q = jnp.dot(_rms(cq_ref[...], gq_ref[...]).astype(BF16), wq_ref[...],
                preferred_element_type=F32)
    qn_ref[...] = q[:, :GROUP_W].astype(BF16)
    for c in range(0, 2 * LANE, LANE):
        qr_ref[:, c:c + LANE] = _rope(q[:, GROUP_W + c:GROUP_W + c + LANE], cos, sin).astype(BF16)
    kv = jnp.dot(_rms(ckv_ref[...], gkv_ref[...]).astype(BF16), wkv_ref[...],
                 preferred_element_type=F32)
    kn_ref[...] = kv[:, :GROUP_W].astype(BF16)
    v_ref[...] = kv[:, GROUP_W:].astype(BF16)
    kr = _rope(kr_ref[...], cos, sin)
    kr2_ref[...] = (kr + pltpu.roll(kr, ROPE_DIM, 1)).astype(BF16)


def _mla_prep(l, z, gq, gkv, wq, wkv, cos_t, sin_t):
    def out(w):
        return jax.ShapeDtypeStruct((NT, w), BF16)

    def row(w):
        return pl.BlockSpec((TM, w), lambda i: (i, 0))

    rope = pl.BlockSpec((TM, LANE), lambda i: (jnp.where(i * TM < NX, i % (SEQ // TM), SEQ // TM), 0))
    return pl.pallas_call(
        _mla_prep_kernel,
        grid=(NT // TM,),
        in_specs=[
            pl.BlockSpec((TM, MLA_KV_LORA), lambda i: (i, COL_CKV // MLA_KV_LORA)),
            pl.BlockSpec((TM, MLA_Q_LORA), lambda i: (i, COL_CQ // MLA_Q_LORA)),
            pl.BlockSpec((TM, LANE), lambda i: (i, COL_KR // LANE)),
            _layer_spec(l, (1, MLA_Q_LORA)),
            _layer_spec(l, (1, MLA_KV_LORA)),
            _layer_spec(l, (MLA_Q_LORA, HEADS * (MLA_NOPE + ROPE_DIM))),
            _layer_spec(l, (MLA_KV_LORA, HEADS * (MLA_NOPE + MLA_VD))),
            rope, rope,
        ],
        out_specs=[row(GROUP_W), row(2 * LANE), row(GROUP_W), row(GROUP_W), row(LANE)],
        out_shape=[out(GROUP_W), out(2 * LANE), out(GROUP_W), out(GROUP_W), out(LANE)],
        compiler_params=_params(("arbitrary",)),
        name="mla_prep",
    )(z, z, z, gq, gkv, wq, wkv, cos_t, sin_t)


def _mla_scores(h, qn, qr, k):
    lane = lax.broadcasted_iota(jnp.int32, qr.shape, 1)
    lo = (h % 2) * ROPE_DIM
    qr = jnp.where((lane >= lo) & (lane < lo + ROPE_DIM), qr, jnp.zeros_like(qr))
    return _nt_dot(jnp.concatenate([qn, qr], axis=-1), k[...]) * ((MLA_NOPE + ROPE_DIM) ** -0.5)


def _mla_out(s, v):
    o = _softmax_pv(s, v[...])
    return (o[:, :LANE] / o[:, LANE:]).astype(BF16)


def _mla_chain(h, qn, qr, k, v):
    return _mla_out(_mla_scores(h, qn, qr, k), v)


def _mla_kernel(qn_ref, qr_ref, knx_ref, knc_ref, krx_ref, krc_ref, vx_ref, vc_ref,
                o_ref, kb_ref, vb_ref):
    h = pl.program_id(1)
    kb_ref[0:SEQ, 0:LANE] = knx_ref[...]
    kb_ref[SEQ:, 0:LANE] = knc_ref[...]
    kb_ref[0:SEQ, LANE:] = krx_ref[...]
    kb_ref[SEQ:, LANE:] = krc_ref[...]
    vb_ref[0:SEQ, 0:LANE] = vx_ref[...]
    vb_ref[SEQ:, 0:LANE] = vc_ref[...]
    vb_ref[:, LANE:] = jnp.ones((SEQ + CTX_LEN, LANE), BF16)
    def scores(rows):
        return _mla_scores(h, qn_ref[rows, :], qr_ref[rows, :], kb_ref)

    chunks = [slice(r, r + QSUB) for r in range(0, SEQ, QSUB)]
    s = scores(chunks[0])
    for c, rows in enumerate(chunks):
        s_next = scores(chunks[c + 1]) if c + 1 < len(chunks) else None
        o_ref[rows, :] = _mla_out(s, vb_ref)
        s = s_next


def _mla_ctx_kernel(qn_ref, qr_ref, kn_ref, kr_ref, v_ref, o_ref):
    for h in range(HEADS):
        cols = slice(h * LANE, (h + 1) * LANE)
        pair = slice(h // 2 * LANE, (h // 2 + 1) * LANE)
        k = jnp.concatenate([kn_ref[:, cols], kr_ref[...]], axis=-1)
        o_ref[:, cols] = _mla_chain(h, qn_ref[:, cols], qr_ref[:, pair], k,
                                    _with_ones(v_ref[:, cols]))


def _mla_attn(l, qn, qr, kn, v, kr2, with_ctx, w_f32):
    def xs(col):
        return pl.BlockSpec((SEQ, LANE), lambda b, h: (b, col(h)))

    def cs(col):
        return pl.BlockSpec((CTX_LEN, LANE), lambda b, h: (CBLK + b, col(h)))

    def whole(a):
        return pl.BlockSpec((CTX_LEN, a.shape[1]), lambda b: (CBLK + b, 0))

    head, pair, first = (lambda h: h), (lambda h: h // 2), (lambda h: 0)
    return _attn_calls(
        _mla_kernel, _mla_ctx_kernel, "mla_attn",
        [qn, qr, kn, kn, kr2, kr2, v, v],
        [xs(head), xs(pair), xs(head), cs(head), xs(first), cs(first), xs(head), cs(head)],
        [qn, qr, kn, kr2, v],
        [whole(qn), whole(qr), whole(kn), whole(kr2), whole(v)],
        [pltpu.VMEM((SEQ + CTX_LEN, 2 * LANE), BF16), pltpu.VMEM((SEQ + CTX_LEN, 2 * LANE), BF16)],
        with_ctx, l, w_f32, _cast_rows)


def _out_kernel(split, ya_ref, yb_ref, ym_ref, yd_ref, w_ref, x_ref, *refs):
    c_ref, (mod_ref, gpost_ref, xo_ref, wb_ref) = (refs[0], refs[1:]) if split else (None, refs)

    @pl.when(pl.program_id(0) == 0)
    def _():
        wb_ref[...] = w_ref[...].astype(BF16)

    y = jnp.dot(ya_ref[...], wb_ref[0:GROUP_W, :], preferred_element_type=F32)
    y += jnp.dot(yb_ref[...], wb_ref[GROUP_W:2 * GROUP_W, :], preferred_element_type=F32)
    y += jnp.dot(ym_ref[...], wb_ref[2 * GROUP_W:3 * GROUP_W, :], preferred_element_type=F32)
    y += jnp.dot(yd_ref[...], wb_ref[3 * GROUP_W:, :], preferred_element_type=F32)
    delta = mod_ref[2:3, :] * _rms(y, gpost_ref[...])

    def add_residual(src):
        xo_ref[...] = src[...] + delta

    _pick_rows(pl.program_id(0), TM, x_ref, c_ref, add_residual)


def _out_proj(l, ya, yb, ym, yd, w, toks, mods, g_post):
    n = ya.shape[0]
    split = len(toks) == 2
    y_spec = pl.BlockSpec((TM, GROUP_W), lambda i: (i, 0))
    row = pl.BlockSpec((TM, D_MODEL), lambda i: (i, 0))
    return pl.pallas_call(
        functools.partial(_out_kernel, split),
        grid=(n // TM,),
        in_specs=[
            y_spec, y_spec, y_spec, y_spec,
            pl.BlockSpec((None, D_MODEL, D_MODEL), lambda i: (l, 0, 0),
                         pipeline_mode=pl.Buffered(1)),
        ] + (_split_rows_specs(TM, NX) if split else [row]) + [
            _mod_spec(l, TM),
            _layer_spec(l, (1, D_MODEL)),
        ],
        out_specs=row,
        out_shape=jax.ShapeDtypeStruct((n, D_MODEL), F32),
        scratch_shapes=[pltpu.VMEM((D_MODEL, D_MODEL), BF16)],
        compiler_params=_params(("arbitrary",)),
        name="out_proj",
    )(ya, yb, ym, yd, w, *toks, mods, g_post)


def _ffn_kernel(x_ref, xp_ref, xn_ref, mod_ref, gpre_ref, gpost_ref, wgv_ref, cg_ref, cv_ref,
                wd_ref, xo_ref, hx_ref, r_ref):
    i = pl.program_id(0)
    j = pl.program_id(1)
    n = TM_BIG

    @pl.when(j == 0)
    def _():
        slab_shape = (BF16_ROWS, D_MODEL)
        gain = jnp.broadcast_to(gpre_ref[...] * (1.0 + mod_ref[4:5, :]), slab_shape)
        shift = jnp.broadcast_to(mod_ref[3:4, :], slab_shape)

        def hmod(x, r):
            return x * r * gain + shift

        def rinv(x):
            return lax.rsqrt(jnp.mean(x * x, axis=-1, keepdims=True) + NORM_EPS)

        _store_rinv(r_ref, x_ref[...])

        def slab(k, carry):
            rows = pl.ds(pl.multiple_of(k * BF16_ROWS, BF16_ROWS), BF16_ROWS)
            dst = pl.ds(pl.multiple_of((k + 1) * BF16_ROWS, BF16_ROWS), BF16_ROWS)
            hx_ref[dst, :] = hmod(x_ref[rows, :], _load_rinv(r_ref, rows)).astype(BF16)
            return carry

        lax.fori_loop(0, n // BF16_ROWS, slab, 0, unroll=4)
        has_prev, has_next = _tile_edges(i, n)
        halo = jnp.concatenate([xp_ref[...], xn_ref[...]], axis=0)
        keep = jnp.where(lax.broadcasted_iota(jnp.int32, (BF16_ROWS, 1), 0) < SUBLANE,
                         has_prev, has_next)
        halo = hmod(halo, rinv(halo)) * keep
        zeros = jnp.zeros((SUBLANE, D_MODEL), F32)
        hx_ref[:BF16_ROWS, :] = jnp.concatenate([zeros, halo[:SUBLANE]], axis=0).astype(BF16)
        hx_ref[BF16_ROWS + n:, :] = jnp.concatenate([halo[SUBLANE:], zeros], axis=0).astype(BF16)
        xo_ref[...] = jnp.zeros_like(xo_ref)

    is_ctx = i * n >= NX
    same_seq = jnp.where(is_ctx, 0.0, 1.0)
    half = n // 2
    us = [jnp.dot(hx_ref[r0:r0 + half + 2 * BF16_ROWS, :], wgv_ref[...],
                  preferred_element_type=F32) for r0 in (0, half)]
    for part, u in enumerate(us):
        r0 = part * half
        prev_row = u[BF16_ROWS - 1:BF16_ROWS]
        next_row = u[BF16_ROWS + half:BF16_ROWS + half + 1]
        if part == 0:
            next_row = next_row * same_seq
        else:
            prev_row = prev_row * same_seq
        main = u[BF16_ROWS:BF16_ROWS + half]

        def conv(cols, c_ref):
            return _conv3_rows(main[:, cols], prev_row[:, cols], next_row[:, cols], c_ref[...],
                               is_ctx)

        gate = conv(slice(0, TF), cg_ref)
        val = conv(slice(TF, 2 * TF), cv_ref)
        act = (gate * jax.nn.sigmoid(gate) * val).astype(BF16)
        xo_ref[r0:r0 + half, :] += jnp.dot(act, wd_ref[...], preferred_element_type=F32)

    @pl.when(j == pl.num_programs(1) - 1)
    def _():
        _store_rinv(r_ref, xo_ref[...])
        slab_shape = (BF16_ROWS, D_MODEL)
        gain = jnp.broadcast_to(gpost_ref[...] * mod_ref[5:6, :], slab_shape)

        def slab(k, carry):
            rows = pl.ds(pl.multiple_of(k * BF16_ROWS, BF16_ROWS), BF16_ROWS)
            xo_ref[rows, :] = x_ref[rows, :] + xo_ref[rows, :] * _load_rinv(r_ref, rows) * gain
            return carry

        lax.fori_loop(0, n // BF16_ROWS, slab, 0, unroll=4)


def _ffn(l, tok, mods, g_pre, g_post, w_up, conv_w, w_down, n_rows):
    nj = D_FF // TF
    return pl.pallas_call(
        _ffn_kernel,
        grid=(n_rows // TM_BIG, nj),
        in_specs=(_halo_specs(TM_BIG, D_MODEL, n_rows, 0) + [
            _mod_spec(l, TM_BIG),
            _layer_spec(l, (1, D_MODEL)),
            _layer_spec(l, (1, D_MODEL)),
            pl.BlockSpec((D_MODEL, 2 * TF), lambda i, j: (0, j)),
            pl.BlockSpec((None, 3, TF), lambda i, j: (l, 0, j)),
            pl.BlockSpec((None, 3, TF), lambda i, j: (l, 0, j + nj)),
            pl.BlockSpec((TF, D_MODEL), lambda i, j: (j, 0)),
        ]),
        out_specs=pl.BlockSpec((TM_BIG, D_MODEL), lambda i, j: (i, 0)),
        out_shape=jax.ShapeDtypeStruct((n_rows, D_MODEL), F32),
        scratch_shapes=[pltpu.VMEM((TM_BIG + 2 * BF16_ROWS, D_MODEL), BF16),
                        pltpu.VMEM((TM_BIG, LANE), F32)],
        compiler_params=_params(("arbitrary", "arbitrary"), VMEM_LIMIT_BIG),
        name="ffn",
    )(tok, tok, tok, mods, g_pre, g_post, w_up, conv_w, conv_w, w_down)


def _rope_tables():
    rows = SEQ // GRID_W
    row = jnp.repeat(jnp.arange(rows), GRID_W).astype(F32)
    col = jnp.tile(jnp.arange(GRID_W), rows).astype(F32)
    n_freq = ROPE_DIM // 4
    inv = ROPE_THETA ** (-jnp.arange(n_freq, dtype=F32) / n_freq)
    ang = jnp.concatenate([row[:, None] * inv, col[:, None] * inv], axis=-1)
    cos, sin = jnp.cos(ang), jnp.sin(ang)
    cos_t = jnp.concatenate([cos, cos, cos, cos], axis=-1)
    sin_t = jnp.concatenate([-sin, sin, -sin, sin], axis=-1)
    cos_t = jnp.concatenate([cos_t, jnp.ones((TM, LANE), F32)], axis=0)
    sin_t = jnp.concatenate([sin_t, jnp.zeros((TM, LANE), F32)], axis=0)
    return cos_t, sin_t


W_IN_KB = 256


def _w_in_kernel(w_ref, o_ref):
    dst_of = {"pool_qkv": COL_A, "conv": COL_DB, "c_kv": COL_CKV, "c_q": COL_CQ, "k_rope": COL_KR}
    src = 0
    for name, n in REF_SEGMENTS:
        dst = dst_of[name]
        o_ref[dst:dst + n, :] = w_ref[src:src + n, :].astype(BF16)
        src += n
    end = COL_KR + ROPE_DIM
    o_ref[end:, :] = jnp.zeros((IN_COLS_PAD - end, W_IN_KB), BF16)


def _layout_w_in(w):
    wt = jnp.swapaxes(w, 1, 2)
    n, k = wt.shape[1:]
    return pl.pallas_call(
        _w_in_kernel,
        grid=(DEPTH, k // W_IN_KB),
        in_specs=[pl.BlockSpec((None, n, W_IN_KB), lambda l, i: (l, 0, i))],
        out_specs=pl.BlockSpec((None, IN_COLS_PAD, W_IN_KB), lambda l, i: (l, 0, i)),
        out_shape=jax.ShapeDtypeStruct((DEPTH, IN_COLS_PAD, k), BF16),
        compiler_params=_params(("arbitrary", "arbitrary")),
        name="w_in_layout",
    )(wt)


def _layout_w_uq(w):
    w = w.reshape(DEPTH, MLA_Q_LORA, HEADS, MLA_NOPE + ROPE_DIM)
    return jnp.concatenate([w[..., :MLA_NOPE].reshape(DEPTH, MLA_Q_LORA, HEADS * MLA_NOPE),
                            w[..., MLA_NOPE:].reshape(DEPTH, MLA_Q_LORA, HEADS * ROPE_DIM)],
                           axis=-1).astype(BF16)


def _layout_w_ukv(w):
    w = w.reshape(DEPTH, MLA_KV_LORA, HEADS, MLA_NOPE + MLA_VD)
    return jnp.concatenate([w[..., :MLA_NOPE].reshape(DEPTH, MLA_KV_LORA, HEADS * MLA_NOPE),
                            w[..., MLA_NOPE:].reshape(DEPTH, MLA_KV_LORA, HEADS * MLA_VD)],
                           axis=-1).astype(BF16)


def kernel(x, c, ctx, c_ctx, ada_w, ada_b, g_pre_mix, g_post_mix, g_pre_ffn, g_post_ffn, w_in,
           pool_w, pool_scale, diff_lambda, diff_subln_g, mla_gq, mla_w_uq, mla_gkv, mla_w_ukv,
           conv_w, w_out, ffn_w_up, ffn_conv_w, ffn_w_down):
    toks = (x.reshape(NX, D_MODEL), ctx.reshape(NC, D_MODEL))
    cc = jnp.concatenate([c, c_ctx[None, :], jnp.zeros((3, D_MODEL), F32)], axis=0)
    mods = _ada(cc, ada_w, ada_b).reshape(DEPTH, 8, 6, D_MODEL)
    cos_t, sin_t = _rope_tables()

    def vec(p):
        return p.reshape(DEPTH, 1, p.shape[-1])

    w_in_b = _layout_w_in(w_in)
    w_uq_b, w_ukv_b = _layout_w_uq(mla_w_uq), _layout_w_ukv(mla_w_ukv)
    pool_w_b = pool_w.astype(BF16)

    for l in range(DEPTH):
        last = l == DEPTH - 1
        n_rows = NX if last else NT
        lam_init = 0.8 - 0.6 * math.exp(-0.3 * l)

        z = _in_proj(l, toks, mods, vec(g_pre_mix), w_in_b)
        ya, yd = _pool_conv(l, z, n_rows, pool_w_b, vec(pool_scale), conv_w)
        yb, w_up_b = _diff_attn(l, z, cos_t, sin_t, diff_lambda, vec(diff_subln_g), lam_init,
                                not last, ffn_w_up)
        qn, qr, kn, v, kr2 = _mla_prep(l, z, vec(mla_gq), vec(mla_gkv), w_uq_b, w_ukv_b, cos_t, sin_t)
        ym, w_down_b = _mla_attn(l, qn, qr, kn, v, kr2, not last, ffn_w_down)
        tok = _out_proj(l, ya, yb, ym, yd, w_out, toks, mods, vec(g_post_mix))
        tok = _ffn(l, tok, mods, vec(g_pre_ffn), vec(g_post_ffn), w_up_b, ffn_conv_w, w_down_b, n_rows)
        toks = (tok,)
    return tok.reshape(BATCH, SEQ, D_MODEL)
```

```python
import functools
import math

import jax
import jax.numpy as jnp
from jax import lax
from jax.experimental import pallas as pl
from jax.experimental.pallas import tpu as pltpu

F32 = jnp.float32
BF16 = jnp.bfloat16

D_MODEL = 2048
BATCH = 4
SEQ = 2048
CTX_LEN = 256
GRID_W = 64
DEPTH = 2
NORM_EPS = 1e-6
ROPE_THETA = 10000.0
GROUP_W = 512
POOL_WINDOWS = (2, 4, 8, 16)
HEADS = 4
D_FF = 5632
ROPE_DIM = 64
DIFF_HD = 64
MLA_NOPE, MLA_VD = 128, 128
MLA_Q_LORA, MLA_KV_LORA = 384, 256
REF_SEGMENTS = (("pool_qkv", 2048), ("c_q", MLA_Q_LORA), ("c_kv", MLA_KV_LORA),
                ("k_rope", ROPE_DIM), ("conv", 1536))

NX = BATCH * SEQ
NC = BATCH * CTX_LEN
NT = NX + NC

COL_A, COL_Q, COL_K, COL_V = 0, 512, 1024, 1536
COL_DB, COL_DC, COL_DH = 2048, 2560, 3072
COL_CKV, COL_CQ, COL_KR = 3584, 3840, 4224
IN_COLS_PAD = 4352

LANE = 128
SUBLANE = 8
BF16_ROWS = 2 * SUBLANE
TM = 512
TM_BIG = 1024
TS = 256
QSUB = 256
TF = 512
TN_ADA = 2048
VMEM_LIMIT = 56 * 1024 * 1024
VMEM_LIMIT_BIG = 62 * 1024 * 1024


def _params(sem, vmem_limit=VMEM_LIMIT):
    return pltpu.CompilerParams(dimension_semantics=sem, vmem_limit_bytes=vmem_limit)


def _split_rows_specs(rows, n_lat_rows):
    n_lat = n_lat_rows // rows
    return [pl.BlockSpec((rows, D_MODEL), lambda i, *_: (jnp.minimum(i, n_lat - 1), 0)),
            pl.BlockSpec((rows, D_MODEL), lambda i, *_: (jnp.maximum(i - n_lat, 0), 0),
                         pipeline_mode=pl.Buffered(1))]


def _pick_rows(i, rows, x_ref, c_ref, fn):
    if c_ref is None:
        fn(x_ref)
        return

    @pl.when(i * rows < NX)
    def _():
        fn(x_ref)

    @pl.when(i * rows >= NX)
    def _():
        fn(c_ref)


def _rms(x, g):
    ms = jnp.mean(x * x, axis=-1, keepdims=True)
    return x * lax.rsqrt(ms + NORM_EPS) * g


def _store_rinv(r_ref, x):
    r = lax.rsqrt(jnp.mean(x * x, axis=-1, keepdims=True) + NORM_EPS)
    r_ref[...] = jnp.broadcast_to(r, r_ref.shape)


def _load_rinv(r_ref, rows):
    return jnp.tile(r_ref[rows, :], (1, D_MODEL // LANE))


def _nt_dot(a, b):
    return lax.dot_general(a, b, (((1,), (1,)), ((), ())), preferred_element_type=F32)


def _rope(v, cos, sin):
    lane = lax.broadcasted_iota(jnp.int32, v.shape, 1)
    half = ROPE_DIM // 2
    sw = jnp.where((lane & half) == 0, pltpu.roll(v, LANE - half, 1), pltpu.roll(v, half, 1))
    return v * cos + sw * sin


def _seq_pos(tile, rows):
    r = tile * rows + lax.broadcasted_iota(jnp.int32, (rows, 1), 0)
    seqlen = jnp.where(r < NX, SEQ, CTX_LEN)
    return r & (seqlen - 1), seqlen


def _tile_edges(tile, rows):
    r0 = tile * rows
    seqlen = jnp.where(r0 < NX, SEQ, CTX_LEN)
    has_prev = jnp.where((r0 & (seqlen - 1)) == 0, 0.0, 1.0)
    has_next = jnp.where(((r0 + rows) & (seqlen - 1)) == 0, 0.0, 1.0)
    return has_prev, has_next


def _conv3_rows(u, prev_row, next_row, c, is_ctx):
    n = u.shape[0]
    row8 = lax.broadcasted_iota(jnp.int32, (SUBLANE, 1), 0)
    prev = pltpu.roll(u, 1, 0)
    prev = jnp.concatenate([jnp.where(row8 == 0, prev_row, prev[:SUBLANE]), prev[SUBLANE:]], axis=0)
    nxt = pltpu.roll(u, n - 1, 0)
    nxt = jnp.concatenate([nxt[:n - SUBLANE],
                           jnp.where(row8 == SUBLANE - 1, next_row, nxt[n - SUBLANE:])], axis=0)
    ctx_row = row8 + jnp.where(is_ctx, 0, 2 * SUBLANE)
    for p in range(CTX_LEN, n, CTX_LEN):
        prev = jnp.concatenate([prev[:p], jnp.where(ctx_row == 0, 0.0, prev[p:p + SUBLANE]),
                                prev[p + SUBLANE:]], axis=0)
        nxt = jnp.concatenate([nxt[:p - SUBLANE],
                               jnp.where(ctx_row == SUBLANE - 1, 0.0, nxt[p - SUBLANE:p]), nxt[p:]],
                              axis=0)
    return c[0:1, :] * prev + c[1:2, :] * u + c[2:3, :] * nxt


def _first_axis(i, *_):
    return i


def _halo_specs(rows, width, n_rows, col, tile=_first_axis):
    r8 = rows // SUBLANE
    last8 = n_rows // SUBLANE - 1
    return [
        pl.BlockSpec((rows, width), lambda *g: (tile(*g), col)),
        pl.BlockSpec((SUBLANE, width), lambda *g: (jnp.maximum(tile(*g) * r8 - 1, 0), col)),
        pl.BlockSpec((SUBLANE, width), lambda *g: (jnp.minimum((tile(*g) + 1) * r8, last8), col)),
    ]


def _layer_spec(l, shape):
    return pl.BlockSpec((None,) + shape, lambda *_: (l,) + (0,) * len(shape))


def _mod_spec(l, rows, tile=_first_axis):
    def index(*g):
        r = tile(*g) * rows
        return (l, jnp.where(r < NX, r // SEQ, BATCH), 0, 0)
    return pl.BlockSpec((None, None, 6, D_MODEL), index)


def _ada_kernel(c_ref, w_ref, b_ref, o_ref):
    c = c_ref[...]
    a = c * jax.nn.sigmoid(c)
    o_ref[...] = jnp.dot(a.astype(BF16), w_ref[...].astype(BF16),
                         preferred_element_type=F32) + b_ref[...]


def _ada(cc, ada_w, ada_b):
    n = ada_w.shape[-1]
    return pl.pallas_call(
        _ada_kernel,
        grid=(DEPTH, n // TN_ADA),
        in_specs=[
            pl.BlockSpec((8, D_MODEL), lambda l, j: (0, 0)),
            pl.BlockSpec((None, D_MODEL, TN_ADA), lambda l, j: (l, 0, j)),
            pl.BlockSpec((None, 1, TN_ADA), lambda l, j: (l, 0, j)),
        ],
        out_specs=pl.BlockSpec((None, 8, TN_ADA), lambda l, j: (l, 0, j)),
        out_shape=jax.ShapeDtypeStruct((DEPTH, 8, n), F32),
        compiler_params=_params(("arbitrary", "arbitrary")),
        name="ada",
    )(cc, ada_w, ada_b.reshape(DEPTH, 1, n))


def _in_kernel(split, x_ref, *refs):
    c_ref, (mod_ref, g_ref, w_ref, z_ref, h_ref, r_ref) = ((refs[0], refs[1:]) if split
                                                              else (None, refs))

    def put(src):
        _store_rinv(r_ref, src[...])

        shape = (BF16_ROWS, D_MODEL)
        gain = jnp.broadcast_to(g_ref[...] * (1.0 + mod_ref[1:2, :]), shape)
        shift = jnp.broadcast_to(mod_ref[0:1, :], shape)

        def slab(k, carry):
            rows = pl.ds(pl.multiple_of(k * BF16_ROWS, BF16_ROWS), BF16_ROWS)
            y = src[rows, :] * _load_rinv(r_ref, rows) * gain + shift
            h_ref[rows, :] = y.astype(BF16)
            return carry

        lax.fori_loop(0, TM // BF16_ROWS, slab, 0, unroll=4)

    _pick_rows(pl.program_id(0), TM, x_ref, c_ref, put)
    z_ref[...] = _nt_dot(h_ref[...], w_ref[...])


def _in_proj(l, toks, mods, g, w):
    split = len(toks) == 2
    rows = (_split_rows_specs(TM, NX) if split
            else [pl.BlockSpec((TM, D_MODEL), lambda i: (i, 0))])
    return pl.pallas_call(
        functools.partial(_in_kernel, split),
        grid=(NT // TM,),
        in_specs=rows + [
            _mod_spec(l, TM),
            _layer_spec(l, (1, D_MODEL)),
            pl.BlockSpec((IN_COLS_PAD, D_MODEL), lambda i: (0, 0), pipeline_mode=pl.Buffered(1)),
        ],
        out_specs=pl.BlockSpec((TM, IN_COLS_PAD), lambda i: (i, 0)),
        out_shape=jax.ShapeDtypeStruct((NT, IN_COLS_PAD), F32),
        scratch_shapes=[pltpu.VMEM((TM, D_MODEL), BF16), pltpu.VMEM((TM, LANE), F32)],
        compiler_params=_params(("arbitrary",), VMEM_LIMIT_BIG),
        name="in_proj",
    )(*toks, mods, g, w)


def _pool_conv_kernel(a_ref, ap_ref, an_ref, db_ref, dc_ref, dcp_ref, dcn_ref,
                      dh_ref, dhp_ref, dhn_ref, pw_ref, ps_ref, cw_ref,
                      ya_ref, yd_ref, e_ref):
    i = pl.program_id(0)
    pos, seqlen = _seq_pos(i, TS)
    pf, nf = _tile_edges(i, TS)

    halo = SUBLANE
    e_ref[0:halo, :] = ap_ref[...] * pf
    e_ref[halo:halo + TS, :] = a_ref[...]
    e_ref[halo + TS:, :] = an_ref[...] * nf
    for g, w in enumerate(POOL_WINDOWS):
        half = w // 2
        cs = slice(g * LANE, (g + 1) * LANE)
        c, s = e_ref[:, cs], 1
        while s < w:
            c = c + pltpu.roll(c, TS + 2 * halo - s, 0)
            s *= 2
        acc = c[halo - half:halo - half + TS]
        cnt = (jnp.minimum(pos + half, seqlen) - jnp.maximum(pos - half, 0)).astype(F32)
        pooled = acc * (1.0 / cnt) - a_ref[:, cs]
        y = jnp.dot(pooled.astype(BF16), pw_ref[g], preferred_element_type=F32)
        ya_ref[:, cs] = (y * ps_ref[:, cs]).astype(BF16)

    p_prev = dcp_ref[SUBLANE - 1:SUBLANE, :] * dhp_ref[SUBLANE - 1:SUBLANE, :] * pf
    p_next = dcn_ref[0:1, :] * dhn_ref[0:1, :] * nf
    conv = _conv3_rows(dc_ref[...] * dh_ref[...], p_prev, p_next, cw_ref[...], False)
    yd_ref[...] = (db_ref[...] * conv).astype(BF16)


def _pool_conv(l, z, n_rows, pool_w, pool_scale, conv_w):
    ca, cb, cc, ch = (COL_A // GROUP_W, COL_DB // GROUP_W, COL_DC // GROUP_W, COL_DH // GROUP_W)
    out = jax.ShapeDtypeStruct((n_rows, GROUP_W), BF16)
    return pl.pallas_call(
        _pool_conv_kernel,
        grid=(n_rows // TS,),
        in_specs=(_halo_specs(TS, GROUP_W, NT, ca)
                  + [pl.BlockSpec((TS, GROUP_W), lambda i: (i, cb))]
                  + _halo_specs(TS, GROUP_W, NT, cc) + _halo_specs(TS, GROUP_W, NT, ch)
                  + [_layer_spec(l, (len(POOL_WINDOWS), LANE, LANE)),
                     _layer_spec(l, (1, GROUP_W)), _layer_spec(l, (3, GROUP_W))]),
        out_specs=[pl.BlockSpec((TS, GROUP_W), lambda i: (i, 0))] * 2,
        out_shape=[out, out],
        scratch_shapes=[pltpu.VMEM((TS + 2 * SUBLANE, GROUP_W), F32)],
        compiler_params=_params(("arbitrary",)),
        name="pool_conv",
    )(z, z, z, z, z, z, z, z, z, z, pool_w, pool_scale, conv_w)


CBLK = NX // CTX_LEN


def _softmax_pv(s, v):
    e = jnp.exp(s - jnp.max(s, axis=-1, keepdims=True)).astype(BF16)
    return jnp.dot(e, v, preferred_element_type=F32)


def _with_ones(v):
    return jnp.concatenate([v, jnp.ones(v.shape, v.dtype)], axis=-1)


def _cast_rows(w_ref, wb_ref):
    wb_ref[...] = w_ref[...].astype(BF16)


def _cast_rows_pair_chunks(w_ref, wb_ref):
    for k in range(D_FF // TF):
        for part in range(2):
            src = part * D_FF + k * TF
            dst = (2 * k + part) * TF
            wb_ref[:, dst:dst + TF] = w_ref[:, src:src + TF].astype(BF16)


def _attn_calls(lat_fn, ctx_fn, name, lat_ins, lat_specs, ctx_ins, ctx_specs, scratch, with_ctx,
                l, w_f32, cast_fn, w_in_t=None):
    steps = BATCH * HEADS
    w_rows, w_cols = w_f32.shape[1] // steps, w_f32.shape[2]
    riders = [(w_f32,
               pl.BlockSpec((None, w_rows, w_cols), lambda b, h: (l, b * HEADS + h, 0)),
               pl.BlockSpec((w_rows, w_cols), lambda b, h: (b * HEADS + h, 0)),
               jax.ShapeDtypeStruct(w_f32.shape[1:], BF16), cast_fn)]
    if w_in_t is not None:
        kb = D_MODEL // steps
        riders.append((w_in_t,
                       pl.BlockSpec((None, w_in_t.shape[1], kb),
                                    lambda b, h: (l + 1, 0, b * HEADS + h)),
                       pl.BlockSpec((IN_COLS_PAD, kb), lambda b, h: (0, b * HEADS + h)),
                       jax.ShapeDtypeStruct((IN_COLS_PAD, D_MODEL), BF16), _w_in_kernel))
    n_r, n_s = len(riders), len(scratch)

    def lat_body(*refs):
        n = len(lat_ins)
        outs = refs[len(refs) - n_s - 1 - n_r:len(refs) - n_s]
        for (_, _, _, _, body), src, dst in zip(riders, refs[n:n + n_r], outs[1:]):
            body(src, dst)
        lat_fn(*refs[:n], outs[0], *refs[len(refs) - n_s:])

    y0 = [jnp.zeros((NT, GROUP_W), BF16)] if with_ctx else []
    y_shape = jax.ShapeDtypeStruct((NT if with_ctx else NX, GROUP_W), BF16)
    y, *w_b = pl.pallas_call(
        lat_body,
        grid=(BATCH, HEADS),
        in_specs=(lat_specs + [r[1] for r in riders]
                  + [pl.BlockSpec(memory_space=pl.ANY)] * len(y0)),
        out_specs=[pl.BlockSpec((SEQ, LANE), lambda b, h: (b, h))] + [r[2] for r in riders],
        out_shape=[y_shape] + [r[3] for r in riders],
        input_output_aliases={len(lat_ins) + n_r: 0} if with_ctx else {},
        scratch_shapes=scratch,
        compiler_params=_params(("arbitrary", "arbitrary")),
        name=name,
    )(*lat_ins, *[r[0] for r in riders], *y0)
    if not with_ctx:
        return y, w_b
    y = pl.pallas_call(
        lambda *refs: ctx_fn(*refs[:len(ctx_ins)], refs[-1]),
        grid=(BATCH,),
        in_specs=ctx_specs + [pl.BlockSpec(memory_space=pl.ANY)],
        out_specs=pl.BlockSpec((CTX_LEN, GROUP_W), lambda b: (CBLK + b, 0)),
        out_shape=y_shape,
        input_output_aliases={len(ctx_ins): 0},
        compiler_params=_params(("arbitrary",)),
        name=name + "_ctx",
    )(*ctx_ins, y)
    return y, w_b


def _diff_lambda(lam_ref, lam_init):
    lp = lam_ref[...]
    return (jnp.exp(jnp.sum(lp[0:1] * lp[1:2], keepdims=True))
            - jnp.exp(jnp.sum(lp[2:3] * lp[3:4], keepdims=True)) + lam_init)


def _diff_scores(q, k):
    q = q * (DIFF_HD ** -0.5)
    lane = lax.broadcasted_iota(jnp.int32, q.shape, 1)
    return (_nt_dot(jnp.where(lane < DIFF_HD, q, 0.0).astype(BF16), k[...]),
            _nt_dot(jnp.where(lane >= DIFF_HD, q, 0.0).astype(BF16), k[...]))


def _diff_out(s, v, lam, g, lam_init):
    o1 = _softmax_pv(s[0], v[...])
    o2 = _softmax_pv(s[1], v[...])
    o = o1[:, :LANE] / o1[:, LANE:] - o2[:, :LANE] * (lam / o2[:, LANE:])
    return (_rms(o, g) * (1.0 - lam_init)).astype(BF16)


def _diff_chain(q, k, v, lam, g, lam_init):
    return _diff_out(_diff_scores(q, k), v, lam, g, lam_init)


def _diff_kernel(lam_init, q_ref, kx_ref, kc_ref, vx_ref, vc_ref, cos_ref, sin_ref, lam_ref, g_ref,
                 o_ref, kb_ref, vb_ref):
    kb_ref[0:SEQ, :] = _rope(kx_ref[...], cos_ref[...], sin_ref[...]).astype(BF16)
    kb_ref[SEQ:, :] = kc_ref[...].astype(BF16)
    vb_ref[0:SEQ, 0:LANE] = vx_ref[...].astype(BF16)
    vb_ref[SEQ:, 0:LANE] = vc_ref[...].astype(BF16)
    vb_ref[:, LANE:] = jnp.ones((SEQ + CTX_LEN, LANE), BF16)
    lam = _diff_lambda(lam_ref, lam_init)
    def scores(rows):
        return _diff_scores(_rope(q_ref[rows, :], cos_ref[rows, :], sin_ref[rows, :]), kb_ref)

    chunks = [slice(r, r + QSUB) for r in range(0, SEQ, QSUB)]
    s = scores(chunks[0])
    for c, rows in enumerate(chunks):
        s_next = scores(chunks[c + 1]) if c + 1 < len(chunks) else None
        o_ref[rows, :] = _diff_out(s, vb_ref, lam, g_ref[...], lam_init)
        s = s_next


def _diff_ctx_kernel(lam_init, q_ref, k_ref, v_ref, lam_ref, g_ref, o_ref):
    lam = _diff_lambda(lam_ref, lam_init)
    for h in range(HEADS):
        cols = slice(h * LANE, (h + 1) * LANE)
        o_ref[:, cols] = _diff_chain(q_ref[:, cols], k_ref[:, cols].astype(BF16),
                                     _with_ones(v_ref[:, cols].astype(BF16)), lam, g_ref[...],
                                     lam_init)


def _diff_attn(l, z, cos_t, sin_t, lam_p, subln_g, lam_init, with_ctx, w_f32, w_in_t):
    cq, ck, cv = COL_Q // LANE, COL_K // LANE, COL_V // LANE

    def xs(col):
        return pl.BlockSpec((SEQ, LANE), lambda b, h: (b, col + h))

    def cs(col):
        return pl.BlockSpec((CTX_LEN, LANE), lambda b, h: (CBLK + b, col + h))

    def all_heads(col):
        return pl.BlockSpec((CTX_LEN, GROUP_W), lambda b: (CBLK + b, col // GROUP_W))

    table = pl.BlockSpec((SEQ, LANE), lambda b, h: (0, 0))
    params = [_layer_spec(l, (4, 64)), _layer_spec(l, (1, LANE))]
    return _attn_calls(
        functools.partial(_diff_kernel, lam_init), functools.partial(_diff_ctx_kernel, lam_init),
        "diff_attn",
        [z, z, z, z, z, cos_t, sin_t, lam_p, subln_g],
        [xs(cq), xs(ck), cs(ck), xs(cv), cs(cv), table, table] + params,
        [z, z, z, lam_p, subln_g],
        [all_heads(COL_Q), all_heads(COL_K), all_heads(COL_V)] + params,
        [pltpu.VMEM((SEQ + CTX_LEN, LANE), BF16), pltpu.VMEM((SEQ + CTX_LEN, 2 * LANE), BF16)],
        with_ctx, l, w_f32, _cast_rows_pair_chunks, w_in_t)


def _mla_prep_kernel(ckv_ref, cq_ref, kr_ref, gq_ref, gkv_ref, wq_ref, wkv_ref, cos_ref, sin_ref,
                     qn_ref, qr_ref, kn_ref, v_ref, kr2_ref):
    cos = cos_ref[...]
    sin = sin_ref[...]
    q = jnp.dot(_rms(cq_ref[...], gq_ref[...]).astype(BF16), wq_ref[...],
                preferred_element_type=F32)
    qn_ref[...] = q[:, :GROUP_W].astype(BF16)
    for c in range(0, 2 * LANE, LANE):
        qr_ref[:, c:c + LANE] = _rope(q[:, GROUP_W + c:GROUP_W + c + LANE], cos, sin).astype(BF16)
    kv = jnp.dot(_rms(ckv_ref[...], gkv_ref[...]).astype(BF16), wkv_ref[...],
                 preferred_element_type=F32)
    kn_ref[...] = kv[:, :GROUP_W].astype(BF16)
    v_ref[...] = kv[:, GROUP_W:].astype(BF16)
    kr = _rope(kr_ref[...], cos, sin)
    kr2_ref[...] = (kr + pltpu.roll(kr, ROPE_DIM, 1)).astype(BF16)


def _mla_prep(l, z, gq, gkv, wq, wkv, cos_t, sin_t):
    def out(w):
        return jax.ShapeDtypeStruct((NT, w), BF16)

    def row(w):
        return pl.BlockSpec((TM, w), lambda i: (i, 0))

    rope = pl.BlockSpec((TM, LANE), lambda i: (jnp.where(i * TM < NX, i % (SEQ // TM), SEQ // TM), 0))
    return pl.pallas_call(
        _mla_prep_kernel,
        grid=(NT // TM,),
        in_specs=[
            pl.BlockSpec((TM, MLA_KV_LORA), lambda i: (i, COL_CKV // MLA_KV_LORA)),
            pl.BlockSpec((TM, MLA_Q_LORA), lambda i: (i, COL_CQ // MLA_Q_LORA)),
            pl.BlockSpec((TM, LANE), lambda i: (i, COL_KR // LANE)),
            _layer_spec(l, (1, MLA_Q_LORA)),
            _layer_spec(l, (1, MLA_KV_LORA)),
            _layer_spec(l, (MLA_Q_LORA, HEADS * (MLA_NOPE + ROPE_DIM))),
            _layer_spec(l, (MLA_KV_LORA, HEADS * (MLA_NOPE + MLA_VD))),
            rope, rope,
        ],
        out_specs=[row(GROUP_W), row(2 * LANE), row(GROUP_W), row(GROUP_W), row(LANE)],
        out_shape=[out(GROUP_W), out(2 * LANE), out(GROUP_W), out(GROUP_W), out(LANE)],
        compiler_params=_params(("arbitrary",)),
        name="mla_prep",
    )(z, z, z, gq, gkv, wq, wkv, cos_t, sin_t)


def _mla_scores(h, qn, qr, k):
    lane = lax.broadcasted_iota(jnp.int32, qr.shape, 1)
    lo = (h % 2) * ROPE_DIM
    qr = jnp.where((lane >= lo) & (lane < lo + ROPE_DIM), qr, jnp.zeros_like(qr))
    return _nt_dot(jnp.concatenate([qn, qr], axis=-1), k[...]) * ((MLA_NOPE + ROPE_DIM) ** -0.5)


def _mla_out(s, v):
    o = _softmax_pv(s, v[...])
    return (o[:, :LANE] / o[:, LANE:]).astype(BF16)


def _mla_chain(h, qn, qr, k, v):
    return _mla_out(_mla_scores(h, qn, qr, k), v)


def _mla_kernel(qn_ref, qr_ref, knx_ref, knc_ref, krx_ref, krc_ref, vx_ref, vc_ref,
                o_ref, kb_ref, vb_ref):
    h = pl.program_id(1)
    kb_ref[0:SEQ, 0:LANE] = knx_ref[...]
    kb_ref[SEQ:, 0:LANE] = knc_ref[...]
    kb_ref[0:SEQ, LANE:] = krx_ref[...]
    kb_ref[SEQ:, LANE:] = krc_ref[...]
    vb_ref[0:SEQ, 0:LANE] = vx_ref[...]
    vb_ref[SEQ:, 0:LANE] = vc_ref[...]
    vb_ref[:, LANE:] = jnp.ones((SEQ + CTX_LEN, LANE), BF16)
    def scores(rows):
        return _mla_scores(h, qn_ref[rows, :], qr_ref[rows, :], kb_ref)

    chunks = [slice(r, r + QSUB) for r in range(0, SEQ, QSUB)]
    s = scores(chunks[0])
    for c, rows in enumerate(chunks):
        s_next = scores(chunks[c + 1]) if c + 1 < len(chunks) else None
        o_ref[rows, :] = _mla_out(s, vb_ref)
        s = s_next


def _mla_ctx_kernel(qn_ref, qr_ref, kn_ref, kr_ref, v_ref, o_ref):
    for h in range(HEADS):
        cols = slice(h * LANE, (h + 1) * LANE)
        pair = slice(h // 2 * LANE, (h // 2 + 1) * LANE)
        k = jnp.concatenate([kn_ref[:, cols], kr_ref[...]], axis=-1)
        o_ref[:, cols] = _mla_chain(h, qn_ref[:, cols], qr_ref[:, pair], k,
                                    _with_ones(v_ref[:, cols]))


def _mla_attn(l, qn, qr, kn, v, kr2, with_ctx, w_f32):
    def xs(col):
        return pl.BlockSpec((SEQ, LANE), lambda b, h: (b, col(h)))

    def cs(col):
        return pl.BlockSpec((CTX_LEN, LANE), lambda b, h: (CBLK + b, col(h)))

    def whole(a):
        return pl.BlockSpec((CTX_LEN, a.shape[1]), lambda b: (CBLK + b, 0))

    head, pair, first = (lambda h: h), (lambda h: h // 2), (lambda h: 0)
    return _attn_calls(
        _mla_kernel, _mla_ctx_kernel, "mla_attn",
        [qn, qr, kn, kn, kr2, kr2, v, v],
        [xs(head), xs(pair), xs(head), cs(head), xs(first), cs(first), xs(head), cs(head)],
        [qn, qr, kn, kr2, v],
        [whole(qn), whole(qr), whole(kn), whole(kr2), whole(v)],
        [pltpu.VMEM((SEQ + CTX_LEN, 2 * LANE), BF16), pltpu.VMEM((SEQ + CTX_LEN, 2 * LANE), BF16)],
        with_ctx, l, w_f32, _cast_rows)


def _out_kernel(split, ya_ref, yb_ref, ym_ref, yd_ref, w_ref, x_ref, *refs):
    c_ref, (mod_ref, gpost_ref, xo_ref, wb_ref) = (refs[0], refs[1:]) if split else (None, refs)

    @pl.when(pl.program_id(0) == 0)
    def _():
        wb_ref[...] = w_ref[...].astype(BF16)

    y = jnp.dot(ya_ref[...], wb_ref[0:GROUP_W, :], preferred_element_type=F32)
    y += jnp.dot(yb_ref[...], wb_ref[GROUP_W:2 * GROUP_W, :], preferred_element_type=F32)
    y += jnp.dot(ym_ref[...], wb_ref[2 * GROUP_W:3 * GROUP_W, :], preferred_element_type=F32)
    y += jnp.dot(yd_ref[...], wb_ref[3 * GROUP_W:, :], preferred_element_type=F32)
    delta = mod_ref[2:3, :] * _rms(y, gpost_ref[...])

    def add_residual(src):
        xo_ref[...] = src[...] + delta

    _pick_rows(pl.program_id(0), TM, x_ref, c_ref, add_residual)


def _out_proj(l, ya, yb, ym, yd, w, toks, mods, g_post):
    n = ya.shape[0]
    split = len(toks) == 2
    y_spec = pl.BlockSpec((TM, GROUP_W), lambda i: (i, 0))
    row = pl.BlockSpec((TM, D_MODEL), lambda i: (i, 0))
    return pl.pallas_call(
        functools.partial(_out_kernel, split),
        grid=(n // TM,),
        in_specs=[
            y_spec, y_spec, y_spec, y_spec,
            pl.BlockSpec((None, D_MODEL, D_MODEL), lambda i: (l, 0, 0),
                         pipeline_mode=pl.Buffered(1)),
        ] + (_split_rows_specs(TM, NX) if split else [row]) + [
            _mod_spec(l, TM),
            _layer_spec(l, (1, D_MODEL)),
        ],
        out_specs=row,
        out_shape=jax.ShapeDtypeStruct((n, D_MODEL), F32),
        scratch_shapes=[pltpu.VMEM((D_MODEL, D_MODEL), BF16)],
        compiler_params=_params(("arbitrary",)),
        name="out_proj",
    )(ya, yb, ym, yd, w, *toks, mods, g_post)


def _ffn_kernel(x_ref, xp_ref, xn_ref, mod_ref, gpre_ref, gpost_ref, wgv_ref, cg_ref, cv_ref,
                wd_ref, xo_ref, hx_ref, r_ref):
    i = pl.program_id(0)
    j = pl.program_id(1)
    n = TM_BIG

    @pl.when(j == 0)
    def _():
        slab_shape = (BF16_ROWS, D_MODEL)
        gain = jnp.broadcast_to(gpre_ref[...] * (1.0 + mod_ref[4:5, :]), slab_shape)
        shift = jnp.broadcast_to(mod_ref[3:4, :], slab_shape)

        def hmod(x, r):
            return x * r * gain + shift

        def rinv(x):
            return lax.rsqrt(jnp.mean(x * x, axis=-1, keepdims=True) + NORM_EPS)

        _store_rinv(r_ref, x_ref[...])

        def slab(k, carry):
            rows = pl.ds(pl.multiple_of(k * BF16_ROWS, BF16_ROWS), BF16_ROWS)
            dst = pl.ds(pl.multiple_of((k + 1) * BF16_ROWS, BF16_ROWS), BF16_ROWS)
            hx_ref[dst, :] = hmod(x_ref[rows, :], _load_rinv(r_ref, rows)).astype(BF16)
            return carry

        lax.fori_loop(0, n // BF16_ROWS, slab, 0, unroll=4)
        has_prev, has_next = _tile_edges(i, n)
        halo = jnp.concatenate([xp_ref[...], xn_ref[...]], axis=0)
        keep = jnp.where(lax.broadcasted_iota(jnp.int32, (BF16_ROWS, 1), 0) < SUBLANE,
                         has_prev, has_next)
        halo = hmod(halo, rinv(halo)) * keep
        zeros = jnp.zeros((SUBLANE, D_MODEL), F32)
        hx_ref[:BF16_ROWS, :] = jnp.concatenate([zeros, halo[:SUBLANE]], axis=0).astype(BF16)
        hx_ref[BF16_ROWS + n:, :] = jnp.concatenate([halo[SUBLANE:], zeros], axis=0).astype(BF16)
        xo_ref[...] = jnp.zeros_like(xo_ref)

    is_ctx = i * n >= NX
    same_seq = jnp.where(is_ctx, 0.0, 1.0)
    half = n // 2
    us = [jnp.dot(hx_ref[r0:r0 + half + 2 * BF16_ROWS, :], wgv_ref[...],
                  preferred_element_type=F32) for r0 in (0, half)]
    for part, u in enumerate(us):
        r0 = part * half
        prev_row = u[BF16_ROWS - 1:BF16_ROWS]
        next_row = u[BF16_ROWS + half:BF16_ROWS + half + 1]
        if part == 0:
            next_row = next_row * same_seq
        else:
            prev_row = prev_row * same_seq
        main = u[BF16_ROWS:BF16_ROWS + half]

        def conv(cols, c_ref):
            return _conv3_rows(main[:, cols], prev_row[:, cols], next_row[:, cols], c_ref[...],
                               is_ctx)

        gate = conv(slice(0, TF), cg_ref)
        val = conv(slice(TF, 2 * TF), cv_ref)
        act = (gate * jax.nn.sigmoid(gate) * val).astype(BF16)
        xo_ref[r0:r0 + half, :] += jnp.dot(act, wd_ref[...], preferred_element_type=F32)

    @pl.when(j == pl.num_programs(1) - 1)
    def _():
        _store_rinv(r_ref, xo_ref[...])
        slab_shape = (BF16_ROWS, D_MODEL)
        gain = jnp.broadcast_to(gpost_ref[...] * mod_ref[5:6, :], slab_shape)

        def slab(k, carry):
            rows = pl.ds(pl.multiple_of(k * BF16_ROWS, BF16_ROWS), BF16_ROWS)
            xo_ref[rows, :] = x_ref[rows, :] + xo_ref[rows, :] * _load_rinv(r_ref, rows) * gain
            return carry

        lax.fori_loop(0, n // BF16_ROWS, slab, 0, unroll=4)


def _ffn(l, tok, mods, g_pre, g_post, w_up, conv_w, w_down, n_rows):
    nj = D_FF // TF
    return pl.pallas_call(
        _ffn_kernel,
        grid=(n_rows // TM_BIG, nj),
        in_specs=(_halo_specs(TM_BIG, D_MODEL, n_rows, 0) + [
            _mod_spec(l, TM_BIG),
            _layer_spec(l, (1, D_MODEL)),
            _layer_spec(l, (1, D_MODEL)),
            pl.BlockSpec((D_MODEL, 2 * TF), lambda i, j: (0, j)),
            pl.BlockSpec((None, 3, TF), lambda i, j: (l, 0, j)),
            pl.BlockSpec((None, 3, TF), lambda i, j: (l, 0, j + nj)),
            pl.BlockSpec((TF, D_MODEL), lambda i, j: (j, 0)),
        ]),
        out_specs=pl.BlockSpec((TM_BIG, D_MODEL), lambda i, j: (i, 0)),
        out_shape=jax.ShapeDtypeStruct((n_rows, D_MODEL), F32),
        scratch_shapes=[pltpu.VMEM((TM_BIG + 2 * BF16_ROWS, D_MODEL), BF16),
                        pltpu.VMEM((TM_BIG, LANE), F32)],
        compiler_params=_params(("arbitrary", "arbitrary"), VMEM_LIMIT_BIG),
        name="ffn",
    )(tok, tok, tok, mods, g_pre, g_post, w_up, conv_w, conv_w, w_down)


def _rope_tables():
    rows = SEQ // GRID_W
    row = jnp.repeat(jnp.arange(rows), GRID_W).astype(F32)
    col = jnp.tile(jnp.arange(GRID_W), rows).astype(F32)
    n_freq = ROPE_DIM // 4
    inv = ROPE_THETA ** (-jnp.arange(n_freq, dtype=F32) / n_freq)
    ang = jnp.concatenate([row[:, None] * inv, col[:, None] * inv], axis=-1)
    cos, sin = jnp.cos(ang), jnp.sin(ang)
    cos_t = jnp.concatenate([cos, cos, cos, cos], axis=-1)
    sin_t = jnp.concatenate([-sin, sin, -sin, sin], axis=-1)
    cos_t = jnp.concatenate([cos_t, jnp.ones((TM, LANE), F32)], axis=0)
    sin_t = jnp.concatenate([sin_t, jnp.zeros((TM, LANE), F32)], axis=0)
    return cos_t, sin_t


W_IN_KB = 256


def _w_in_kernel(w_ref, o_ref):
    dst_of = {"pool_qkv": COL_A, "conv": COL_DB, "c_kv": COL_CKV, "c_q": COL_CQ, "k_rope": COL_KR}
    src = 0
    for name, n in REF_SEGMENTS:
        dst = dst_of[name]
        o_ref[dst:dst + n, :] = w_ref[src:src + n, :].astype(BF16)
        src += n
    end = COL_KR + ROPE_DIM
    o_ref[end:, :] = jnp.zeros((IN_COLS_PAD - end, o_ref.shape[1]), BF16)


def _layout_w_in(wt):
    n, k = wt.shape[1:]
    return pl.pallas_call(
        _w_in_kernel,
        grid=(k // W_IN_KB,),
        in_specs=[pl.BlockSpec((None, n, W_IN_KB), lambda i: (0, 0, i))],
        out_specs=pl.BlockSpec((IN_COLS_PAD, W_IN_KB), lambda i: (0, i)),
        out_shape=jax.ShapeDtypeStruct((IN_COLS_PAD, k), BF16),
        compiler_params=_params(("arbitrary",)),
        name="w_in_layout",
    )(wt)


def _layout_w_uq(w):
    w = w.reshape(DEPTH, MLA_Q_LORA, HEADS, MLA_NOPE + ROPE_DIM)
    return jnp.concatenate([w[..., :MLA_NOPE].reshape(DEPTH, MLA_Q_LORA, HEADS * MLA_NOPE),
                            w[..., MLA_NOPE:].reshape(DEPTH, MLA_Q_LORA, HEADS * ROPE_DIM)],
                           axis=-1).astype(BF16)


def _layout_w_ukv(w):
    w = w.reshape(DEPTH, MLA_KV_LORA, HEADS, MLA_NOPE + MLA_VD)
    return jnp.concatenate([w[..., :MLA_NOPE].reshape(DEPTH, MLA_KV_LORA, HEADS * MLA_NOPE),
                            w[..., MLA_NOPE:].reshape(DEPTH, MLA_KV_LORA, HEADS * MLA_VD)],
                           axis=-1).astype(BF16)


def kernel(x, c, ctx, c_ctx, ada_w, ada_b, g_pre_mix, g_post_mix, g_pre_ffn, g_post_ffn, w_in,
           pool_w, pool_scale, diff_lambda, diff_subln_g, mla_gq, mla_w_uq, mla_gkv, mla_w_ukv,
           conv_w, w_out, ffn_w_up, ffn_conv_w, ffn_w_down):
    toks = (x.reshape(NX, D_MODEL), ctx.reshape(NC, D_MODEL))
    cc = jnp.concatenate([c, c_ctx[None, :], jnp.zeros((3, D_MODEL), F32)], axis=0)
    mods = _ada(cc, ada_w, ada_b).reshape(DEPTH, 8, 6, D_MODEL)
    cos_t, sin_t = _rope_tables()

    def vec(p):
        return p.reshape(DEPTH, 1, p.shape[-1])

    w_in_t = jnp.swapaxes(w_in, 1, 2)
    w_in_b = _layout_w_in(w_in_t)
    w_uq_b, w_ukv_b = _layout_w_uq(mla_w_uq), _layout_w_ukv(mla_w_ukv)
    pool_w_b = pool_w.astype(BF16)

    for l in range(DEPTH):
        last = l == DEPTH - 1
        n_rows = NX if last else NT
        lam_init = 0.8 - 0.6 * math.exp(-0.3 * l)

        z = _in_proj(l, toks, mods, vec(g_pre_mix), w_in_b)
        ya, yd = _pool_conv(l, z, n_rows, pool_w_b, vec(pool_scale), conv_w)
        yb, (w_up_b, *w_in_next) = _diff_attn(l, z, cos_t, sin_t, diff_lambda, vec(diff_subln_g),
                                              lam_init, not last, ffn_w_up,
                                              None if last else w_in_t)
        qn, qr, kn, v, kr2 = _mla_prep(l, z, vec(mla_gq), vec(mla_gkv), w_uq_b, w_ukv_b, cos_t, sin_t)
        ym, (w_down_b,) = _mla_attn(l, qn, qr, kn, v, kr2, not last, ffn_w_down)
        tok = _out_proj(l, ya, yb, ym, yd, w_out, toks, mods, vec(g_post_mix))
        tok = _ffn(l, tok, mods, vec(g_pre_ffn), vec(g_post_ffn), w_up_b, ffn_conv_w, w_down_b, n_rows)
        toks = (tok,)
        if w_in_next:
            w_in_b = w_in_next[0]
    return tok.reshape(BATCH, SEQ, D_MODEL)
```
